```python
import math
import jax, jax.numpy as jnp
from jax import lax
import numpy as np

D_MODEL = 1024
BATCH = 4
SEQ = 4096
DEPTH = 1

CHUNK = 64
EPS = 1e-6
GDN_HEADS = 8
GDN_DK = 128
GDN_DV = 128
GDN_QK = GDN_HEADS * GDN_DK
GDN_V = GDN_HEADS * GDN_DV
CONV_W = 4
RWKV_HEAD = 64
RWKV_WIDTH = D_MODEL
RWKV_HEADS = RWKV_WIDTH // RWKV_HEAD
DECAY_LORA = 64
AAA_LORA = 64
GATE_LORA = 128
RWKV_GN_EPS = 64e-5
GDN_QKV_COLS = 2 * GDN_QK + GDN_V
RWKV_COLS = 3 * RWKV_WIDTH + DECAY_LORA + AAA_LORA + GATE_LORA
IN_WIDTHS = (GDN_QKV_COLS, GDN_V, GDN_HEADS, GDN_HEADS, RWKV_COLS, D_MODEL, D_MODEL)
N_IN_COLS = GDN_QKV_COLS + GDN_V + 2 * GDN_HEADS + RWKV_COLS + 2 * D_MODEL
N_EXPERTS = 32
TOP_K = 4
D_FF = D_MODEL
SWIGLU_ALPHA = 1.702
SWIGLU_LIMIT = 7.0
MOE_BLOCK = 128

kernel_name = "hybrid_gdn_rwkv7_moe_block"


def _split(t, widths):
    offs = [int(o) for o in np.cumsum(widths)[:-1]]
    return jnp.split(t, offs, axis=-1)


def _rmsnorm(x, w, eps=EPS):
    xf = x.astype(jnp.float32)
    y = xf * lax.rsqrt(jnp.mean(xf * xf, axis=-1, keepdims=True) + eps)
    return (y * w.astype(jnp.float32)).astype(x.dtype)


def _l2norm(x):
    xf = x.astype(jnp.float32)
    return xf * lax.rsqrt(jnp.sum(xf * xf, axis=-1, keepdims=True) + EPS)


def _shift1(t):
    return jnp.pad(t, ((0, 0), (1, 0), (0, 0)))[:, :-1]


def _causal_dwconv(x, w):
    S = x.shape[1]
    xp = jnp.pad(x, ((0, 0), (CONV_W - 1, 0), (0, 0)))
    y = w[0] * xp[:, 0:S]
    for j in range(1, CONV_W):
        y = y + w[j] * xp[:, j:j + S]
    return y


def _gated_delta_chunked(q, k, v, g, beta):
    f32 = jnp.float32
    Bsz, S, H, _ = q.shape
    nc = S // CHUNK

    def to_chunks(t):
        return t.astype(f32).reshape(Bsz, nc, CHUNK, H, -1).transpose(0, 3, 1, 2, 4)

    q, k, v = to_chunks(q), to_chunks(k), to_chunks(v)
    g = g.astype(f32).reshape(Bsz, nc, CHUNK, H).transpose(0, 3, 1, 2)
    beta = beta.astype(f32).reshape(Bsz, nc, CHUNK, H).transpose(0, 3, 1, 2)
    gc = jnp.cumsum(g, axis=-1)
    idx = jnp.arange(CHUNK)
    incl = idx[:, None] >= idx[None, :]
    strict = idx[:, None] > idx[None, :]
    diff = gc[..., :, None] - gc[..., None, :]
    decay = jnp.where(incl, jnp.exp(jnp.where(incl, diff, 0.0)), 0.0)
    k_beta = k * beta[..., None]
    m = jnp.where(strict, jnp.einsum('bhnik,bhnjk->bhnij', k_beta, k) * decay, 0.0)
    eye = jnp.eye(CHUNK, dtype=f32)
    t_inv = lax.linalg.triangular_solve(eye + m, jnp.broadcast_to(eye, m.shape),
                                        left_side=True, lower=True, unit_diagonal=True)
    u = jnp.matmul(t_inv, v * beta[..., None])
    w = jnp.matmul(t_inv, k_beta * jnp.exp(gc)[..., None])
    attn = jnp.einsum('bhnik,bhnjk->bhnij', q, k) * decay
    q_dec = q * jnp.exp(gc)[..., None]
    k_dec = k * jnp.exp(gc[..., -1:] - gc)[..., None]
    g_last = jnp.exp(gc[..., -1])

    def step(state, inp):
        u_c, w_c, a_c, qd_c, kd_c, gl_c = inp
        v_new = u_c - jnp.matmul(w_c, state)
        o = jnp.matmul(qd_c, state) + jnp.matmul(a_c, v_new)
        state = state * gl_c[..., None, None] + jnp.einsum('bhck,bhcv->bhkv', kd_c, v_new)
        return state, o

    xs = tuple(jnp.moveaxis(t, 2, 0) for t in (u, w, attn, q_dec, k_dec, g_last))
    s0 = jnp.zeros((Bsz, H, q.shape[-1], v.shape[-1]), f32)
    _, o = lax.scan(step, s0, xs)
    return o.transpose(1, 0, 3, 2, 4).reshape(Bsz, S, H, -1)


def _gdn_branch(qkv, z, b, a, conv_w, A_log, dt_bias, norm_w):
    Bsz, S, _ = qkv.shape
    f32 = jnp.float32
    qkv = jax.nn.silu(_causal_dwconv(qkv, conv_w))
    q, k, v = _split(qkv, (GDN_QK, GDN_QK, GDN_V))
    q = _l2norm(q.reshape(Bsz, S, GDN_HEADS, GDN_DK)) * (GDN_DK ** -0.5)
    k = _l2norm(k.reshape(Bsz, S, GDN_HEADS, GDN_DK))
    v = v.reshape(Bsz, S, GDN_HEADS, GDN_DV)
    beta = jax.nn.sigmoid(b.astype(f32))
    g = -jnp.exp(A_log.astype(f32)) * jax.nn.softplus(a.astype(f32) + dt_bias.astype(f32))
    o = _gated_delta_chunked(q, k, v, g, beta)
    o = _rmsnorm(o, norm_w) * jax.nn.silu(z.reshape(Bsz, S, GDN_HEADS, GDN_DV).astype(f32))
    return o.reshape(Bsz, S, GDN_V)


def _rwkv7_scan(r, w, k, v, kk, a):
    f32 = jnp.float32
    Bsz, _, H, N = r.shape
    xs = tuple(jnp.moveaxis(t.astype(f32), 1, 0) for t in (r, w, k, v, kk, a))

    def step(state, inp):
        r_t, w_t, k_t, v_t, kk_t, a_t = inp
        sk = jnp.einsum('bhvk,bhk->bhv', state, kk_t)
        state = (state * w_t[:, :, None, :]
                 - sk[..., None] * (kk_t * a_t)[:, :, None, :]
                 + v_t[..., None] * k_t[:, :, None, :])
        return state, jnp.einsum('bhvk,bhk->bhv', state, r_t)

    _, o = lax.scan(step, jnp.zeros((Bsz, H, N, N), f32), xs)
    return jnp.moveaxis(o, 0, 1)


def _rwkv7_branch(p, mu, w0, w2, a0, a2, g2, k_k, k_a, r_k, ln_w, ln_b):
    Bsz, S, _ = p.shape
    f32 = jnp.float32
    p = p.astype(f32)
    p = p + (_shift1(p) - p) * mu
    r, k, v, xw, xa, xg = _split(p, (RWKV_WIDTH, RWKV_WIDTH, RWKV_WIDTH, DECAY_LORA, AAA_LORA, GATE_LORA))
    w_log = -jax.nn.softplus(-(w0 + jnp.tanh(xw) @ w2)) - 0.5
    decay = jnp.exp(-jnp.exp(w_log))
    aa = jax.nn.sigmoid(a0 + xa @ a2)
    gg = jax.nn.sigmoid(xg) @ g2
    hs = lambda t: t.reshape(Bsz, S, RWKV_HEADS, RWKV_HEAD)
    kk = _l2norm(hs(k * k_k))
    k = k * (1.0 + (aa - 1.0) * k_a)
    r_h, k_h, v_h, aa_h = hs(r), hs(k), hs(v), hs(aa)
    o = _rwkv7_scan(r_h, hs(decay), k_h, v_h, kk, aa_h)
    mean = jnp.mean(o, axis=-1, keepdims=True)
    var = jnp.mean(jnp.square(o - mean), axis=-1, keepdims=True)
    o = ((o - mean) * lax.rsqrt(var + RWKV_GN_EPS)).reshape(Bsz, S, RWKV_WIDTH) * ln_w + ln_b
    bonus = jnp.sum(r_h * k_h * r_k, axis=-1, keepdims=True) * v_h
    return (o + bonus.reshape(Bsz, S, RWKV_WIDTH)) * gg


def _clamped_swiglu(gu):
    glu, lin = gu[..., ::2], gu[..., 1::2]
    glu = jnp.minimum(glu, SWIGLU_LIMIT)
    lin = jnp.clip(lin, -SWIGLU_LIMIT, SWIGLU_LIMIT)
    return glu * jax.nn.sigmoid(SWIGLU_ALPHA * glu) * (lin + 1.0)


def _moe(h, router_w, router_b, w_gu, b_gu, w_down, b_down):
    Bsz, S, D = h.shape
    f32 = jnp.float32
    xt = h.reshape(-1, D)
    T = xt.shape[0]
    logits = (xt @ router_w + router_b).astype(f32)
    top_val, top_idx = lax.top_k(logits, TOP_K)
    gate = jax.nn.softmax(top_val, axis=-1)
    A = T * TOP_K
    flat_e = top_idx.reshape(A)
    flat_tok = jnp.arange(A, dtype=jnp.int32) // TOP_K
    flat_gate = gate.reshape(A)
    order = jnp.argsort(flat_e)
    sorted_e = flat_e[order]
    counts = jnp.bincount(flat_e, length=N_EXPERTS)
    padded = ((counts + MOE_BLOCK - 1) // MOE_BLOCK) * MOE_BLOCK
    start_sorted = jnp.cumsum(counts) - counts
    end_padded = jnp.cumsum(padded)
    start_padded = end_padded - padded
    dest = start_padded[sorted_e] + (jnp.arange(A, dtype=jnp.int32) - start_sorted[sorted_e])
    n_blocks = -(-A // MOE_BLOCK) + N_EXPERTS
    P = n_blocks * MOE_BLOCK
    slot_tok = jnp.zeros((P,), jnp.int32).at[dest].set(flat_tok[order])
    slot_gate = jnp.zeros((P,), f32).at[dest].set(flat_gate[order])
    block_start = jnp.arange(n_blocks, dtype=jnp.int32) * MOE_BLOCK
    block_expert = jnp.minimum(jnp.searchsorted(end_padded, block_start, side='right'), N_EXPERTS - 1)
    xb = xt[slot_tok].reshape(n_blocks, MOE_BLOCK, D)

    def expert_block(args):
        xblk, e = args
        gu = xblk @ w_gu[e] + b_gu[e]
        return _clamped_swiglu(gu) @ w_down[e] + b_down[e]

    yb = lax.map(expert_block, (xb, block_expert)).reshape(P, D)
    y = jax.ops.segment_sum(yb.astype(f32) * slot_gate[:, None], slot_tok, num_segments=T)
    return y.reshape(Bsz, S, D).astype(h.dtype)


def setup_inputs(seed: int = 0) -> dict:
    key = jax.random.key(seed)
    ks = jax.random.split(key, 32)
    f32 = jnp.float32
    L, D = DEPTH, D_MODEL
    nrm = lambda k, shape, s: jax.random.normal(k, shape, f32) * s
    dt = jnp.exp(jax.random.uniform(ks[4], (L, GDN_HEADS), f32, math.log(1e-3), math.log(1e-1)))
    w0_base = jnp.linspace(-6.0, -1.0, RWKV_WIDTH, dtype=f32)
    return {
        "x": nrm(ks[0], (BATCH, SEQ, D), 1.0),
        "norm_mix": 1.0 + nrm(ks[1], (L, D), 0.1),
        "w_in": nrm(ks[2], (L, D, N_IN_COLS), D ** -0.5),
        "gdn_conv": nrm(ks[3], (L, CONV_W, GDN_QKV_COLS), CONV_W ** -0.5),
        "gdn_A_log": jnp.log(jax.random.uniform(ks[5], (L, GDN_HEADS), f32, 1.0, 16.0)),
        "gdn_dt_bias": dt + jnp.log(-jnp.expm1(-dt)),
        "gdn_norm": 1.0 + nrm(ks[6], (L, GDN_DV), 0.1),
        "rwkv_mu": jax.random.uniform(ks[7], (L, RWKV_COLS), f32),
        "rwkv_w0": w0_base + nrm(ks[8], (L, RWKV_WIDTH), 0.1),
        "rwkv_w2": nrm(ks[9], (L, DECAY_LORA, RWKV_WIDTH), 0.1),
        "rwkv_a0": nrm(ks[10], (L, RWKV_WIDTH), 0.1),
        "rwkv_a2": nrm(ks[11], (L, AAA_LORA, RWKV_WIDTH), AAA_LORA ** -0.5),
        "rwkv_g2": nrm(ks[12], (L, GATE_LORA, RWKV_WIDTH), GATE_LORA ** -0.5),
        "rwkv_k_k": 0.85 + nrm(ks[13], (L, RWKV_WIDTH), 0.05),
        "rwkv_k_a": 1.0 + nrm(ks[14], (L, RWKV_WIDTH), 0.05),
        "rwkv_r_k": nrm(ks[15], (L, RWKV_HEADS, RWKV_HEAD), 0.1),
        "rwkv_ln_w": 1.0 + nrm(ks[16], (L, RWKV_WIDTH), 0.1),
        "rwkv_ln_b": nrm(ks[17], (L, RWKV_WIDTH), 0.02),
        "proj_a": nrm(ks[18], (L, GDN_V, D), GDN_V ** -0.5),
        "proj_b": nrm(ks[19], (L, RWKV_WIDTH, D), RWKV_WIDTH ** -0.5),
        "w_out": nrm(ks[20], (L, D, D), D ** -0.5),
        "norm_ffn": 1.0 + nrm(ks[21], (L, D), 0.1),
        "router_w": nrm(ks[22], (L, D, N_EXPERTS), D ** -0.5),
        "router_b": nrm(ks[23], (L, N_EXPERTS), 0.01),
        "w_gate_up": nrm(ks[24], (L, N_EXPERTS, D, 2 * D_FF), D ** -0.5),
        "b_gate_up": nrm(ks[25], (L, N_EXPERTS, 2 * D_FF), 0.02),
        "w_down": nrm(ks[26], (L, N_EXPERTS, D_FF, D), D_FF ** -0.5),
        "b_down": nrm(ks[27], (L, N_EXPERTS, D), 0.02),
        "norm_final": 1.0 + nrm(ks[28], (D,), 0.1),
    }


def reference(x, norm_mix, w_in, gdn_conv, gdn_A_log, gdn_dt_bias, gdn_norm,
              rwkv_mu, rwkv_w0, rwkv_w2, rwkv_a0, rwkv_a2, rwkv_g2, rwkv_k_k, rwkv_k_a,
              rwkv_r_k, rwkv_ln_w, rwkv_ln_b, proj_a, proj_b, w_out, norm_ffn,
              router_w, router_b, w_gate_up, b_gate_up, w_down, b_down, norm_final):
    for l in range(DEPTH):
        h = _rmsnorm(x, norm_mix[l])
        p = h @ w_in[l]
        gdn_qkv, gdn_z, gdn_b, gdn_a, rwkv_p, gate_a, gate_b = _split(p, IN_WIDTHS)
        y_a = _gdn_branch(gdn_qkv, gdn_z, gdn_b, gdn_a, gdn_conv[l], gdn_A_log[l],
                          gdn_dt_bias[l], gdn_norm[l])
        y_b = _rwkv7_branch(rwkv_p, rwkv_mu[l], rwkv_w0[l], rwkv_w2[l], rwkv_a0[l], rwkv_a2[l],
                            rwkv_g2[l], rwkv_k_k[l], rwkv_k_a[l], rwkv_r_k[l],
                            rwkv_ln_w[l], rwkv_ln_b[l])
        merged = (jax.nn.sigmoid(gate_a.astype(jnp.float32)) * (y_a @ proj_a[l])
                  + jax.nn.sigmoid(gate_b.astype(jnp.float32)) * (y_b @ proj_b[l]))
        x = x + (merged @ w_out[l]).astype(x.dtype)
        h = _rmsnorm(x, norm_ffn[l])
        x = x + _moe(h, router_w[l], router_b[l], w_gate_up[l], b_gate_up[l], w_down[l], b_down[l])
    return _rmsnorm(x, norm_final)
```

```python
import functools

import jax
import jax.numpy as jnp
import numpy as np
from jax import lax
from jax.experimental import pallas as pl
from jax.experimental.pallas import tpu as pltpu

F32 = jnp.float32
BF16 = jnp.bfloat16
I32 = jnp.int32
HIGHEST = lax.Precision.HIGHEST

D_MODEL = 1024
EPS = 1e-6
CHUNK = 64
GDN_HEADS = 8
GDN_DK = 128
CONV_W = 4
RWKV_HEADS = 16
RWKV_N = 64
RWKV_GN_EPS = 64e-5
LORA_COLS = 256
N_EXPERTS = 32
TOP_K = 4
SWIGLU_ALPHA = 1.702
SWIGLU_LIMIT = 7.0

PACK_COLS = 9600
COL_BA = 9472
HEADS_PER_GROUP = 4
PACK_W = HEADS_PER_GROUP * CHUNK

SLOT_BLOCK = 128
EXPERT_BLOCK = 512
VMEM_LIMIT = 48 * 1024 * 1024


def _mm(a, b):
    return jnp.dot(a.astype(BF16), b.astype(BF16), preferred_element_type=F32)


def _mm_nt(a, b):
    return lax.dot_general(a.astype(BF16), b.astype(BF16), (((1,), (1,)), ((), ())),
                           preferred_element_type=F32)


def _mm_tn(a, b):
    return lax.dot_general(a.astype(BF16), b.astype(BF16), (((0,), (0,)), ((), ())),
                           preferred_element_type=F32)


def _mm_hi(a, b):
    return jnp.dot(a, b, preferred_element_type=F32, precision=HIGHEST)


def _iota(shape, dim):
    return lax.broadcasted_iota(I32, shape, dim)


def _softplus(x):
    return jnp.maximum(x, 0.0) + jnp.log(1.0 + jnp.exp(-jnp.abs(x)))


def _stack_rows(x, n):
    return jnp.concatenate([x] * n, axis=0)


def _tri_inv_packed(m, eye, bdmask):
    def bd(x):
        return jnp.where(bdmask, _stack_rows(x, HEADS_PER_GROUP), 0.0)

    t = eye - m
    x = _mm_hi(m, bd(m))
    for _ in range(4):
        r = _mm_hi(jnp.concatenate([t, x], axis=0), bd(x))
        t = t + r[:CHUNK]
        x = r[CHUNK:]
    return t + _mm_hi(t, bd(x))


def _shift_rows(x, prev8, k, row8):
    r = pltpu.roll(x, k, 0)
    pr = pltpu.roll(prev8, k, 0)
    head = jnp.where(row8 < k, pr, r[:8])
    return jnp.concatenate([head, r[8:]], axis=0)


def _in_proj_kernel(x_ref, nw_ref, w_ref, o_ref, h_ref):
    @pl.when(pl.program_id(1) == 0)
    def _():
        x = x_ref[...]
        y = x * lax.rsqrt(jnp.mean(x * x, axis=-1, keepdims=True) + EPS)
        h_ref[...] = (y * nw_ref[...]).astype(BF16)

    o_ref[...] = jnp.dot(h_ref[...], w_ref[...], preferred_element_type=F32)


def _in_proj(x2, norm_w, w_pack):
    t = x2.shape[0]
    tm = min(512, t)
    tn = 1920
    return pl.pallas_call(
        _in_proj_kernel,
        grid=(t // tm, PACK_COLS // tn),
        in_specs=[pl.BlockSpec((tm, D_MODEL), lambda i, j: (i, 0)),
                  pl.BlockSpec((1, D_MODEL), lambda i, j: (0, 0)),
                  pl.BlockSpec((D_MODEL, tn), lambda i, j: (0, j))],
        out_specs=pl.BlockSpec((tm, tn), lambda i, j: (i, j)),
        out_shape=jax.ShapeDtypeStruct((t, PACK_COLS), F32),
        scratch_shapes=[pltpu.VMEM((tm, D_MODEL), BF16)],
        compiler_params=pltpu.CompilerParams(dimension_semantics=("parallel", "arbitrary"),
                                             vmem_limit_bytes=VMEM_LIMIT),
        name="in_proj",
    )(x2, norm_w.reshape(1, D_MODEL), w_pack)


def _gdn_kernel(q_ref, k_ref, v_ref, z_ref, ba_ref, conv_ref, alog_ref, dtb_ref, nw_ref, o_ref,
                carry_ref, qn_ref, kn_ref, vv_ref, beta_ref, g_ref, s_ref, *, ts):
    @pl.when(pl.program_id(1) == 0)
    def _():
        carry_ref[...] = jnp.zeros_like(carry_ref)
        s_ref[...] = jnp.zeros_like(s_ref)

    row8 = _iota((8, D_MODEL), 0)

    def conv_silu(idx, x_ref):
        x = x_ref[...]
        prev8 = carry_ref[idx]
        w4 = conv_ref[:, idx * D_MODEL:(idx + 1) * D_MODEL]
        y = x * w4[CONV_W - 1:CONV_W]
        for k in range(1, CONV_W):
            y = y + _shift_rows(x, prev8, k, row8) * w4[CONV_W - 1 - k:CONV_W - k]
        carry_ref[idx] = x[ts - 8:ts]
        return y * jax.nn.sigmoid(y)

    def l2norm_heads(x, scale):
        parts = []
        for h in range(GDN_HEADS):
            xh = x[:, h * GDN_DK:(h + 1) * GDN_DK]
            parts.append(xh * (lax.rsqrt(jnp.sum(xh * xh, axis=-1, keepdims=True) + EPS) * scale))
        return jnp.concatenate(parts, axis=1)

    qn_ref[...] = l2norm_heads(conv_silu(0, q_ref), GDN_DK ** -0.5)
    kn_ref[...] = l2norm_heads(conv_silu(1, k_ref), 1.0)
    vv_ref[...] = conv_silu(2, v_ref)
    ba = ba_ref[...]
    beta_ref[...] = jax.nn.sigmoid(ba)
    g_ref[...] = -jnp.exp(alog_ref[...]) * _softplus(ba + dtb_ref[...])

    gw = HEADS_PER_GROUP * GDN_DK
    lane_p = _iota((CHUNK, PACK_W), 1)
    row_p = _iota((CHUNK, PACK_W), 0)
    eye = jnp.where((lane_p % CHUNK) == row_p, 1.0, 0.0).astype(F32)
    strict = (lane_p % CHUNK) < row_p
    incl = (lane_p % CHUNK) <= row_p
    bd_pp = (_iota((PACK_W, PACK_W), 0) // CHUNK) == (_iota((PACK_W, PACK_W), 1) // CHUNK)
    bd_pn = (_iota((PACK_W, gw), 0) // CHUNK) == (_iota((PACK_W, gw), 1) // GDN_DK)
    bd_pn2 = (_iota((PACK_W, 2 * gw), 0) // CHUNK) == ((_iota((PACK_W, 2 * gw), 1) % gw) // GDN_DK)
    ltri = jnp.where(_iota((CHUNK, CHUNK), 1) <= _iota((CHUNK, CHUNK), 0), 1.0, 0.0).astype(F32)
    ones_cc = jnp.ones((CHUNK, CHUNK), F32)
    lane128 = _iota((CHUNK, 128), 1)
    nw = nw_ref[...]

    def chunk_body(c, carry):
        r0 = pl.multiple_of(c * CHUNK, CHUNK)
        rows = pl.ds(r0, CHUNK)
        beta_c = beta_ref[rows, :]
        gcs = _mm_hi(ltri, g_ref[rows, :])
        for gi in range(GDN_HEADS // HEADS_PER_GROUP):
            heads = [gi * HEADS_PER_GROUP + j for j in range(HEADS_PER_GROUP)]
            cols = slice(gi * gw, (gi + 1) * gw)
            qn = qn_ref[rows, cols]
            kn = kn_ref[rows, cols]
            vv = vv_ref[rows, cols]
            beta_nat = jnp.concatenate(
                [jnp.broadcast_to(beta_c[:, h:h + 1], (CHUNK, GDN_DK)) for h in heads], axis=1)
            gcol = [jnp.broadcast_to(gcs[:, 8 + h:9 + h], (CHUNK, GDN_DK)) for h in heads]
            gc_nat = jnp.concatenate(gcol, axis=1)
            gcol_p = jnp.concatenate([jnp.where(lane128 < CHUNK, gcol[0], gcol[1]),
                                      jnp.where(lane128 < CHUNK, gcol[2], gcol[3])], axis=1)
            grow_p = _mm_hi(ones_cc, gcol_p * eye)
            decay_p = jnp.exp(jnp.minimum(gcol_p - grow_p, 0.0))
            kb = kn * beta_nat
            bdk = jnp.where(bd_pn, _stack_rows(kn, HEADS_PER_GROUP), 0.0)
            aa = _mm_nt(jnp.concatenate([kb, qn], axis=0), bdk)
            m = jnp.where(strict, aa[:CHUNK] * decay_p, 0.0)
            attn = jnp.where(incl, aa[CHUNK:] * decay_p, 0.0)
            t_inv = _tri_inv_packed(m, eye, bd_pp)
            egc = jnp.exp(gc_nat)
            data = jnp.concatenate([vv * beta_nat, kb * egc], axis=1)
            uw = _mm(t_inv, jnp.where(bd_pn2, _stack_rows(data, HEADS_PER_GROUP), 0.0))
            qd = qn * egc
            gl = gc_nat[CHUNK - 1:CHUNK, :]
            kd = kn * jnp.exp(gl - gc_nat)
            egl = jnp.exp(gl)
            vnew, o1 = [], []
            for j, h in enumerate(heads):
                hc = slice(j * GDN_DK, (j + 1) * GDN_DK)
                state = s_ref[h]
                wq = _mm(jnp.concatenate([uw[:, gw + j * GDN_DK:gw + (j + 1) * GDN_DK], qd[:, hc]],
                                         axis=0), state)
                vn = uw[:, hc] - wq[:CHUNK]
                vnew.append(vn)
                o1.append(wq[CHUNK:])
                s_ref[h] = state * egl[:, hc] + _mm_tn(kd[:, hc], vn)
            vnew = jnp.concatenate(vnew, axis=1)
            o = jnp.concatenate(o1, axis=1) + _mm(
                attn, jnp.where(bd_pn, _stack_rows(vnew, HEADS_PER_GROUP), 0.0))
            for j, h in enumerate(heads):
                hc = slice(j * GDN_DK, (j + 1) * GDN_DK)
                oh = o[:, hc]
                oh = oh * lax.rsqrt(jnp.mean(oh * oh, axis=-1, keepdims=True) + EPS) * nw
                zh = z_ref[rows, h * GDN_DK:(h + 1) * GDN_DK]
                o_ref[rows, h * GDN_DK:(h + 1) * GDN_DK] = (oh * (zh * jax.nn.sigmoid(zh))).astype(BF16)
        return carry

    lax.fori_loop(0, ts // CHUNK, chunk_body, 0)


def _gdn(p, conv_w, a_log, dt_bias, norm_w, bsz, seq):
    ts = min(256, seq)
    ns = seq // ts
    pad8 = lambda v: jnp.zeros((1, 128), F32).at[0, 8:16].set(v)
    tok = lambda col: pl.BlockSpec((ts, D_MODEL), lambda b, s, col=col: (b * ns + s, col))
    const = lambda shape: pl.BlockSpec(shape, lambda b, s: (0,) * len(shape))
    return pl.pallas_call(
        functools.partial(_gdn_kernel, ts=ts),
        grid=(bsz, ns),
        in_specs=[tok(0), tok(1), tok(2), tok(3),
                  pl.BlockSpec((ts, 128), lambda b, s: (b * ns + s, COL_BA // 128)),
                  const((CONV_W, 3 * D_MODEL)), const((1, 128)), const((1, 128)), const((1, GDN_DK))],
        out_specs=pl.BlockSpec((ts, D_MODEL), lambda b, s: (b * ns + s, 0)),
        out_shape=jax.ShapeDtypeStruct((bsz * seq, D_MODEL), BF16),
        scratch_shapes=[pltpu.VMEM((3, 8, D_MODEL), F32),
                        pltpu.VMEM((ts, D_MODEL), F32), pltpu.VMEM((ts, D_MODEL), F32),
                        pltpu.VMEM((ts, D_MODEL), F32),
                        pltpu.VMEM((ts, 128), F32), pltpu.VMEM((ts, 128), F32),
                        pltpu.VMEM((GDN_HEADS, GDN_DK, GDN_DK), F32)],
        compiler_params=pltpu.CompilerParams(dimension_semantics=("parallel", "arbitrary"),
                                             vmem_limit_bytes=VMEM_LIMIT),
        name="gdn",
    )(p, p, p, p, p, conv_w, pad8(a_log), pad8(dt_bias), norm_w.reshape(1, GDN_DK))


def _rwkv_kernel(r_ref, k_ref, v_ref, lo_ref, mu_ref, w0_ref, w2_ref, a0_ref, a2_ref, g2_ref,
                 kk_ref, ka_ref, rk_ref, lnw_ref, lnb_ref, seg_ref, o_ref,
                 carry_ref, rs_ref, ks_ref, vs_ref, kks_ref, bs_ref, ld_ref, gg_ref, os_ref, st_ref,
                 *, ts):
    @pl.when(pl.program_id(1) == 0)
    def _():
        carry_ref[...] = jnp.zeros_like(carry_ref)
        st_ref[...] = jnp.zeros_like(st_ref)

    width = D_MODEL
    row8 = _iota((8, width), 0)
    row8l = _iota((8, LORA_COLS), 0)

    def seg_sum(x):
        hi = x.astype(BF16)
        lo = (x - hi.astype(F32)).astype(BF16)
        parts = []
        for g in range(width // PACK_W):
            cs = slice(g * PACK_W, (g + 1) * PACK_W)
            parts.append(jnp.dot(hi[:, cs], seg_ref[...], preferred_element_type=F32)
                         + jnp.dot(lo[:, cs], seg_ref[...], preferred_element_type=F32))
        return jnp.concatenate(parts, axis=1)

    def lerp(idx, x_ref, mu, r8):
        x = x_ref[...]
        xs = x + (_shift_rows(x, carry_ref[idx, :, :x.shape[1]], 1, r8) - x) * mu
        carry_ref[idx, :, :x.shape[1]] = x[ts - 8:ts]
        return xs

    r = lerp(0, r_ref, mu_ref[:, 0:width], row8)
    k = lerp(1, k_ref, mu_ref[:, width:2 * width], row8)
    v = lerp(2, v_ref, mu_ref[:, 2 * width:3 * width], row8)
    lo = lerp(3, lo_ref, mu_ref[:, 3 * width:3 * width + LORA_COLS], row8l)
    lo_a = lo[:, :128]
    w_log = -_softplus(-(w0_ref[...] + _mm(jnp.tanh(lo_a), w2_ref[...]))) - 0.5
    ld_ref[...] = -jnp.exp(w_log)
    aa = jax.nn.sigmoid(a0_ref[...] + _mm(lo_a, a2_ref[...]))
    gg_ref[...] = _mm(jax.nn.sigmoid(lo[:, 128:]), g2_ref[...])
    kx = k * kk_ref[...]
    kk = kx * lax.rsqrt(seg_sum(kx * kx) + EPS)
    k = k * (1.0 + (aa - 1.0) * ka_ref[...])
    rs_ref[...] = r
    ks_ref[...] = k
    vs_ref[...] = v
    kks_ref[...] = kk
    bs_ref[...] = kk * aa

    lane_p = _iota((CHUNK, PACK_W), 1)
    row_p = _iota((CHUNK, PACK_W), 0)
    eye = jnp.where((lane_p % CHUNK) == row_p, 1.0, 0.0).astype(F32)
    strict = (lane_p % CHUNK) < row_p
    incl = (lane_p % CHUNK) <= row_p
    bd = (_iota((PACK_W, PACK_W), 0) // CHUNK) == (_iota((PACK_W, PACK_W), 1) // CHUNK)
    bd2 = jnp.concatenate([bd, bd], axis=1)
    ltri = jnp.where(_iota((CHUNK, CHUNK), 1) <= _iota((CHUNK, CHUNK), 0), 1.0, 0.0).astype(F32)

    def bdiag(x):
        return jnp.where(bd, _stack_rows(x, HEADS_PER_GROUP), 0.0)

    def chunk_body(c, carry):
        r0 = pl.multiple_of(c * CHUNK, CHUNK)
        rows = pl.ds(r0, CHUNK)
        for g in range(width // PACK_W):
            cols = slice(g * PACK_W, (g + 1) * PACK_W)
            ld = ld_ref[rows, cols]
            cs = _mm_hi(ltri, ld)
            cl = cs[CHUNK - 1:CHUNK, :]
            e_in = jnp.exp(cs)
            e_ex = jnp.exp(cs - ld)
            e_neg = jnp.exp(-cs)
            e_dec = jnp.exp(cl - cs)
            rc, kc, vc = rs_ref[rows, cols], ks_ref[rows, cols], vs_ref[rows, cols]
            kkc, bc = kks_ref[rows, cols], bs_ref[rows, cols]
            rt = rc * e_in
            kt = kkc * e_ex
            lhs = jnp.concatenate([kt, rt], axis=0)
            ak = _mm_nt(lhs, bdiag(kc * e_neg))
            ab = lax.dot_general(lhs, bdiag(bc * e_neg), (((1,), (1,)), ((), ())),
                                 preferred_element_type=F32, precision=HIGHEST)
            a_kk = jnp.where(strict, ak[:CHUNK], 0.0)
            a_rk = jnp.where(incl, ak[CHUNK:], 0.0)
            a_kb = jnp.where(strict, ab[:CHUNK], 0.0)
            a_rb = jnp.where(incl, ab[CHUNK:], 0.0)
            t_inv = _tri_inv_packed(a_kb, eye, bd)
            xo = _mm(jnp.concatenate([a_kk, a_rk], axis=0), bdiag(vc))
            x0, o0 = xo[:CHUNK], xo[CHUNK:]
            tt = _mm(t_inv, jnp.where(bd2, _stack_rows(jnp.concatenate([x0, kt], axis=1),
                                                       HEADS_PER_GROUP), 0.0))
            u0, tk = tt[:, :PACK_W], tt[:, PACK_W:]
            rr = _mm(a_rb, jnp.where(bd2, _stack_rows(jnp.concatenate([tk, u0], axis=1),
                                                      HEADS_PER_GROUP), 0.0))
            rq = rt - rr[:, :PACK_W]
            o1 = o0 - rr[:, PACK_W:]
            state = st_ref[g]
            uo = _mm_nt(jnp.concatenate([tk, rq], axis=0), state)
            u = u0 + uo[:CHUNK]
            os_ref[rows, cols] = o1 + uo[CHUNK:]
            upd = _mm_tn(jnp.concatenate([vc, u], axis=0),
                         jnp.concatenate([kc * e_dec, -(bc * e_dec)], axis=0))
            st_ref[g] = state * jnp.exp(cl) + jnp.where(bd, upd, 0.0)
        return carry

    lax.fori_loop(0, ts // CHUNK, chunk_body, 0)

    o = os_ref[...]
    inv_n = 1.0 / RWKV_N
    mean = seg_sum(o) * inv_n
    cen = o - mean
    var = seg_sum(cen * cen) * inv_n
    o = cen * lax.rsqrt(var + RWKV_GN_EPS) * lnw_ref[...] + lnb_ref[...]
    bonus = seg_sum(rs_ref[...] * ks_ref[...] * rk_ref[...]) * vs_ref[...]
    o_ref[...] = ((o + bonus) * gg_ref[...]).astype(BF16)


def _rwkv(p, mu, w0, w2, a0, a2, g2, k_k, k_a, r_k, ln_w, ln_b, bsz, seq):
    ts = min(256, seq)
    ns = seq // ts
    width = D_MODEL
    row = lambda v: v.reshape(1, -1)
    w2p = jnp.concatenate([w2, jnp.zeros_like(w2)], axis=0).astype(BF16)
    a2p = jnp.concatenate([jnp.zeros_like(a2), a2], axis=0).astype(BF16)
    seg = (np.arange(PACK_W)[:, None] // RWKV_N == np.arange(PACK_W)[None, :] // RWKV_N)
    seg = jnp.asarray(seg, BF16)
    tok = lambda col: pl.BlockSpec((ts, width), lambda b, s, col=col: (b * ns + s, col))
    const = lambda shape: pl.BlockSpec(shape, lambda b, s: (0,) * len(shape))
    fbuf = lambda: pltpu.VMEM((ts, width), F32)
    return pl.pallas_call(
        functools.partial(_rwkv_kernel, ts=ts),
        grid=(bsz, ns),
        in_specs=[tok(4), tok(5), tok(6),
                  pl.BlockSpec((ts, LORA_COLS), lambda b, s: (b * ns + s, 9216 // LORA_COLS)),
                  const((1, 3 * width + LORA_COLS)), const((1, width)), const((128, width)),
                  const((1, width)), const((128, width)), const((128, width)),
                  const((1, width)), const((1, width)), const((1, width)), const((1, width)),
                  const((1, width)), const((PACK_W, PACK_W))],
        out_specs=pl.BlockSpec((ts, width), lambda b, s: (b * ns + s, 0)),
        out_shape=jax.ShapeDtypeStruct((bsz * seq, width), BF16),
        scratch_shapes=[pltpu.VMEM((4, 8, width), F32),
                        fbuf(), fbuf(), fbuf(), fbuf(), fbuf(), fbuf(), fbuf(), fbuf(),
                        pltpu.VMEM((width // PACK_W, PACK_W, PACK_W), F32)],
        compiler_params=pltpu.CompilerParams(dimension_semantics=("parallel", "arbitrary"),
                                             vmem_limit_bytes=VMEM_LIMIT),
        name="rwkv",
    )(p, p, p, p, row(mu), row(w0), w2p, row(a0), a2p, g2.astype(BF16), row(k_k), row(k_a),
      row(r_k), row(ln_w), row(ln_b), seg)


def _merge_kernel(x_ref, ya_ref, yb_ref, ga_ref, gb_ref, pa_ref, pb_ref, wo_ref, nw_ref, rw_ref, rb_ref,
                  x1_ref, hn_ref, lg_ref):
    merged = (jax.nn.sigmoid(ga_ref[...]) * jnp.dot(ya_ref[...], pa_ref[...], preferred_element_type=F32)
              + jax.nn.sigmoid(gb_ref[...]) * jnp.dot(yb_ref[...], pb_ref[...], preferred_element_type=F32))
    x1 = x_ref[...] + _mm(merged, wo_ref[...])
    x1_ref[...] = x1
    hn = x1 * lax.rsqrt(jnp.mean(x1 * x1, axis=-1, keepdims=True) + EPS) * nw_ref[...]
    hn_ref[...] = hn.astype(BF16)
    lg_ref[...] = lax.dot_general(rw_ref[...], hn, (((1,), (1,)), ((), ())),
                                  preferred_element_type=F32, precision=HIGHEST) + rb_ref[...]


def _merge(x2, ya, yb, p, proj_a, proj_b, w_out, norm_w, router_w, router_b):
    t = x2.shape[0]
    tm = min(512, t)
    tok = lambda col: pl.BlockSpec((tm, D_MODEL), lambda i, col=col: (i, col))
    const = lambda shape: pl.BlockSpec(shape, lambda i: (0,) * len(shape))
    return pl.pallas_call(
        _merge_kernel,
        grid=(t // tm,),
        in_specs=[tok(0), tok(0), tok(0), tok(7), tok(8),
                  const((D_MODEL, D_MODEL)), const((D_MODEL, D_MODEL)), const((D_MODEL, D_MODEL)),
                  const((1, D_MODEL)), const((N_EXPERTS, D_MODEL)), const((N_EXPERTS, 1))],
        out_specs=[tok(0), tok(0), pl.BlockSpec((N_EXPERTS, tm), lambda i: (0, i))],
        out_shape=[jax.ShapeDtypeStruct((t, D_MODEL), F32), jax.ShapeDtypeStruct((t, D_MODEL), BF16),
                   jax.ShapeDtypeStruct((N_EXPERTS, t), F32)],
        compiler_params=pltpu.CompilerParams(dimension_semantics=("parallel",),
                                             vmem_limit_bytes=VMEM_LIMIT),
        name="merge",
    )(x2, ya, yb, p, p, proj_a.astype(BF16), proj_b.astype(BF16), w_out.astype(BF16),
      norm_w.reshape(1, D_MODEL), router_w.T, router_b.reshape(N_EXPERTS, 1))


def _route_kernel(lg_ref, eidx_ref, gate_ref, rank_ref, base_ref, cnt_ref, carry_ref, *, tt):
    @pl.when(pl.program_id(0) == 0)
    def _():
        carry_ref[...] = jnp.zeros_like(carry_ref)

    l = lg_ref[...]
    ie = _iota((N_EXPERTS, tt), 0)
    vals, hots, idxs = [], [], []
    for _ in range(TOP_K):
        m = jnp.max(l, axis=0, keepdims=True)
        idx = jnp.min(jnp.where(l == m, ie, N_EXPERTS), axis=0, keepdims=True)
        hot = ie == idx
        vals.append(m)
        hots.append(hot)
        idxs.append(idx)
        l = jnp.where(hot, -jnp.inf, l)
    exps = [jnp.exp(v - vals[0]) for v in vals]
    den = exps[0] + exps[1] + exps[2] + exps[3]
    gate_ref[...] = jnp.concatenate([e / den for e in exps], axis=0)
    eidx_ref[...] = jnp.concatenate(idxs, axis=0)

    sel = jnp.zeros((N_EXPERTS, tt), F32)
    for hot in hots:
        sel = sel + jnp.where(hot, 1.0, 0.0)
    before = jnp.where(_iota((tt, tt), 0) < _iota((tt, tt), 1), 1.0, 0.0).astype(BF16)
    carry = carry_ref[...]
    prefix = jnp.dot(sel.astype(BF16), before, preferred_element_type=F32) + carry[:, 0:1]
    rank_ref[...] = jnp.concatenate(
        [jnp.sum(jnp.where(hot, prefix, 0.0), axis=0, keepdims=True) for hot in hots], axis=0).astype(I32)
    cnt = jnp.broadcast_to(jnp.sum(sel, axis=1, keepdims=True), (N_EXPERTS, 128))
    base_ref[0] = carry
    cnt_ref[0] = cnt
    carry_ref[...] = carry + cnt


def _route(logits_t, tt):
    t = logits_t.shape[1]
    nt = t // tt
    row4 = pl.BlockSpec((TOP_K, tt), lambda i: (0, i))
    per_tile = pl.BlockSpec((1, N_EXPERTS, 128), lambda i: (i, 0, 0))
    return pl.pallas_call(
        functools.partial(_route_kernel, tt=tt),
        grid=(nt,),
        in_specs=[pl.BlockSpec((N_EXPERTS, tt), lambda i: (0, i))],
        out_specs=[row4, row4, row4, per_tile, per_tile],
        out_shape=[jax.ShapeDtypeStruct((TOP_K, t), I32), jax.ShapeDtypeStruct((TOP_K, t), F32),
                   jax.ShapeDtypeStruct((TOP_K, t), I32),
                   jax.ShapeDtypeStruct((nt, N_EXPERTS, 128), F32),
                   jax.ShapeDtypeStruct((nt, N_EXPERTS, 128), F32)],
        scratch_shapes=[pltpu.VMEM((N_EXPERTS, 128), F32)],
        compiler_params=pltpu.CompilerParams(dimension_semantics=("arbitrary",)),
        name="route",
    )(logits_t)


def _count_le(sorted_vals, queries):
    return jnp.sum((sorted_vals[None, :] <= queries[:, None]).astype(I32), axis=1)


def _routing_plan(base, cnt, eidx, rank, tt):
    nt = base.shape[0]
    t = nt * tt
    n_mb = (t * TOP_K) // EXPERT_BLOCK + N_EXPERTS
    n_sb = n_mb * (EXPERT_BLOCK // SLOT_BLOCK)
    counts = jnp.sum(cnt, axis=0)
    padded = ((counts + EXPERT_BLOCK - 1) // EXPERT_BLOCK) * EXPERT_BLOCK
    end_pad = jnp.cumsum(padded)
    start_pad = end_pad - padded
    dest = start_pad[eidx] + rank

    mb_start = jnp.arange(n_mb, dtype=I32) * EXPERT_BLOCK
    mb_expert = jnp.minimum(_count_le(end_pad, mb_start), N_EXPERTS - 1).astype(I32)
    mb_active = (mb_start < end_pad[-1]).astype(I32)

    sb_start = jnp.arange(n_sb, dtype=I32) * SLOT_BLOCK
    sb_e = jnp.minimum(_count_le(end_pad, sb_start), N_EXPERTS - 1)
    r_lo = sb_start - start_pad[sb_e]
    sb_valid = (r_lo >= 0) & (r_lo < counts[sb_e])
    r_hi = jnp.minimum(r_lo + SLOT_BLOCK - 1, counts[sb_e] - 1)
    base_rows = base.T[sb_e]
    t_lo = jnp.sum(base_rows <= r_lo[:, None], axis=1) - 1
    t_hi = jnp.sum(base_rows <= r_hi[:, None], axis=1) - 1
    npairs = jnp.where(sb_valid, t_hi - t_lo + 1, 1)
    d_end = jnp.cumsum(npairs)
    d_off = d_end - npairs
    max_d = n_sb + N_EXPERTS * nt
    pid = jnp.arange(max_d, dtype=I32)
    pj = jnp.minimum(_count_le(d_end, pid), n_sb - 1)
    in_range = pid < d_end[-1]
    pi = jnp.where(in_range & sb_valid[pj], t_lo[pj] + pid - d_off[pj], 0)
    pi = jnp.where(in_range, pi, pi[jnp.maximum(d_end[-1] - 1, 0)])
    d_first = (in_range & (pid == d_off[pj])).astype(I32)
    d_valid = (in_range & sb_valid[pj]).astype(I32)
    dispatch = (pj.astype(I32), jnp.clip(pi, 0, nt - 1).astype(I32), d_first, d_valid)

    lo = (start_pad[None, :] + base).reshape(-1)
    n = cnt.reshape(-1)
    nb = jnp.where(n > 0, (lo + n - 1) // SLOT_BLOCK - lo // SLOT_BLOCK + 1, 0)
    c_end = jnp.cumsum(nb)
    c_off = c_end - nb
    max_c = n_sb + N_EXPERTS * nt
    pid = jnp.arange(max_c, dtype=I32)
    q = jnp.minimum(_count_le(c_end, pid), nt * N_EXPERTS - 1)
    in_range = pid < c_end[-1]
    cj = lo[q] // SLOT_BLOCK + pid - c_off[q]
    cj = jnp.where(in_range, cj, cj[jnp.maximum(c_end[-1] - 1, 0)])
    ci = jnp.where(in_range, q // N_EXPERTS, nt - 1)
    tile_first = c_off.reshape(nt, N_EXPERTS)[:, 0]
    tile_last = c_end.reshape(nt, N_EXPERTS)[:, -1] - 1
    c_first = (in_range & (pid == tile_first[ci])).astype(I32)
    c_last = (in_range & (pid == tile_last[ci])).astype(I32)
    combine = (ci.astype(I32), jnp.clip(cj, 0, n_sb - 1).astype(I32), c_first, c_last, in_range.astype(I32))
    return dest.astype(I32), mb_expert, mb_active, dispatch, combine, n_mb, n_sb


def _dispatch_kernel(pj_ref, pi_ref, first_ref, valid_ref, hn_ref, dest_ref, o_ref, *, tt):
    p = pl.program_id(0)

    @pl.when(first_ref[p] == 1)
    def _():
        o_ref[...] = jnp.zeros_like(o_ref)

    @pl.when(valid_ref[p] == 1)
    def _():
        slot = pj_ref[p] * SLOT_BLOCK + _iota((SLOT_BLOCK, tt), 0)
        d = dest_ref[...]
        hot = jnp.zeros((SLOT_BLOCK, tt), F32)
        for k in range(TOP_K):
            hot = hot + jnp.where(d[k:k + 1] == slot, 1.0, 0.0)
        o_ref[...] += jnp.dot(hot.astype(BF16), hn_ref[...], preferred_element_type=F32).astype(BF16)


def _dispatch(hn, dest, plan, n_sb, tt):
    pj, pi, first, valid = plan
    return pl.pallas_call(
        functools.partial(_dispatch_kernel, tt=tt),
        grid_spec=pltpu.PrefetchScalarGridSpec(
            num_scalar_prefetch=4,
            grid=(pj.shape[0],),
            in_specs=[pl.BlockSpec((tt, D_MODEL), lambda p, pj, pi, f, v: (pi[p], 0)),
                      pl.BlockSpec((TOP_K, tt), lambda p, pj, pi, f, v: (0, pi[p]))],
            out_specs=pl.BlockSpec((SLOT_BLOCK, D_MODEL), lambda p, pj, pi, f, v: (pj[p], 0))),
        out_shape=jax.ShapeDtypeStruct((n_sb * SLOT_BLOCK, D_MODEL), BF16),
        compiler_params=pltpu.CompilerParams(dimension_semantics=("arbitrary",)),
        name="dispatch",
    )(pj, pi, first, valid, hn, dest)


def _expert_kernel(e_ref, act_ref, x_ref, wg_ref, wl_ref, wd_ref, bg_ref, bl_ref, bd_ref, o_ref):
    mb = pl.program_id(0)

    @pl.when(act_ref[mb] == 0)
    def _():
        o_ref[...] = jnp.zeros_like(o_ref)

    @pl.when(act_ref[mb] == 1)
    def _():
        x = x_ref[...]
        glu = jnp.dot(x, wg_ref[0], preferred_element_type=F32) + bg_ref[0]
        lin = jnp.dot(x, wl_ref[0], preferred_element_type=F32) + bl_ref[0]
        glu = jnp.minimum(glu, SWIGLU_LIMIT)
        lin = jnp.clip(lin, -SWIGLU_LIMIT, SWIGLU_LIMIT)
        act = glu * jax.nn.sigmoid(SWIGLU_ALPHA * glu) * (lin + 1.0)
        o_ref[...] = (_mm(act, wd_ref[0]) + bd_ref[0]).astype(BF16)


def _experts(xb, mb_expert, mb_active, wg, wl, wd, bg, bl, bd, n_mb):
    wspec = pl.BlockSpec((1, D_MODEL, D_MODEL), lambda m, e, a: (e[m], 0, 0))
    bspec = pl.BlockSpec((1, 1, D_MODEL), lambda m, e, a: (e[m], 0, 0))
    xspec = pl.BlockSpec((EXPERT_BLOCK, D_MODEL), lambda m, e, a: (m, 0))
    return pl.pallas_call(
        _expert_kernel,
        grid_spec=pltpu.PrefetchScalarGridSpec(
            num_scalar_prefetch=2,
            grid=(n_mb,),
            in_specs=[xspec, wspec, wspec, wspec, bspec, bspec, bspec],
            out_specs=xspec),
        out_shape=jax.ShapeDtypeStruct(xb.shape, BF16),
        compiler_params=pltpu.CompilerParams(dimension_semantics=("arbitrary",),
                                             vmem_limit_bytes=VMEM_LIMIT),
        name="experts",
    )(mb_expert, mb_active, xb, wg, wl, wd, bg, bl, bd)


def _combine_kernel(ci_ref, cj_ref, first_ref, last_ref, valid_ref, yb_ref, dest_ref, gate_ref, x1_ref,
                    nw_ref, o_ref, acc_ref, *, tt):
    p = pl.program_id(0)

    @pl.when(first_ref[p] == 1)
    def _():
        acc_ref[...] = jnp.zeros_like(acc_ref)

    @pl.when(valid_ref[p] == 1)
    def _():
        slot = cj_ref[p] * SLOT_BLOCK + _iota((tt, SLOT_BLOCK), 1)
        d = dest_ref[...]
        g = gate_ref[...]
        w = jnp.zeros((tt, SLOT_BLOCK), F32)
        for k in range(TOP_K):
            w = w + jnp.where(d[:, k:k + 1] == slot, g[:, k:k + 1], 0.0)
        acc_ref[...] += jnp.dot(w.astype(BF16), yb_ref[...], preferred_element_type=F32)

    @pl.when(last_ref[p] == 1)
    def _():
        y = x1_ref[...] + acc_ref[...]
        o_ref[...] = y * lax.rsqrt(jnp.mean(y * y, axis=-1, keepdims=True) + EPS) * nw_ref[...]


def _combine(yb, dest_t, gate_t, x1, norm_w, plan, tt):
    ci, cj, first, last, valid = plan
    t = x1.shape[0]
    tile = lambda shape: pl.BlockSpec(shape, lambda p, ci, cj, f, l, v: (ci[p], 0))
    return pl.pallas_call(
        functools.partial(_combine_kernel, tt=tt),
        grid_spec=pltpu.PrefetchScalarGridSpec(
            num_scalar_prefetch=5,
            grid=(ci.shape[0],),
            in_specs=[pl.BlockSpec((SLOT_BLOCK, D_MODEL), lambda p, ci, cj, f, l, v: (cj[p], 0)),
                      tile((tt, TOP_K)), tile((tt, TOP_K)), tile((tt, D_MODEL)),
                      pl.BlockSpec((1, D_MODEL), lambda p, ci, cj, f, l, v: (0, 0))],
            out_specs=tile((tt, D_MODEL)),
            scratch_shapes=[pltpu.VMEM((tt, D_MODEL), F32)]),
        out_shape=jax.ShapeDtypeStruct((t, D_MODEL), F32),
        compiler_params=pltpu.CompilerParams(dimension_semantics=("arbitrary",)),
        name="combine",
    )(ci, cj, first, last, valid, yb, dest_t, gate_t, x1, norm_w.reshape(1, D_MODEL))


def _moe(x1, hn, logits_t, w_gu, b_gu, w_down, b_down, norm_final):
    t = x1.shape[0]
    tt = min(512, t)
    eidx, gate, rank, base, cnt = _route(logits_t, tt)
    base = base[:, :, 0].astype(I32)
    cnt = cnt[:, :, 0].astype(I32)
    dest, mb_expert, mb_active, d_plan, c_plan, n_mb, n_sb = _routing_plan(base, cnt, eidx, rank, tt)
    xb = _dispatch(hn, dest, d_plan, n_sb, tt)
    wg = w_gu[:, :, 0::2].astype(BF16)
    wl = w_gu[:, :, 1::2].astype(BF16)
    bg = b_gu[:, None, 0::2]
    bl = b_gu[:, None, 1::2]
    yb = _experts(xb, mb_expert, mb_active, wg, wl, w_down.astype(BF16), bg, bl, b_down[:, None, :], n_mb)
    return _combine(yb, dest.T, gate.T, x1, norm_final, c_plan, tt)


def kernel(x, norm_mix, w_in, gdn_conv, gdn_A_log, gdn_dt_bias, gdn_norm, rwkv_mu, rwkv_w0, rwkv_w2, rwkv_a0, rwkv_a2, rwkv_g2, rwkv_k_k, rwkv_k_a, rwkv_r_k, rwkv_ln_w, rwkv_ln_b, proj_a, proj_b, w_out, norm_ffn, router_w, router_b, w_gate_up, b_gate_up, w_down, b_down, norm_final):
    bsz, seq, d = x.shape
    depth = w_in.shape[0]
    x2 = x.reshape(bsz * seq, d)
    out = None
    for l in range(depth):
        w = w_in[l]
        w_pack = jnp.concatenate([w[:, 0:4096], w[:, 4112:7184], w[:, 7440:9488], w[:, 7184:7440],
                                  w[:, 4096:4112], jnp.zeros((d, PACK_COLS - 9488), w.dtype)],
                                 axis=1).astype(BF16)
        p = _in_proj(x2, norm_mix[l], w_pack)
        ya = _gdn(p, gdn_conv[l], gdn_A_log[l], gdn_dt_bias[l], gdn_norm[l], bsz, seq)
        yb = _rwkv(p, rwkv_mu[l], rwkv_w0[l], rwkv_w2[l], rwkv_a0[l], rwkv_a2[l], rwkv_g2[l],
                   rwkv_k_k[l], rwkv_k_a[l], rwkv_r_k[l], rwkv_ln_w[l], rwkv_ln_b[l], bsz, seq)
        x1, hn, logits_t = _merge(x2, ya, yb, p, proj_a[l], proj_b[l], w_out[l], norm_ffn[l],
                                  router_w[l], router_b[l])
        assert l == depth - 1, "only the final layer's residual is fused with the output norm"
        out = _moe(x1, hn, logits_t, w_gate_up[l], b_gate_up[l], w_down[l], b_down[l], norm_final)
    return out.reshape(bsz, seq, d)
```

```python
import functools

import jax
import jax.numpy as jnp
import numpy as np
from jax import lax
from jax.experimental import pallas as pl
from jax.experimental.pallas import tpu as pltpu

F32 = jnp.float32
BF16 = jnp.bfloat16
I32 = jnp.int32
HIGHEST = lax.Precision.HIGHEST

D_MODEL = 1024
EPS = 1e-6
CHUNK = 64
GDN_HEADS = 8
GDN_DK = 128
CONV_W = 4
RWKV_HEADS = 16
RWKV_N = 64
RWKV_GN_EPS = 64e-5
LORA_COLS = 256
N_EXPERTS = 32
TOP_K = 4
SWIGLU_ALPHA = 1.702
SWIGLU_LIMIT = 7.0

PACK_COLS = 9600
COL_BA = 9472
HEADS_PER_GROUP = 4
PACK_W = HEADS_PER_GROUP * CHUNK

SLOT_BLOCK = 128
EXPERT_BLOCK = 512
VMEM_LIMIT = 48 * 1024 * 1024


def _mm(a, b):
    return jnp.dot(a.astype(BF16), b.astype(BF16), preferred_element_type=F32)


def _mm_nt(a, b):
    return lax.dot_general(a.astype(BF16), b.astype(BF16), (((1,), (1,)), ((), ())),
                           preferred_element_type=F32)


def _mm_tn(a, b):
    return lax.dot_general(a.astype(BF16), b.astype(BF16), (((0,), (0,)), ((), ())),
                           preferred_element_type=F32)


def _split_bf16(x, terms):
    parts = []
    for _ in range(terms - 1):
        hi = x.astype(BF16)
        parts.append(hi)
        x = x - hi.astype(F32)
    parts.append(x.astype(BF16))
    return parts


def _mm_exact_lhs(a_bf16, b):
    out = None
    for part in _split_bf16(b, 3):
        d = jnp.dot(a_bf16, part, preferred_element_type=F32)
        out = d if out is None else out + d
    return out


def _iota(shape, dim):
    return lax.broadcasted_iota(I32, shape, dim)


def _softplus(x):
    return jnp.maximum(x, 0.0) + jnp.log(1.0 + jnp.exp(-jnp.abs(x)))


def _stack_rows(x, n):
    return jnp.concatenate([x] * n, axis=0)


def _tri_inv_packed(m, eye, bdmask):
    def bd(x):
        return jnp.where(bdmask, _stack_rows(x, HEADS_PER_GROUP), jnp.zeros((), BF16))

    def mul(a, b):
        ah, al = _split_bf16(a, 2)
        bh, bl = _split_bf16(b, 2)
        bdh = bd(bh)
        return (jnp.dot(ah, bdh, preferred_element_type=F32) + jnp.dot(al, bdh, preferred_element_type=F32)
                + jnp.dot(ah, bd(bl), preferred_element_type=F32))

    t = eye - m
    x = mul(m, m)
    for _ in range(4):
        r = mul(jnp.concatenate([t, x], axis=0), x)
        t = t + r[:CHUNK]
        x = r[CHUNK:]
    return t + mul(t, x)


def _shift_rows(x, prev8, k, row8):
    r = pltpu.roll(x, k, 0)
    pr = pltpu.roll(prev8, k, 0)
    head = jnp.where(row8 < k, pr, r[:8])
    return jnp.concatenate([head, r[8:]], axis=0)


def _in_proj_kernel(x_ref, nw_ref, w_ref, o_ref, h_ref):
    @pl.when(pl.program_id(1) == 0)
    def _():
        x = x_ref[...]
        y = x * lax.rsqrt(jnp.mean(x * x, axis=-1, keepdims=True) + EPS)
        h_ref[...] = (y * nw_ref[...]).astype(BF16)

    o_ref[...] = jnp.dot(h_ref[...], w_ref[...], preferred_element_type=F32)


def _in_proj(x2, norm_w, w_pack):
    t = x2.shape[0]
    tm = min(512, t)
    tn = 1920
    return pl.pallas_call(
        _in_proj_kernel,
        grid=(t // tm, PACK_COLS // tn),
        in_specs=[pl.BlockSpec((tm, D_MODEL), lambda i, j: (i, 0)),
                  pl.BlockSpec((1, D_MODEL), lambda i, j: (0, 0)),
                  pl.BlockSpec((D_MODEL, tn), lambda i, j: (0, j))],
        out_specs=pl.BlockSpec((tm, tn), lambda i, j: (i, j)),
        out_shape=jax.ShapeDtypeStruct((t, PACK_COLS), F32),
        scratch_shapes=[pltpu.VMEM((tm, D_MODEL), BF16)],
        compiler_params=pltpu.CompilerParams(dimension_semantics=("parallel", "arbitrary"),
                                             vmem_limit_bytes=VMEM_LIMIT),
        name="in_proj",
    )(x2, norm_w.reshape(1, D_MODEL), w_pack)


def _gdn_kernel(q_ref, k_ref, v_ref, z_ref, ba_ref, conv_ref, alog_ref, dtb_ref, nw_ref, o_ref,
                carry_ref, qn_ref, kn_ref, vv_ref, beta_ref, g_ref, s_ref, *, ts):
    @pl.when(pl.program_id(1) == 0)
    def _():
        carry_ref[...] = jnp.zeros_like(carry_ref)
        s_ref[...] = jnp.zeros_like(s_ref)

    row8 = _iota((8, D_MODEL), 0)

    def conv_silu(idx, x_ref):
        x = x_ref[...]
        prev8 = carry_ref[idx]
        w4 = conv_ref[:, idx * D_MODEL:(idx + 1) * D_MODEL]
        y = x * w4[CONV_W - 1:CONV_W]
        for k in range(1, CONV_W):
            y = y + _shift_rows(x, prev8, k, row8) * w4[CONV_W - 1 - k:CONV_W - k]
        carry_ref[idx] = x[ts - 8:ts]
        return y * jax.nn.sigmoid(y)

    def l2norm_heads(x, scale):
        parts = []
        for h in range(GDN_HEADS):
            xh = x[:, h * GDN_DK:(h + 1) * GDN_DK]
            parts.append(xh * (lax.rsqrt(jnp.sum(xh * xh, axis=-1, keepdims=True) + EPS) * scale))
        return jnp.concatenate(parts, axis=1)

    qn_ref[...] = l2norm_heads(conv_silu(0, q_ref), GDN_DK ** -0.5)
    kn_ref[...] = l2norm_heads(conv_silu(1, k_ref), 1.0)
    vv_ref[...] = conv_silu(2, v_ref)
    ba = ba_ref[...]
    beta_ref[...] = jax.nn.sigmoid(ba)
    g_ref[...] = -jnp.exp(alog_ref[...]) * _softplus(ba + dtb_ref[...])

    gw = HEADS_PER_GROUP * GDN_DK
    lane_p = _iota((CHUNK, PACK_W), 1)
    row_p = _iota((CHUNK, PACK_W), 0)
    eye = jnp.where((lane_p % CHUNK) == row_p, 1.0, 0.0).astype(F32)
    strict = (lane_p % CHUNK) < row_p
    incl = (lane_p % CHUNK) <= row_p
    bd_pp = (_iota((PACK_W, PACK_W), 0) // CHUNK) == (_iota((PACK_W, PACK_W), 1) // CHUNK)
    bd_pn = (_iota((PACK_W, gw), 0) // CHUNK) == (_iota((PACK_W, gw), 1) // GDN_DK)
    bd_pn2 = (_iota((PACK_W, 2 * gw), 0) // CHUNK) == ((_iota((PACK_W, 2 * gw), 1) % gw) // GDN_DK)
    ltri = jnp.where(_iota((CHUNK, CHUNK), 1) <= _iota((CHUNK, CHUNK), 0), 1.0, 0.0).astype(BF16)
    ones_cc = jnp.ones((CHUNK, CHUNK), BF16)
    lane128 = _iota((CHUNK, 128), 1)
    nw = nw_ref[...]

    def chunk_body(c, carry):
        r0 = pl.multiple_of(c * CHUNK, CHUNK)
        rows = pl.ds(r0, CHUNK)
        beta_c = beta_ref[rows, :]
        gcs = _mm_exact_lhs(ltri, g_ref[rows, :])
        for gi in range(GDN_HEADS // HEADS_PER_GROUP):
            heads = [gi * HEADS_PER_GROUP + j for j in range(HEADS_PER_GROUP)]
            cols = slice(gi * gw, (gi + 1) * gw)
            qn = qn_ref[rows, cols]
            kn = kn_ref[rows, cols]
            vv = vv_ref[rows, cols]
            beta_nat = jnp.concatenate(
                [jnp.broadcast_to(beta_c[:, h:h + 1], (CHUNK, GDN_DK)) for h in heads], axis=1)
            gcol = [jnp.broadcast_to(gcs[:, 8 + h:9 + h], (CHUNK, GDN_DK)) for h in heads]
            gc_nat = jnp.concatenate(gcol, axis=1)
            gcol_p = jnp.concatenate([jnp.where(lane128 < CHUNK, gcol[0], gcol[1]),
                                      jnp.where(lane128 < CHUNK, gcol[2], gcol[3])], axis=1)
            grow_p = _mm_exact_lhs(ones_cc, gcol_p * eye)
            decay_p = jnp.exp(jnp.minimum(gcol_p - grow_p, 0.0))
            kb = kn * beta_nat
            bdk = jnp.where(bd_pn, _stack_rows(kn, HEADS_PER_GROUP), 0.0)
            aa = _mm_nt(jnp.concatenate([kb, qn], axis=0), bdk)
            m = jnp.where(strict, aa[:CHUNK] * decay_p, 0.0)
            attn = jnp.where(incl, aa[CHUNK:] * decay_p, 0.0)
            t_inv = _tri_inv_packed(m, eye, bd_pp)
            egc = jnp.exp(gc_nat)
            data = jnp.concatenate([vv * beta_nat, kb * egc], axis=1)
            uw = _mm(t_inv, jnp.where(bd_pn2, _stack_rows(data, HEADS_PER_GROUP), 0.0))
            qd = qn * egc
            gl = gc_nat[CHUNK - 1:CHUNK, :]
            kd = kn * jnp.exp(gl - gc_nat)
            egl = jnp.exp(gl)
            vnew, o1 = [], []
            for j, h in enumerate(heads):
                hc = slice(j * GDN_DK, (j + 1) * GDN_DK)
                state = s_ref[h]
                wq = _mm(jnp.concatenate([uw[:, gw + j * GDN_DK:gw + (j + 1) * GDN_DK], qd[:, hc]],
                                         axis=0), state)
                vn = uw[:, hc] - wq[:CHUNK]
                vnew.append(vn)
                o1.append(wq[CHUNK:])
                s_ref[h] = state * egl[:, hc] + _mm_tn(kd[:, hc], vn)
            vnew = jnp.concatenate(vnew, axis=1)
            o = jnp.concatenate(o1, axis=1) + _mm(
                attn, jnp.where(bd_pn, _stack_rows(vnew, HEADS_PER_GROUP), 0.0))
            for j, h in enumerate(heads):
                hc = slice(j * GDN_DK, (j + 1) * GDN_DK)
                oh = o[:, hc]
                oh = oh * lax.rsqrt(jnp.mean(oh * oh, axis=-1, keepdims=True) + EPS) * nw
                zh = z_ref[rows, h * GDN_DK:(h + 1) * GDN_DK]
                o_ref[rows, h * GDN_DK:(h + 1) * GDN_DK] = (oh * (zh * jax.nn.sigmoid(zh))).astype(BF16)
        return carry

    lax.fori_loop(0, ts // CHUNK, chunk_body, 0)


def _gdn(p, conv_w, a_log, dt_bias, norm_w, bsz, seq):
    ts = min(256, seq)
    ns = seq // ts
    pad8 = lambda v: jnp.zeros((1, 128), F32).at[0, 8:16].set(v)
    tok = lambda col: pl.BlockSpec((ts, D_MODEL), lambda b, s, col=col: (b * ns + s, col))
    const = lambda shape: pl.BlockSpec(shape, lambda b, s: (0,) * len(shape))
    return pl.pallas_call(
        functools.partial(_gdn_kernel, ts=ts),
        grid=(bsz, ns),
        in_specs=[tok(0), tok(1), tok(2), tok(3),
                  pl.BlockSpec((ts, 128), lambda b, s: (b * ns + s, COL_BA // 128)),
                  const((CONV_W, 3 * D_MODEL)), const((1, 128)), const((1, 128)), const((1, GDN_DK))],
        out_specs=pl.BlockSpec((ts, D_MODEL), lambda b, s: (b * ns + s, 0)),
        out_shape=jax.ShapeDtypeStruct((bsz * seq, D_MODEL), BF16),
        scratch_shapes=[pltpu.VMEM((3, 8, D_MODEL), F32),
                        pltpu.VMEM((ts, D_MODEL), F32), pltpu.VMEM((ts, D_MODEL), F32),
                        pltpu.VMEM((ts, D_MODEL), F32),
                        pltpu.VMEM((ts, 128), F32), pltpu.VMEM((ts, 128), F32),
                        pltpu.VMEM((GDN_HEADS, GDN_DK, GDN_DK), F32)],
        compiler_params=pltpu.CompilerParams(dimension_semantics=("parallel", "arbitrary"),
                                             vmem_limit_bytes=VMEM_LIMIT),
        name="gdn",
    )(p, p, p, p, p, conv_w, pad8(a_log), pad8(dt_bias), norm_w.reshape(1, GDN_DK))


def _rwkv_kernel(r_ref, k_ref, v_ref, lo_ref, mu_ref, w0_ref, w2_ref, a0_ref, a2_ref, g2_ref,
                 kk_ref, ka_ref, rk_ref, lnw_ref, lnb_ref, seg_ref, o_ref,
                 carry_ref, rs_ref, ks_ref, vs_ref, kks_ref, bs_ref, ld_ref, gg_ref, os_ref, st_ref,
                 *, ts):
    @pl.when(pl.program_id(1) == 0)
    def _():
        carry_ref[...] = jnp.zeros_like(carry_ref)
        st_ref[...] = jnp.zeros_like(st_ref)

    width = D_MODEL
    row8 = _iota((8, width), 0)
    row8l = _iota((8, LORA_COLS), 0)

    def seg_sum(x):
        hi = x.astype(BF16)
        lo = (x - hi.astype(F32)).astype(BF16)
        parts = []
        for g in range(width // PACK_W):
            cs = slice(g * PACK_W, (g + 1) * PACK_W)
            parts.append(jnp.dot(hi[:, cs], seg_ref[...], preferred_element_type=F32)
                         + jnp.dot(lo[:, cs], seg_ref[...], preferred_element_type=F32))
        return jnp.concatenate(parts, axis=1)

    def lerp(idx, x_ref, mu, r8):
        x = x_ref[...]
        xs = x + (_shift_rows(x, carry_ref[idx, :, :x.shape[1]], 1, r8) - x) * mu
        carry_ref[idx, :, :x.shape[1]] = x[ts - 8:ts]
        return xs

    r = lerp(0, r_ref, mu_ref[:, 0:width], row8)
    k = lerp(1, k_ref, mu_ref[:, width:2 * width], row8)
    v = lerp(2, v_ref, mu_ref[:, 2 * width:3 * width], row8)
    lo = lerp(3, lo_ref, mu_ref[:, 3 * width:3 * width + LORA_COLS], row8l)
    lo_a = lo[:, :128]
    w_log = -_softplus(-(w0_ref[...] + _mm(jnp.tanh(lo_a), w2_ref[...]))) - 0.5
    ld_ref[...] = -jnp.exp(w_log)
    aa = jax.nn.sigmoid(a0_ref[...] + _mm(lo_a, a2_ref[...]))
    gg_ref[...] = _mm(jax.nn.sigmoid(lo[:, 128:]), g2_ref[...])
    kx = k * kk_ref[...]
    kk = kx * lax.rsqrt(seg_sum(kx * kx) + EPS)
    k = k * (1.0 + (aa - 1.0) * ka_ref[...])
    rs_ref[...] = r
    ks_ref[...] = k
    vs_ref[...] = v
    kks_ref[...] = kk
    bs_ref[...] = kk * aa

    lane_p = _iota((CHUNK, PACK_W), 1)
    row_p = _iota((CHUNK, PACK_W), 0)
    eye = jnp.where((lane_p % CHUNK) == row_p, 1.0, 0.0).astype(F32)
    strict = (lane_p % CHUNK) < row_p
    incl = (lane_p % CHUNK) <= row_p
    bd = (_iota((PACK_W, PACK_W), 0) // CHUNK) == (_iota((PACK_W, PACK_W), 1) // CHUNK)
    bd2 = jnp.concatenate([bd, bd], axis=1)
    ltri = jnp.where(_iota((CHUNK, CHUNK), 1) <= _iota((CHUNK, CHUNK), 0), 1.0, 0.0).astype(BF16)

    def bdiag(x):
        return jnp.where(bd, _stack_rows(x, HEADS_PER_GROUP), 0.0)

    def chunk_body(c, carry):
        r0 = pl.multiple_of(c * CHUNK, CHUNK)
        rows = pl.ds(r0, CHUNK)
        for g in range(width // PACK_W):
            cols = slice(g * PACK_W, (g + 1) * PACK_W)
            ld = ld_ref[rows, cols]
            cs = _mm_exact_lhs(ltri, ld)
            cl = cs[CHUNK - 1:CHUNK, :]
            e_in = jnp.exp(cs)
            e_ex = jnp.exp(cs - ld)
            e_neg = jnp.exp(-cs)
            e_dec = jnp.exp(cl - cs)
            rc, kc, vc = rs_ref[rows, cols], ks_ref[rows, cols], vs_ref[rows, cols]
            kkc, bc = kks_ref[rows, cols], bs_ref[rows, cols]
            rt = rc * e_in
            kt = kkc * e_ex
            lhs = jnp.concatenate([kt, rt], axis=0)
            ak = _mm_nt(lhs, bdiag(kc * e_neg))
            ab = _mm_nt(lhs, bdiag(bc * e_neg))
            a_kk = jnp.where(strict, ak[:CHUNK], 0.0)
            a_rk = jnp.where(incl, ak[CHUNK:], 0.0)
            a_kb = jnp.where(strict, ab[:CHUNK], 0.0)
            a_rb = jnp.where(incl, ab[CHUNK:], 0.0)
            t_inv = _tri_inv_packed(a_kb, eye, bd)
            xo = _mm(jnp.concatenate([a_kk, a_rk], axis=0), bdiag(vc))
            x0, o0 = xo[:CHUNK], xo[CHUNK:]
            tt = _mm(t_inv, jnp.where(bd2, _stack_rows(jnp.concatenate([x0, kt], axis=1),
                                                       HEADS_PER_GROUP), 0.0))
            u0, tk = tt[:, :PACK_W], tt[:, PACK_W:]
            rr = _mm(a_rb, jnp.where(bd2, _stack_rows(jnp.concatenate([tk, u0], axis=1),
                                                      HEADS_PER_GROUP), 0.0))
            rq = rt - rr[:, :PACK_W]
            o1 = o0 - rr[:, PACK_W:]
            state = st_ref[g]
            uo = _mm_nt(jnp.concatenate([tk, rq], axis=0), state)
            u = u0 + uo[:CHUNK]
            os_ref[rows, cols] = o1 + uo[CHUNK:]
            upd = _mm_tn(jnp.concatenate([vc, u], axis=0),
                         jnp.concatenate([kc * e_dec, -(bc * e_dec)], axis=0))
            st_ref[g] = state * jnp.exp(cl) + jnp.where(bd, upd, 0.0)
        return carry

    lax.fori_loop(0, ts // CHUNK, chunk_body, 0)

    o = os_ref[...]
    inv_n = 1.0 / RWKV_N
    mean = seg_sum(o) * inv_n
    cen = o - mean
    var = seg_sum(cen * cen) * inv_n
    o = cen * lax.rsqrt(var + RWKV_GN_EPS) * lnw_ref[...] + lnb_ref[...]
    bonus = seg_sum(rs_ref[...] * ks_ref[...] * rk_ref[...]) * vs_ref[...]
    o_ref[...] = ((o + bonus) * gg_ref[...]).astype(BF16)


def _rwkv(p, mu, w0, w2, a0, a2, g2, k_k, k_a, r_k, ln_w, ln_b, bsz, seq):
    ts = min(256, seq)
    ns = seq // ts
    width = D_MODEL
    row = lambda v: v.reshape(1, -1)
    w2p = jnp.concatenate([w2, jnp.zeros_like(w2)], axis=0).astype(BF16)
    a2p = jnp.concatenate([jnp.zeros_like(a2), a2], axis=0).astype(BF16)
    seg = (np.arange(PACK_W)[:, None] // RWKV_N == np.arange(PACK_W)[None, :] // RWKV_N)
    seg = jnp.asarray(seg, BF16)
    tok = lambda col: pl.BlockSpec((ts, width), lambda b, s, col=col: (b * ns + s, col))
    const = lambda shape: pl.BlockSpec(shape, lambda b, s: (0,) * len(shape))
    fbuf = lambda: pltpu.VMEM((ts, width), F32)
    return pl.pallas_call(
        functools.partial(_rwkv_kernel, ts=ts),
        grid=(bsz, ns),
        in_specs=[tok(4), tok(5), tok(6),
                  pl.BlockSpec((ts, LORA_COLS), lambda b, s: (b * ns + s, 9216 // LORA_COLS)),
                  const((1, 3 * width + LORA_COLS)), const((1, width)), const((128, width)),
                  const((1, width)), const((128, width)), const((128, width)),
                  const((1, width)), const((1, width)), const((1, width)), const((1, width)),
                  const((1, width)), const((PACK_W, PACK_W))],
        out_specs=pl.BlockSpec((ts, width), lambda b, s: (b * ns + s, 0)),
        out_shape=jax.ShapeDtypeStruct((bsz * seq, width), BF16),
        scratch_shapes=[pltpu.VMEM((4, 8, width), F32),
                        fbuf(), fbuf(), fbuf(), fbuf(), fbuf(), fbuf(), fbuf(), fbuf(),
                        pltpu.VMEM((width // PACK_W, PACK_W, PACK_W), F32)],
        compiler_params=pltpu.CompilerParams(dimension_semantics=("parallel", "arbitrary"),
                                             vmem_limit_bytes=VMEM_LIMIT),
        name="rwkv",
    )(p, p, p, p, row(mu), row(w0), w2p, row(a0), a2p, g2.astype(BF16), row(k_k), row(k_a),
      row(r_k), row(ln_w), row(ln_b), seg)


def _merge_kernel(x_ref, ya_ref, yb_ref, ga_ref, gb_ref, pa_ref, pb_ref, wo_ref, nw_ref, rw_ref, rb_ref,
                  x1_ref, hn_ref, lg_ref):
    merged = (jax.nn.sigmoid(ga_ref[...]) * jnp.dot(ya_ref[...], pa_ref[...], preferred_element_type=F32)
              + jax.nn.sigmoid(gb_ref[...]) * jnp.dot(yb_ref[...], pb_ref[...], preferred_element_type=F32))
    x1 = x_ref[...] + _mm(merged, wo_ref[...])
    x1_ref[...] = x1
    hn = x1 * lax.rsqrt(jnp.mean(x1 * x1, axis=-1, keepdims=True) + EPS) * nw_ref[...]
    hn_ref[...] = hn.astype(BF16)
    lg_ref[...] = lax.dot_general(rw_ref[...], hn, (((1,), (1,)), ((), ())),
                                  preferred_element_type=F32, precision=HIGHEST) + rb_ref[...]


def _merge(x2, ya, yb, p, proj_a, proj_b, w_out, norm_w, router_w, router_b):
    t = x2.shape[0]
    tm = min(512, t)
    tok = lambda col: pl.BlockSpec((tm, D_MODEL), lambda i, col=col: (i, col))
    const = lambda shape: pl.BlockSpec(shape, lambda i: (0,) * len(shape))
    return pl.pallas_call(
        _merge_kernel,
        grid=(t // tm,),
        in_specs=[tok(0), tok(0), tok(0), tok(7), tok(8),
                  const((D_MODEL, D_MODEL)), const((D_MODEL, D_MODEL)), const((D_MODEL, D_MODEL)),
                  const((1, D_MODEL)), const((N_EXPERTS, D_MODEL)), const((N_EXPERTS, 1))],
        out_specs=[tok(0), tok(0), pl.BlockSpec((N_EXPERTS, tm), lambda i: (0, i))],
        out_shape=[jax.ShapeDtypeStruct((t, D_MODEL), F32), jax.ShapeDtypeStruct((t, D_MODEL), BF16),
                   jax.ShapeDtypeStruct((N_EXPERTS, t), F32)],
        compiler_params=pltpu.CompilerParams(dimension_semantics=("parallel",),
                                             vmem_limit_bytes=VMEM_LIMIT),
        name="merge",
    )(x2, ya, yb, p, p, proj_a.astype(BF16), proj_b.astype(BF16), w_out.astype(BF16),
      norm_w.reshape(1, D_MODEL), router_w.T, router_b.reshape(N_EXPERTS, 1))


def _route_kernel(lg_ref, eidx_ref, gate_ref, rank_ref, base_ref, cnt_ref, carry_ref, *, tt):
    @pl.when(pl.program_id(0) == 0)
    def _():
        carry_ref[...] = jnp.zeros_like(carry_ref)

    l = lg_ref[...]
    ie = _iota((N_EXPERTS, tt), 0)
    vals, hots, idxs = [], [], []
    for _ in range(TOP_K):
        m = jnp.max(l, axis=0, keepdims=True)
        idx = jnp.min(jnp.where(l == m, ie, N_EXPERTS), axis=0, keepdims=True)
        hot = ie == idx
        vals.append(m)
        hots.append(hot)
        idxs.append(idx)
        l = jnp.where(hot, -jnp.inf, l)
    exps = [jnp.exp(v - vals[0]) for v in vals]
    den = exps[0] + exps[1] + exps[2] + exps[3]
    gate_ref[...] = jnp.concatenate([e / den for e in exps], axis=0)
    eidx_ref[...] = jnp.concatenate(idxs, axis=0)

    sel = jnp.zeros((N_EXPERTS, tt), F32)
    for hot in hots:
        sel = sel + jnp.where(hot, 1.0, 0.0)
    before = jnp.where(_iota((tt, tt), 0) < _iota((tt, tt), 1), 1.0, 0.0).astype(BF16)
    carry = carry_ref[...]
    prefix = jnp.dot(sel.astype(BF16), before, preferred_element_type=F32) + carry[:, 0:1]
    rank_ref[...] = jnp.concatenate(
        [jnp.sum(jnp.where(hot, prefix, 0.0), axis=0, keepdims=True) for hot in hots], axis=0).astype(I32)
    cnt = jnp.broadcast_to(jnp.sum(sel, axis=1, keepdims=True), (N_EXPERTS, 128))
    base_ref[0] = carry
    cnt_ref[0] = cnt
    carry_ref[...] = carry + cnt


def _route(logits_t, tt):
    t = logits_t.shape[1]
    nt = t // tt
    row4 = pl.BlockSpec((TOP_K, tt), lambda i: (0, i))
    per_tile = pl.BlockSpec((1, N_EXPERTS, 128), lambda i: (i, 0, 0))
    return pl.pallas_call(
        functools.partial(_route_kernel, tt=tt),
        grid=(nt,),
        in_specs=[pl.BlockSpec((N_EXPERTS, tt), lambda i: (0, i))],
        out_specs=[row4, row4, row4, per_tile, per_tile],
        out_shape=[jax.ShapeDtypeStruct((TOP_K, t), I32), jax.ShapeDtypeStruct((TOP_K, t), F32),
                   jax.ShapeDtypeStruct((TOP_K, t), I32),
                   jax.ShapeDtypeStruct((nt, N_EXPERTS, 128), F32),
                   jax.ShapeDtypeStruct((nt, N_EXPERTS, 128), F32)],
        scratch_shapes=[pltpu.VMEM((N_EXPERTS, 128), F32)],
        compiler_params=pltpu.CompilerParams(dimension_semantics=("arbitrary",)),
        name="route",
    )(logits_t)


def _count_le(sorted_vals, queries):
    return jnp.sum((sorted_vals[None, :] <= queries[:, None]).astype(I32), axis=1)


def _routing_plan(base, cnt, eidx, rank, tt):
    nt = base.shape[0]
    t = nt * tt
    n_mb = (t * TOP_K) // EXPERT_BLOCK + N_EXPERTS
    n_sb = n_mb * (EXPERT_BLOCK // SLOT_BLOCK)
    counts = jnp.sum(cnt, axis=0)
    padded = ((counts + EXPERT_BLOCK - 1) // EXPERT_BLOCK) * EXPERT_BLOCK
    end_pad = jnp.cumsum(padded)
    start_pad = end_pad - padded
    hot = eidx[:, :, None] == jnp.arange(N_EXPERTS, dtype=I32)[None, None, :]
    dest = jnp.sum(jnp.where(hot, start_pad[None, None, :], 0), axis=-1) + rank

    mb_start = jnp.arange(n_mb, dtype=I32) * EXPERT_BLOCK
    mb_expert = jnp.minimum(_count_le(end_pad, mb_start), N_EXPERTS - 1).astype(I32)
    mb_active = (mb_start < end_pad[-1]).astype(I32)

    sb_start = jnp.arange(n_sb, dtype=I32) * SLOT_BLOCK
    sb_e = jnp.minimum(_count_le(end_pad, sb_start), N_EXPERTS - 1)
    r_lo = sb_start - start_pad[sb_e]
    sb_valid = (r_lo >= 0) & (r_lo < counts[sb_e])
    r_hi = jnp.minimum(r_lo + SLOT_BLOCK - 1, counts[sb_e] - 1)
    base_rows = base.T[sb_e]
    t_lo = jnp.sum(base_rows <= r_lo[:, None], axis=1) - 1
    t_hi = jnp.sum(base_rows <= r_hi[:, None], axis=1) - 1
    npairs = jnp.where(sb_valid, t_hi - t_lo + 1, 1)
    d_end = jnp.cumsum(npairs)
    d_off = d_end - npairs
    max_d = n_sb + N_EXPERTS * nt
    pid = jnp.arange(max_d, dtype=I32)
    pj = jnp.minimum(_count_le(d_end, pid), n_sb - 1)
    in_range = pid < d_end[-1]
    pi = jnp.where(in_range & sb_valid[pj], t_lo[pj] + pid - d_off[pj], 0)
    pi = jnp.where(in_range, pi, pi[jnp.maximum(d_end[-1] - 1, 0)])
    d_first = (in_range & (pid == d_off[pj])).astype(I32)
    d_valid = (in_range & sb_valid[pj]).astype(I32)
    dispatch = (pj.astype(I32), jnp.clip(pi, 0, nt - 1).astype(I32), d_first, d_valid)

    lo = (start_pad[None, :] + base).reshape(-1)
    n = cnt.reshape(-1)
    nb = jnp.where(n > 0, (lo + n - 1) // SLOT_BLOCK - lo // SLOT_BLOCK + 1, 0)
    c_end = jnp.cumsum(nb)
    c_off = c_end - nb
    max_c = n_sb + N_EXPERTS * nt
    pid = jnp.arange(max_c, dtype=I32)
    q = jnp.minimum(_count_le(c_end, pid), nt * N_EXPERTS - 1)
    in_range = pid < c_end[-1]
    cj = lo[q] // SLOT_BLOCK + pid - c_off[q]
    cj = jnp.where(in_range, cj, cj[jnp.maximum(c_end[-1] - 1, 0)])
    ci = jnp.where(in_range, q // N_EXPERTS, nt - 1)
    tile_first = c_off.reshape(nt, N_EXPERTS)[:, 0]
    tile_last = c_end.reshape(nt, N_EXPERTS)[:, -1] - 1
    c_first = (in_range & (pid == tile_first[ci])).astype(I32)
    c_last = (in_range & (pid == tile_last[ci])).astype(I32)
    combine = (ci.astype(I32), jnp.clip(cj, 0, n_sb - 1).astype(I32), c_first, c_last, in_range.astype(I32))
    return dest.astype(I32), mb_expert, mb_active, dispatch, combine, n_mb, n_sb


def _dispatch_kernel(pj_ref, pi_ref, first_ref, valid_ref, hn_ref, dest_ref, o_ref, *, tt):
    p = pl.program_id(0)

    @pl.when(first_ref[p] == 1)
    def _():
        o_ref[...] = jnp.zeros_like(o_ref)

    @pl.when(valid_ref[p] == 1)
    def _():
        slot = pj_ref[p] * SLOT_BLOCK + _iota((SLOT_BLOCK, tt), 0)
        d = dest_ref[...]
        hot = jnp.zeros((SLOT_BLOCK, tt), F32)
        for k in range(TOP_K):
            hot = hot + jnp.where(d[k:k + 1] == slot, 1.0, 0.0)
        o_ref[...] += jnp.dot(hot.astype(BF16), hn_ref[...], preferred_element_type=F32).astype(BF16)


def _dispatch(hn, dest, plan, n_sb, tt):
    pj, pi, first, valid = plan
    return pl.pallas_call(
        functools.partial(_dispatch_kernel, tt=tt),
        grid_spec=pltpu.PrefetchScalarGridSpec(
            num_scalar_prefetch=4,
            grid=(pj.shape[0],),
            in_specs=[pl.BlockSpec((tt, D_MODEL), lambda p, pj, pi, f, v: (pi[p], 0)),
                      pl.BlockSpec((TOP_K, tt), lambda p, pj, pi, f, v: (0, pi[p]))],
            out_specs=pl.BlockSpec((SLOT_BLOCK, D_MODEL), lambda p, pj, pi, f, v: (pj[p], 0))),
        out_shape=jax.ShapeDtypeStruct((n_sb * SLOT_BLOCK, D_MODEL), BF16),
        compiler_params=pltpu.CompilerParams(dimension_semantics=("arbitrary",)),
        name="dispatch",
    )(pj, pi, first, valid, hn, dest)


def _expert_kernel(e_ref, act_ref, x_ref, wgu_ref, wd_ref, bg_ref, bl_ref, bd_ref, o_ref,
                   wg_c, wl_c, wd_c):
    mb = pl.program_id(0)
    new_expert = jnp.logical_or(mb == 0, e_ref[mb] != e_ref[jnp.maximum(mb - 1, 0)])

    @pl.when(jnp.logical_and(new_expert, act_ref[mb] == 1))
    def _():
        lane = _iota((D_MODEL, 128), 1)
        half = lane < 64
        idx = jnp.where(half, 2 * lane, 2 * (lane - 64) + 1)
        for m in range(D_MODEL // 128):
            a = jnp.take_along_axis(wgu_ref[0, :, (2 * m) * 128:(2 * m + 1) * 128], idx, axis=1)
            b = jnp.take_along_axis(wgu_ref[0, :, (2 * m + 1) * 128:(2 * m + 2) * 128], idx, axis=1)
            cols = slice(m * 128, (m + 1) * 128)
            wg_c[:, cols] = jnp.where(half, a, pltpu.roll(b, 64, 1)).astype(BF16)
            wl_c[:, cols] = jnp.where(half, pltpu.roll(a, 64, 1), b).astype(BF16)
        wd_c[...] = wd_ref[0].astype(BF16)

    @pl.when(act_ref[mb] == 0)
    def _():
        o_ref[...] = jnp.zeros_like(o_ref)

    @pl.when(act_ref[mb] == 1)
    def _():
        x = x_ref[...]
        glu = jnp.dot(x, wg_c[...], preferred_element_type=F32) + bg_ref[0]
        lin = jnp.dot(x, wl_c[...], preferred_element_type=F32) + bl_ref[0]
        glu = jnp.minimum(glu, SWIGLU_LIMIT)
        lin = jnp.clip(lin, -SWIGLU_LIMIT, SWIGLU_LIMIT)
        act = glu * jax.nn.sigmoid(SWIGLU_ALPHA * glu) * (lin + 1.0)
        o_ref[...] = (_mm(act, wd_c[...]) + bd_ref[0]).astype(BF16)


def _experts(xb, mb_expert, mb_active, w_gu, w_down, bg, bl, bd, n_mb):
    d_ff = w_down.shape[1]
    assert d_ff == D_MODEL and w_gu.shape[1:] == (D_MODEL, 2 * d_ff)
    bspec = pl.BlockSpec((1, 1, D_MODEL), lambda m, e, a: (e[m], 0, 0))
    xspec = pl.BlockSpec((EXPERT_BLOCK, D_MODEL), lambda m, e, a: (m, 0))
    wcache = pltpu.VMEM((D_MODEL, D_MODEL), BF16)
    return pl.pallas_call(
        _expert_kernel,
        grid_spec=pltpu.PrefetchScalarGridSpec(
            num_scalar_prefetch=2,
            grid=(n_mb,),
            in_specs=[xspec,
                      pl.BlockSpec((1, D_MODEL, 2 * d_ff), lambda m, e, a: (e[m], 0, 0)),
                      pl.BlockSpec((1, d_ff, D_MODEL), lambda m, e, a: (e[m], 0, 0)),
                      bspec, bspec, bspec],
            out_specs=xspec,
            scratch_shapes=[wcache, wcache, wcache]),
        out_shape=jax.ShapeDtypeStruct(xb.shape, BF16),
        compiler_params=pltpu.CompilerParams(dimension_semantics=("arbitrary",),
                                             vmem_limit_bytes=VMEM_LIMIT),
        name="experts",
    )(mb_expert, mb_active, xb, w_gu, w_down, bg, bl, bd)


def _combine_kernel(ci_ref, cj_ref, first_ref, last_ref, valid_ref, yb_ref, dest_ref, gate_ref, x1_ref,
                    nw_ref, o_ref, acc_ref, *, tt):
    p = pl.program_id(0)

    @pl.when(first_ref[p] == 1)
    def _():
        acc_ref[...] = jnp.zeros_like(acc_ref)

    @pl.when(valid_ref[p] == 1)
    def _():
        slot = cj_ref[p] * SLOT_BLOCK + _iota((tt, SLOT_BLOCK), 1)
        d = dest_ref[...]
        g = gate_ref[...]
        w = jnp.zeros((tt, SLOT_BLOCK), F32)
        for k in range(TOP_K):
            w = w + jnp.where(d[:, k:k + 1] == slot, g[:, k:k + 1], 0.0)
        acc_ref[...] += jnp.dot(w.astype(BF16), yb_ref[...], preferred_element_type=F32)

    @pl.when(last_ref[p] == 1)
    def _():
        y = x1_ref[...] + acc_ref[...]
        o_ref[...] = y * lax.rsqrt(jnp.mean(y * y, axis=-1, keepdims=True) + EPS) * nw_ref[...]


def _combine(yb, dest_t, gate_t, x1, norm_w, plan, tt):
    ci, cj, first, last, valid = plan
    t = x1.shape[0]
    tile = lambda shape: pl.BlockSpec(shape, lambda p, ci, cj, f, l, v: (ci[p], 0))
    return pl.pallas_call(
        functools.partial(_combine_kernel, tt=tt),
        grid_spec=pltpu.PrefetchScalarGridSpec(
            num_scalar_prefetch=5,
            grid=(ci.shape[0],),
            in_specs=[pl.BlockSpec((SLOT_BLOCK, D_MODEL), lambda p, ci, cj, f, l, v: (cj[p], 0)),
                      tile((tt, TOP_K)), tile((tt, TOP_K)), tile((tt, D_MODEL)),
                      pl.BlockSpec((1, D_MODEL), lambda p, ci, cj, f, l, v: (0, 0))],
            out_specs=tile((tt, D_MODEL)),
            scratch_shapes=[pltpu.VMEM((tt, D_MODEL), F32)]),
        out_shape=jax.ShapeDtypeStruct((t, D_MODEL), F32),
        compiler_params=pltpu.CompilerParams(dimension_semantics=("arbitrary",)),
        name="combine",
    )(ci, cj, first, last, valid, yb, dest_t, gate_t, x1, norm_w.reshape(1, D_MODEL))


def _moe(x1, hn, logits_t, w_gu, b_gu, w_down, b_down, norm_final):
    t = x1.shape[0]
    tt = min(512, t)
    eidx, gate, rank, base, cnt = _route(logits_t, tt)
    base = base[:, :, 0].astype(I32)
    cnt = cnt[:, :, 0].astype(I32)
    dest, mb_expert, mb_active, d_plan, c_plan, n_mb, n_sb = _routing_plan(base, cnt, eidx, rank, tt)
    xb = _dispatch(hn, dest, d_plan, n_sb, tt)
    bg = b_gu[:, None, 0::2]
    bl = b_gu[:, None, 1::2]
    yb = _experts(xb, mb_expert, mb_active, w_gu, w_down, bg, bl, b_down[:, None, :], n_mb)
    return _combine(yb, dest.T, gate.T, x1, norm_final, c_plan, tt)


def kernel(x, norm_mix, w_in, gdn_conv, gdn_A_log, gdn_dt_bias, gdn_norm, rwkv_mu, rwkv_w0, rwkv_w2, rwkv_a0, rwkv_a2, rwkv_g2, rwkv_k_k, rwkv_k_a, rwkv_r_k, rwkv_ln_w, rwkv_ln_b, proj_a, proj_b, w_out, norm_ffn, router_w, router_b, w_gate_up, b_gate_up, w_down, b_down, norm_final):
    bsz, seq, d = x.shape
    depth = w_in.shape[0]
    x2 = x.reshape(bsz * seq, d)
    out = None
    for l in range(depth):
        w = w_in[l]
        w_pack = jnp.concatenate([w[:, 0:4096], w[:, 4112:7184], w[:, 7440:9488], w[:, 7184:7440],
                                  w[:, 4096:4112], jnp.zeros((d, PACK_COLS - 9488), w.dtype)],
                                 axis=1).astype(BF16)
        p = _in_proj(x2, norm_mix[l], w_pack)
        ya = _gdn(p, gdn_conv[l], gdn_A_log[l], gdn_dt_bias[l], gdn_norm[l], bsz, seq)
        yb = _rwkv(p, rwkv_mu[l], rwkv_w0[l], rwkv_w2[l], rwkv_a0[l], rwkv_a2[l], rwkv_g2[l],
                   rwkv_k_k[l], rwkv_k_a[l], rwkv_r_k[l], rwkv_ln_w[l], rwkv_ln_b[l], bsz, seq)
        x1, hn, logits_t = _merge(x2, ya, yb, p, proj_a[l], proj_b[l], w_out[l], norm_ffn[l],
                                  router_w[l], router_b[l])
        assert l == depth - 1, "only the final layer's residual is fused with the output norm"
        out = _moe(x1, hn, logits_t, w_gate_up[l], b_gate_up[l], w_down[l], b_down[l], norm_final)
    return out.reshape(bsz, seq, d)
```

```python
import functools

import jax
import jax.numpy as jnp
import numpy as np
from jax import lax
from jax.experimental import pallas as pl
from jax.experimental.pallas import tpu as pltpu

F32 = jnp.float32
BF16 = jnp.bfloat16
I32 = jnp.int32
HIGHEST = lax.Precision.HIGHEST

D_MODEL = 1024
EPS = 1e-6
CHUNK = 64
GDN_HEADS = 8
GDN_DK = 128
CONV_W = 4
RWKV_HEADS = 16
RWKV_N = 64
RWKV_GN_EPS = 64e-5
LORA_COLS = 256
N_EXPERTS = 32
TOP_K = 4
SWIGLU_ALPHA = 1.702
SWIGLU_LIMIT = 7.0

PACK_COLS = 9600
COL_BA = 9472
HEADS_PER_GROUP = 4
PACK_W = HEADS_PER_GROUP * CHUNK
GDN_CHUNKS_PER_ITER = 2

SLOT_BLOCK = 128
EXPERT_BLOCK = 512
VMEM_LIMIT = 48 * 1024 * 1024


def _mm(a, b):
    return jnp.dot(a.astype(BF16), b.astype(BF16), preferred_element_type=F32)


def _mm_nt(a, b):
    return lax.dot_general(a.astype(BF16), b.astype(BF16), (((1,), (1,)), ((), ())),
                           preferred_element_type=F32)


def _mm_tn(a, b):
    return lax.dot_general(a.astype(BF16), b.astype(BF16), (((0,), (0,)), ((), ())),
                           preferred_element_type=F32)


def _split_bf16(x, terms):
    parts = []
    for _ in range(terms - 1):
        hi = x.astype(BF16)
        parts.append(hi)
        x = x - hi.astype(F32)
    parts.append(x.astype(BF16))
    return parts


def _mm_exact_lhs(a_bf16, b):
    out = None
    for part in _split_bf16(b, 3):
        d = jnp.dot(a_bf16, part, preferred_element_type=F32)
        out = d if out is None else out + d
    return out


def _iota(shape, dim):
    return lax.broadcasted_iota(I32, shape, dim)


def _softplus(x):
    return jnp.maximum(x, 0.0) + jnp.log(1.0 + jnp.exp(-jnp.abs(x)))


def _stack_rows(x, n):
    return jnp.concatenate([x] * n, axis=0)


def _tri_inv_packed(ms, eye, bdmask):
    def bd(x):
        return jnp.where(bdmask, _stack_rows(x, HEADS_PER_GROUP), jnp.zeros((), BF16))

    def mul(a, b):
        ah, al = _split_bf16(a, 2)
        bh, bl = _split_bf16(b, 2)
        bdh = bd(bh)
        return (jnp.dot(ah, bdh, preferred_element_type=F32) + jnp.dot(al, bdh, preferred_element_type=F32)
                + jnp.dot(ah, bd(bl), preferred_element_type=F32))

    ts = [eye - m for m in ms]
    xs = [mul(m, m) for m in ms]
    for _ in range(4):
        rs = [mul(jnp.concatenate([t, x], axis=0), x) for t, x in zip(ts, xs)]
        ts = [t + r[:CHUNK] for t, r in zip(ts, rs)]
        xs = [r[CHUNK:] for r in rs]
    return [t + mul(t, x) for t, x in zip(ts, xs)]


def _shift_rows(x, prev8, k, row8):
    r = pltpu.roll(x, k, 0)
    pr = pltpu.roll(prev8, k, 0)
    head = jnp.where(row8 < k, pr, r[:8])
    return jnp.concatenate([head, r[8:]], axis=0)


def _in_proj_kernel(x_ref, nw_ref, w_ref, o_ref, h_ref):
    @pl.when(pl.program_id(1) == 0)
    def _():
        x = x_ref[...]
        y = x * lax.rsqrt(jnp.mean(x * x, axis=-1, keepdims=True) + EPS)
        h_ref[...] = (y * nw_ref[...]).astype(BF16)

    o_ref[...] = jnp.dot(h_ref[...], w_ref[...], preferred_element_type=F32)


def _in_proj(x2, norm_w, w_pack):
    t = x2.shape[0]
    tm = min(512, t)
    tn = 1920
    return pl.pallas_call(
        _in_proj_kernel,
        grid=(t // tm, PACK_COLS // tn),
        in_specs=[pl.BlockSpec((tm, D_MODEL), lambda i, j: (i, 0)),
                  pl.BlockSpec((1, D_MODEL), lambda i, j: (0, 0)),
                  pl.BlockSpec((D_MODEL, tn), lambda i, j: (0, j))],
        out_specs=pl.BlockSpec((tm, tn), lambda i, j: (i, j)),
        out_shape=jax.ShapeDtypeStruct((t, PACK_COLS), F32),
        scratch_shapes=[pltpu.VMEM((tm, D_MODEL), BF16)],
        compiler_params=pltpu.CompilerParams(dimension_semantics=("parallel", "arbitrary"),
                                             vmem_limit_bytes=VMEM_LIMIT),
        name="in_proj",
    )(x2, norm_w.reshape(1, D_MODEL), w_pack)


def _gdn_kernel(q_ref, k_ref, v_ref, z_ref, ba_ref, conv_ref, alog_ref, dtb_ref, nw_ref, o_ref,
                carry_ref, qn_ref, kn_ref, vv_ref, beta_ref, g_ref, s_ref, *, ts):
    @pl.when(pl.program_id(1) == 0)
    def _():
        carry_ref[...] = jnp.zeros_like(carry_ref)
        s_ref[...] = jnp.zeros_like(s_ref)

    row8 = _iota((8, D_MODEL), 0)

    def conv_silu(idx, x_ref):
        x = x_ref[...]
        prev8 = carry_ref[idx]
        w4 = conv_ref[:, idx * D_MODEL:(idx + 1) * D_MODEL]
        y = x * w4[CONV_W - 1:CONV_W]
        for k in range(1, CONV_W):
            y = y + _shift_rows(x, prev8, k, row8) * w4[CONV_W - 1 - k:CONV_W - k]
        carry_ref[idx] = x[ts - 8:ts]
        return y * jax.nn.sigmoid(y)

    def l2norm_heads(x, scale):
        parts = []
        for h in range(GDN_HEADS):
            xh = x[:, h * GDN_DK:(h + 1) * GDN_DK]
            parts.append(xh * (lax.rsqrt(jnp.sum(xh * xh, axis=-1, keepdims=True) + EPS) * scale))
        return jnp.concatenate(parts, axis=1)

    qn_ref[...] = l2norm_heads(conv_silu(0, q_ref), GDN_DK ** -0.5)
    kn_ref[...] = l2norm_heads(conv_silu(1, k_ref), 1.0)
    vv_ref[...] = conv_silu(2, v_ref)
    ba = ba_ref[...]
    beta_ref[...] = jax.nn.sigmoid(ba)
    g_ref[...] = -jnp.exp(alog_ref[...]) * _softplus(ba + dtb_ref[...])

    gw = HEADS_PER_GROUP * GDN_DK
    lane_p = _iota((CHUNK, PACK_W), 1)
    row_p = _iota((CHUNK, PACK_W), 0)
    eye = jnp.where((lane_p % CHUNK) == row_p, 1.0, 0.0).astype(F32)
    strict = (lane_p % CHUNK) < row_p
    incl = (lane_p % CHUNK) <= row_p
    bd_pp = (_iota((PACK_W, PACK_W), 0) // CHUNK) == (_iota((PACK_W, PACK_W), 1) // CHUNK)
    bd_pn = (_iota((PACK_W, gw), 0) // CHUNK) == (_iota((PACK_W, gw), 1) // GDN_DK)
    bd_pn2 = (_iota((PACK_W, 2 * gw), 0) // CHUNK) == ((_iota((PACK_W, 2 * gw), 1) % gw) // GDN_DK)
    ltri = jnp.where(_iota((CHUNK, CHUNK), 1) <= _iota((CHUNK, CHUNK), 0), 1.0, 0.0).astype(BF16)
    ones_cc = jnp.ones((CHUNK, CHUNK), BF16)
    lane128 = _iota((CHUNK, 128), 1)
    nw = nw_ref[...]

    n_groups = GDN_HEADS // HEADS_PER_GROUP
    hcols = [slice(j * GDN_DK, (j + 1) * GDN_DK) for j in range(HEADS_PER_GROUP)]

    def intra_chunk(rows_list):
        units = [(i, gi) for i in range(len(rows_list)) for gi in range(n_groups)]
        gcs = [_mm_exact_lhs(ltri, g_ref[rows, :]) for rows in rows_list]
        beta = [beta_ref[rows, :] for rows in rows_list]
        pre = []
        for i, gi in units:
            rows = rows_list[i]
            heads = [gi * HEADS_PER_GROUP + j for j in range(HEADS_PER_GROUP)]
            cols = slice(gi * gw, (gi + 1) * gw)
            beta_nat = jnp.concatenate(
                [jnp.broadcast_to(beta[i][:, h:h + 1], (CHUNK, GDN_DK)) for h in heads], axis=1)
            gcol = [jnp.broadcast_to(gcs[i][:, 8 + h:9 + h], (CHUNK, GDN_DK)) for h in heads]
            gcol_p = jnp.concatenate([jnp.where(lane128 < CHUNK, gcol[0], gcol[1]),
                                      jnp.where(lane128 < CHUNK, gcol[2], gcol[3])], axis=1)
            pre.append(dict(qn=qn_ref[rows, cols], kn=kn_ref[rows, cols], vv=vv_ref[rows, cols],
                            beta_nat=beta_nat, gc_nat=jnp.concatenate(gcol, axis=1), gcol_p=gcol_p))
        grow = [_mm_exact_lhs(ones_cc, u["gcol_p"] * eye) for u in pre]
        aas = [_mm_nt(jnp.concatenate([u["kn"] * u["beta_nat"], u["qn"]], axis=0),
                      jnp.where(bd_pn, _stack_rows(u["kn"], HEADS_PER_GROUP), 0.0)) for u in pre]
        decay = [jnp.exp(jnp.minimum(u["gcol_p"] - gr, 0.0)) for u, gr in zip(pre, grow)]
        t_inv = _tri_inv_packed([jnp.where(strict, aa[:CHUNK] * d, 0.0) for aa, d in zip(aas, decay)],
                                eye, bd_pp)
        out = []
        for u, aa, d, ti in zip(pre, aas, decay, t_inv):
            egc = jnp.exp(u["gc_nat"])
            data = jnp.concatenate([u["vv"] * u["beta_nat"], u["kn"] * u["beta_nat"] * egc], axis=1)
            uw = _mm(ti, jnp.where(bd_pn2, _stack_rows(data, HEADS_PER_GROUP), 0.0))
            gl = u["gc_nat"][CHUNK - 1:CHUNK, :]
            out.append(dict(uw=uw, attn=jnp.where(incl, aa[CHUNK:] * d, 0.0), qd=u["qn"] * egc,
                            kd=u["kn"] * jnp.exp(gl - u["gc_nat"]), egl=jnp.exp(gl)))
        return out

    def recurrence(rows, group_units):
        wq = {}
        for gi, u in enumerate(group_units):
            for j, hc in enumerate(hcols):
                h = gi * HEADS_PER_GROUP + j
                wq[h] = _mm(jnp.concatenate([u["uw"][:, gw + j * GDN_DK:gw + (j + 1) * GDN_DK],
                                             u["qd"][:, hc]], axis=0), s_ref[h])
        vnew = {}
        for gi, u in enumerate(group_units):
            for j, hc in enumerate(hcols):
                h = gi * HEADS_PER_GROUP + j
                vnew[h] = u["uw"][:, hc] - wq[h][:CHUNK]
                s_ref[h] = s_ref[h] * u["egl"][:, hc] + _mm_tn(u["kd"][:, hc], vnew[h])
        for gi, u in enumerate(group_units):
            heads = [gi * HEADS_PER_GROUP + j for j in range(HEADS_PER_GROUP)]
            vn = jnp.concatenate([vnew[h] for h in heads], axis=1)
            o = jnp.concatenate([wq[h][CHUNK:] for h in heads], axis=1) + _mm(
                u["attn"], jnp.where(bd_pn, _stack_rows(vn, HEADS_PER_GROUP), 0.0))
            for j, h in enumerate(heads):
                oh = o[:, hcols[j]]
                oh = oh * lax.rsqrt(jnp.mean(oh * oh, axis=-1, keepdims=True) + EPS) * nw
                zh = z_ref[rows, h * GDN_DK:(h + 1) * GDN_DK]
                o_ref[rows, h * GDN_DK:(h + 1) * GDN_DK] = (oh * (zh * jax.nn.sigmoid(zh))).astype(BF16)

    def chunk_pair_body(c, carry):
        rows_list = [pl.ds(pl.multiple_of((c * GDN_CHUNKS_PER_ITER + i) * CHUNK, CHUNK), CHUNK)
                     for i in range(GDN_CHUNKS_PER_ITER)]
        units = intra_chunk(rows_list)
        for i, rows in enumerate(rows_list):
            recurrence(rows, units[i * n_groups:(i + 1) * n_groups])
        return carry

    lax.fori_loop(0, ts // (CHUNK * GDN_CHUNKS_PER_ITER), chunk_pair_body, 0)


def _gdn(p, conv_w, a_log, dt_bias, norm_w, bsz, seq):
    ts = min(256, seq)
    ns = seq // ts
    pad8 = lambda v: jnp.zeros((1, 128), F32).at[0, 8:16].set(v)
    tok = lambda col: pl.BlockSpec((ts, D_MODEL), lambda b, s, col=col: (b * ns + s, col))
    const = lambda shape: pl.BlockSpec(shape, lambda b, s: (0,) * len(shape))
    return pl.pallas_call(
        functools.partial(_gdn_kernel, ts=ts),
        grid=(bsz, ns),
        in_specs=[tok(0), tok(1), tok(2), tok(3),
                  pl.BlockSpec((ts, 128), lambda b, s: (b * ns + s, COL_BA // 128)),
                  const((CONV_W, 3 * D_MODEL)), const((1, 128)), const((1, 128)), const((1, GDN_DK))],
        out_specs=pl.BlockSpec((ts, D_MODEL), lambda b, s: (b * ns + s, 0)),
        out_shape=jax.ShapeDtypeStruct((bsz * seq, D_MODEL), BF16),
        scratch_shapes=[pltpu.VMEM((3, 8, D_MODEL), F32),
                        pltpu.VMEM((ts, D_MODEL), F32), pltpu.VMEM((ts, D_MODEL), F32),
                        pltpu.VMEM((ts, D_MODEL), F32),
                        pltpu.VMEM((ts, 128), F32), pltpu.VMEM((ts, 128), F32),
                        pltpu.VMEM((GDN_HEADS, GDN_DK, GDN_DK), F32)],
        compiler_params=pltpu.CompilerParams(dimension_semantics=("parallel", "arbitrary"),
                                             vmem_limit_bytes=VMEM_LIMIT),
        name="gdn",
    )(p, p, p, p, p, conv_w, pad8(a_log), pad8(dt_bias), norm_w.reshape(1, GDN_DK))


def _rwkv_kernel(r_ref, k_ref, v_ref, lo_ref, mu_ref, w0_ref, w2_ref, a0_ref, a2_ref, g2_ref,
                 kk_ref, ka_ref, rk_ref, lnw_ref, lnb_ref, seg_ref, o_ref,
                 carry_ref, rs_ref, ks_ref, vs_ref, kks_ref, bs_ref, ld_ref, gg_ref, os_ref, st_ref,
                 *, ts):
    @pl.when(pl.program_id(1) == 0)
    def _():
        carry_ref[...] = jnp.zeros_like(carry_ref)
        st_ref[...] = jnp.zeros_like(st_ref)

    width = D_MODEL
    row8 = _iota((8, width), 0)
    row8l = _iota((8, LORA_COLS), 0)

    def seg_sum(x):
        hi = x.astype(BF16)
        lo = (x - hi.astype(F32)).astype(BF16)
        parts = []
        for g in range(width // PACK_W):
            cs = slice(g * PACK_W, (g + 1) * PACK_W)
            parts.append(jnp.dot(hi[:, cs], seg_ref[...], preferred_element_type=F32)
                         + jnp.dot(lo[:, cs], seg_ref[...], preferred_element_type=F32))
        return jnp.concatenate(parts, axis=1)

    def lerp(idx, x_ref, mu, r8):
        x = x_ref[...]
        xs = x + (_shift_rows(x, carry_ref[idx, :, :x.shape[1]], 1, r8) - x) * mu
        carry_ref[idx, :, :x.shape[1]] = x[ts - 8:ts]
        return xs

    r = lerp(0, r_ref, mu_ref[:, 0:width], row8)
    k = lerp(1, k_ref, mu_ref[:, width:2 * width], row8)
    v = lerp(2, v_ref, mu_ref[:, 2 * width:3 * width], row8)
    lo = lerp(3, lo_ref, mu_ref[:, 3 * width:3 * width + LORA_COLS], row8l)
    lo_a = lo[:, :128]
    w_log = -_softplus(-(w0_ref[...] + _mm(jnp.tanh(lo_a), w2_ref[...]))) - 0.5
    ld_ref[...] = -jnp.exp(w_log)
    aa = jax.nn.sigmoid(a0_ref[...] + _mm(lo_a, a2_ref[...]))
    gg_ref[...] = _mm(jax.nn.sigmoid(lo[:, 128:]), g2_ref[...])
    kx = k * kk_ref[...]
    kk = kx * lax.rsqrt(seg_sum(kx * kx) + EPS)
    k = k * (1.0 + (aa - 1.0) * ka_ref[...])
    rs_ref[...] = r
    ks_ref[...] = k
    vs_ref[...] = v
    kks_ref[...] = kk
    bs_ref[...] = kk * aa

    lane_p = _iota((CHUNK, PACK_W), 1)
    row_p = _iota((CHUNK, PACK_W), 0)
    eye = jnp.where((lane_p % CHUNK) == row_p, 1.0, 0.0).astype(F32)
    strict = (lane_p % CHUNK) < row_p
    incl = (lane_p % CHUNK) <= row_p
    bd = (_iota((PACK_W, PACK_W), 0) // CHUNK) == (_iota((PACK_W, PACK_W), 1) // CHUNK)
    bd2 = jnp.concatenate([bd, bd], axis=1)
    ltri = jnp.where(_iota((CHUNK, CHUNK), 1) <= _iota((CHUNK, CHUNK), 0), 1.0, 0.0).astype(BF16)

    def bdiag(x):
        return jnp.where(bd, _stack_rows(x, HEADS_PER_GROUP), 0.0)

    n_groups = width // PACK_W
    gcols = [slice(g * PACK_W, (g + 1) * PACK_W) for g in range(n_groups)]

    def chunk_body(c, carry):
        rows = pl.ds(pl.multiple_of(c * CHUNK, CHUNK), CHUNK)
        ld = [ld_ref[rows, cols] for cols in gcols]
        cs = [_mm_exact_lhs(ltri, x) for x in ld]
        pre = []
        for g, cols in enumerate(gcols):
            cl = cs[g][CHUNK - 1:CHUNK, :]
            e_neg = jnp.exp(-cs[g])
            e_dec = jnp.exp(cl - cs[g])
            kc, bc = ks_ref[rows, cols], bs_ref[rows, cols]
            rt = rs_ref[rows, cols] * jnp.exp(cs[g])
            kt = kks_ref[rows, cols] * jnp.exp(cs[g] - ld[g])
            pre.append(dict(cl=cl, rt=rt, kt=kt, vc=vs_ref[rows, cols], kh=kc * e_neg, bh=bc * e_neg,
                            kdec=kc * e_dec, bdec=bc * e_dec))
        akb = [_mm_nt(jnp.concatenate([u["kt"], u["rt"]], axis=0),
                      jnp.concatenate([bdiag(u["kh"]), bdiag(u["bh"])], axis=0)) for u in pre]
        t_inv = _tri_inv_packed([jnp.where(strict, a[:CHUNK, PACK_W:], 0.0) for a in akb], eye, bd)
        xo = [_mm(jnp.concatenate([jnp.where(strict, a[:CHUNK, :PACK_W], 0.0),
                                   jnp.where(incl, a[CHUNK:, :PACK_W], 0.0)], axis=0), bdiag(u["vc"]))
              for a, u in zip(akb, pre)]
        tt = [_mm(ti, jnp.where(bd2, _stack_rows(jnp.concatenate([x[:CHUNK], u["kt"]], axis=1),
                                                 HEADS_PER_GROUP), 0.0))
              for ti, x, u in zip(t_inv, xo, pre)]
        rr = [_mm(jnp.where(incl, a[CHUNK:, PACK_W:], 0.0),
                  jnp.where(bd2, _stack_rows(jnp.concatenate([t[:, PACK_W:], t[:, :PACK_W]], axis=1),
                                             HEADS_PER_GROUP), 0.0))
              for a, t in zip(akb, tt)]
        uo = [_mm_nt(jnp.concatenate([t[:, PACK_W:], u["rt"] - r[:, :PACK_W]], axis=0), st_ref[g])
              for g, (t, r, u) in enumerate(zip(tt, rr, pre))]
        upd = []
        for g, cols in enumerate(gcols):
            os_ref[rows, cols] = xo[g][CHUNK:] - rr[g][:, PACK_W:] + uo[g][CHUNK:]
            u_full = tt[g][:, :PACK_W] + uo[g][:CHUNK]
            upd.append(_mm_tn(jnp.concatenate([pre[g]["vc"], u_full], axis=0),
                              jnp.concatenate([pre[g]["kdec"], -pre[g]["bdec"]], axis=0)))
        for g in range(n_groups):
            st_ref[g] = st_ref[g] * jnp.exp(pre[g]["cl"]) + jnp.where(bd, upd[g], 0.0)
        return carry

    lax.fori_loop(0, ts // CHUNK, chunk_body, 0)

    o = os_ref[...]
    inv_n = 1.0 / RWKV_N
    mean = seg_sum(o) * inv_n
    cen = o - mean
    var = seg_sum(cen * cen) * inv_n
    o = cen * lax.rsqrt(var + RWKV_GN_EPS) * lnw_ref[...] + lnb_ref[...]
    bonus = seg_sum(rs_ref[...] * ks_ref[...] * rk_ref[...]) * vs_ref[...]
    o_ref[...] = ((o + bonus) * gg_ref[...]).astype(BF16)


def _rwkv(p, mu, w0, w2, a0, a2, g2, k_k, k_a, r_k, ln_w, ln_b, bsz, seq):
    ts = min(256, seq)
    ns = seq // ts
    width = D_MODEL
    row = lambda v: v.reshape(1, -1)
    w2p = jnp.concatenate([w2, jnp.zeros_like(w2)], axis=0).astype(BF16)
    a2p = jnp.concatenate([jnp.zeros_like(a2), a2], axis=0).astype(BF16)
    seg = (np.arange(PACK_W)[:, None] // RWKV_N == np.arange(PACK_W)[None, :] // RWKV_N)
    seg = jnp.asarray(seg, BF16)
    tok = lambda col: pl.BlockSpec((ts, width), lambda b, s, col=col: (b * ns + s, col))
    const = lambda shape: pl.BlockSpec(shape, lambda b, s: (0,) * len(shape))
    fbuf = lambda: pltpu.VMEM((ts, width), F32)
    return pl.pallas_call(
        functools.partial(_rwkv_kernel, ts=ts),
        grid=(bsz, ns),
        in_specs=[tok(4), tok(5), tok(6),
                  pl.BlockSpec((ts, LORA_COLS), lambda b, s: (b * ns + s, 9216 // LORA_COLS)),
                  const((1, 3 * width + LORA_COLS)), const((1, width)), const((128, width)),
                  const((1, width)), const((128, width)), const((128, width)),
                  const((1, width)), const((1, width)), const((1, width)), const((1, width)),
                  const((1, width)), const((PACK_W, PACK_W))],
        out_specs=pl.BlockSpec((ts, width), lambda b, s: (b * ns + s, 0)),
        out_shape=jax.ShapeDtypeStruct((bsz * seq, width), BF16),
        scratch_shapes=[pltpu.VMEM((4, 8, width), F32),
                        fbuf(), fbuf(), fbuf(), fbuf(), fbuf(), fbuf(), fbuf(), fbuf(),
                        pltpu.VMEM((width // PACK_W, PACK_W, PACK_W), F32)],
        compiler_params=pltpu.CompilerParams(dimension_semantics=("parallel", "arbitrary"),
                                             vmem_limit_bytes=VMEM_LIMIT),
        name="rwkv",
    )(p, p, p, p, row(mu), row(w0), w2p, row(a0), a2p, g2.astype(BF16), row(k_k), row(k_a),
      row(r_k), row(ln_w), row(ln_b), seg)


def _merge_kernel(x_ref, ya_ref, yb_ref, ga_ref, gb_ref, pa_ref, pb_ref, wo_ref, nw_ref, rw_ref, rb_ref,
                  x1_ref, hn_ref, lg_ref):
    merged = (jax.nn.sigmoid(ga_ref[...]) * jnp.dot(ya_ref[...], pa_ref[...], preferred_element_type=F32)
              + jax.nn.sigmoid(gb_ref[...]) * jnp.dot(yb_ref[...], pb_ref[...], preferred_element_type=F32))
    x1 = x_ref[...] + _mm(merged, wo_ref[...])
    x1_ref[...] = x1
    hn = x1 * lax.rsqrt(jnp.mean(x1 * x1, axis=-1, keepdims=True) + EPS) * nw_ref[...]
    hn_ref[...] = hn.astype(BF16)
    lg_ref[...] = lax.dot_general(rw_ref[...], hn, (((1,), (1,)), ((), ())),
                                  preferred_element_type=F32, precision=HIGHEST) + rb_ref[...]


def _merge(x2, ya, yb, p, proj_a, proj_b, w_out, norm_w, router_w, router_b):
    t = x2.shape[0]
    tm = min(512, t)
    tok = lambda col: pl.BlockSpec((tm, D_MODEL), lambda i, col=col: (i, col))
    const = lambda shape: pl.BlockSpec(shape, lambda i: (0,) * len(shape))
    return pl.pallas_call(
        _merge_kernel,
        grid=(t // tm,),
        in_specs=[tok(0), tok(0), tok(0), tok(7), tok(8),
                  const((D_MODEL, D_MODEL)), const((D_MODEL, D_MODEL)), const((D_MODEL, D_MODEL)),
                  const((1, D_MODEL)), const((N_EXPERTS, D_MODEL)), const((N_EXPERTS, 1))],
        out_specs=[tok(0), tok(0), pl.BlockSpec((N_EXPERTS, tm), lambda i: (0, i))],
        out_shape=[jax.ShapeDtypeStruct((t, D_MODEL), F32), jax.ShapeDtypeStruct((t, D_MODEL), BF16),
                   jax.ShapeDtypeStruct((N_EXPERTS, t), F32)],
        compiler_params=pltpu.CompilerParams(dimension_semantics=("parallel",),
                                             vmem_limit_bytes=VMEM_LIMIT),
        name="merge",
    )(x2, ya, yb, p, p, proj_a.astype(BF16), proj_b.astype(BF16), w_out.astype(BF16),
      norm_w.reshape(1, D_MODEL), router_w.T, router_b.reshape(N_EXPERTS, 1))


def _route_kernel(lg_ref, eidx_ref, gate_ref, rank_ref, base_ref, cnt_ref, carry_ref, *, tt):
    @pl.when(pl.program_id(0) == 0)
    def _():
        carry_ref[...] = jnp.zeros_like(carry_ref)

    l = lg_ref[...]
    ie = _iota((N_EXPERTS, tt), 0)
    vals, hots, idxs = [], [], []
    for _ in range(TOP_K):
        m = jnp.max(l, axis=0, keepdims=True)
        idx = jnp.min(jnp.where(l == m, ie, N_EXPERTS), axis=0, keepdims=True)
        hot = ie == idx
        vals.append(m)
        hots.append(hot)
        idxs.append(idx)
        l = jnp.where(hot, -jnp.inf, l)
    exps = [jnp.exp(v - vals[0]) for v in vals]
    den = exps[0] + exps[1] + exps[2] + exps[3]
    gate_ref[...] = jnp.concatenate([e / den for e in exps], axis=0)
    eidx_ref[...] = jnp.concatenate(idxs, axis=0)

    sel = jnp.zeros((N_EXPERTS, tt), F32)
    for hot in hots:
        sel = sel + jnp.where(hot, 1.0, 0.0)
    before = jnp.where(_iota((tt, tt), 0) < _iota((tt, tt), 1), 1.0, 0.0).astype(BF16)
    carry = carry_ref[...]
    prefix = jnp.dot(sel.astype(BF16), before, preferred_element_type=F32) + carry[:, 0:1]
    rank_ref[...] = jnp.concatenate(
        [jnp.sum(jnp.where(hot, prefix, 0.0), axis=0, keepdims=True) for hot in hots], axis=0).astype(I32)
    cnt = jnp.broadcast_to(jnp.sum(sel, axis=1, keepdims=True), (N_EXPERTS, 128))
    base_ref[0] = carry
    cnt_ref[0] = cnt
    carry_ref[...] = carry + cnt


def _route(logits_t, tt):
    t = logits_t.shape[1]
    nt = t // tt
    row4 = pl.BlockSpec((TOP_K, tt), lambda i: (0, i))
    per_tile = pl.BlockSpec((1, N_EXPERTS, 128), lambda i: (i, 0, 0))
    return pl.pallas_call(
        functools.partial(_route_kernel, tt=tt),
        grid=(nt,),
        in_specs=[pl.BlockSpec((N_EXPERTS, tt), lambda i: (0, i))],
        out_specs=[row4, row4, row4, per_tile, per_tile],
        out_shape=[jax.ShapeDtypeStruct((TOP_K, t), I32), jax.ShapeDtypeStruct((TOP_K, t), F32),
                   jax.ShapeDtypeStruct((TOP_K, t), I32),
                   jax.ShapeDtypeStruct((nt, N_EXPERTS, 128), F32),
                   jax.ShapeDtypeStruct((nt, N_EXPERTS, 128), F32)],
        scratch_shapes=[pltpu.VMEM((N_EXPERTS, 128), F32)],
        compiler_params=pltpu.CompilerParams(dimension_semantics=("arbitrary",)),
        name="route",
    )(logits_t)


def _count_le(sorted_vals, queries):
    return jnp.sum((sorted_vals[None, :] <= queries[:, None]).astype(I32), axis=1)


def _routing_plan(base, cnt, eidx, rank, tt):
    nt = base.shape[0]
    t = nt * tt
    n_mb = (t * TOP_K) // EXPERT_BLOCK + N_EXPERTS
    n_sb = n_mb * (EXPERT_BLOCK // SLOT_BLOCK)
    counts = jnp.sum(cnt, axis=0)
    padded = ((counts + EXPERT_BLOCK - 1) // EXPERT_BLOCK) * EXPERT_BLOCK
    end_pad = jnp.cumsum(padded)
    start_pad = end_pad - padded
    hot = eidx[:, :, None] == jnp.arange(N_EXPERTS, dtype=I32)[None, None, :]
    dest = jnp.sum(jnp.where(hot, start_pad[None, None, :], 0), axis=-1) + rank

    mb_start = jnp.arange(n_mb, dtype=I32) * EXPERT_BLOCK
    mb_expert = jnp.minimum(_count_le(end_pad, mb_start), N_EXPERTS - 1).astype(I32)
    mb_active = (mb_start < end_pad[-1]).astype(I32)

    sb_start = jnp.arange(n_sb, dtype=I32) * SLOT_BLOCK
    sb_e = jnp.minimum(_count_le(end_pad, sb_start), N_EXPERTS - 1)
    r_lo = sb_start - start_pad[sb_e]
    sb_valid = (r_lo >= 0) & (r_lo < counts[sb_e])
    r_hi = jnp.minimum(r_lo + SLOT_BLOCK - 1, counts[sb_e] - 1)
    base_rows = base.T[sb_e]
    t_lo = jnp.sum(base_rows <= r_lo[:, None], axis=1) - 1
    t_hi = jnp.sum(base_rows <= r_hi[:, None], axis=1) - 1
    npairs = jnp.where(sb_valid, t_hi - t_lo + 1, 1)
    d_end = jnp.cumsum(npairs)
    d_off = d_end - npairs
    max_d = n_sb + N_EXPERTS * nt
    pid = jnp.arange(max_d, dtype=I32)
    pj = jnp.minimum(_count_le(d_end, pid), n_sb - 1)
    in_range = pid < d_end[-1]
    pi = jnp.where(in_range & sb_valid[pj], t_lo[pj] + pid - d_off[pj], 0)
    pi = jnp.where(in_range, pi, pi[jnp.maximum(d_end[-1] - 1, 0)])
    d_first = (in_range & (pid == d_off[pj])).astype(I32)
    d_valid = (in_range & sb_valid[pj]).astype(I32)
    dispatch = (pj.astype(I32), jnp.clip(pi, 0, nt - 1).astype(I32), d_first, d_valid)

    lo = (start_pad[None, :] + base).reshape(-1)
    n = cnt.reshape(-1)
    nb = jnp.where(n > 0, (lo + n - 1) // SLOT_BLOCK - lo // SLOT_BLOCK + 1, 0)
    c_end = jnp.cumsum(nb)
    c_off = c_end - nb
    max_c = n_sb + N_EXPERTS * nt
    pid = jnp.arange(max_c, dtype=I32)
    q = jnp.minimum(_count_le(c_end, pid), nt * N_EXPERTS - 1)
    in_range = pid < c_end[-1]
    cj = lo[q] // SLOT_BLOCK + pid - c_off[q]
    cj = jnp.where(in_range, cj, cj[jnp.maximum(c_end[-1] - 1, 0)])
    ci = jnp.where(in_range, q // N_EXPERTS, nt - 1)
    tile_first = c_off.reshape(nt, N_EXPERTS)[:, 0]
    tile_last = c_end.reshape(nt, N_EXPERTS)[:, -1] - 1
    c_first = (in_range & (pid == tile_first[ci])).astype(I32)
    c_last = (in_range & (pid == tile_last[ci])).astype(I32)
    combine = (ci.astype(I32), jnp.clip(cj, 0, n_sb - 1).astype(I32), c_first, c_last, in_range.astype(I32))
    return dest.astype(I32), mb_expert, mb_active, dispatch, combine, n_mb, n_sb


def _dispatch_kernel(pj_ref, pi_ref, first_ref, valid_ref, hn_ref, dest_ref, o_ref, *, tt):
    p = pl.program_id(0)

    @pl.when(first_ref[p] == 1)
    def _():
        o_ref[...] = jnp.zeros_like(o_ref)

    @pl.when(valid_ref[p] == 1)
    def _():
        slot = pj_ref[p] * SLOT_BLOCK + _iota((SLOT_BLOCK, tt), 0)
        d = dest_ref[...]
        hot = jnp.zeros((SLOT_BLOCK, tt), F32)
        for k in range(TOP_K):
            hot = hot + jnp.where(d[k:k + 1] == slot, 1.0, 0.0)
        o_ref[...] += jnp.dot(hot.astype(BF16), hn_ref[...], preferred_element_type=F32).astype(BF16)


def _dispatch(hn, dest, plan, n_sb, tt):
    pj, pi, first, valid = plan
    return pl.pallas_call(
        functools.partial(_dispatch_kernel, tt=tt),
        grid_spec=pltpu.PrefetchScalarGridSpec(
            num_scalar_prefetch=4,
            grid=(pj.shape[0],),
            in_specs=[pl.BlockSpec((tt, D_MODEL), lambda p, pj, pi, f, v: (pi[p], 0)),
                      pl.BlockSpec((TOP_K, tt), lambda p, pj, pi, f, v: (0, pi[p]))],
            out_specs=pl.BlockSpec((SLOT_BLOCK, D_MODEL), lambda p, pj, pi, f, v: (pj[p], 0))),
        out_shape=jax.ShapeDtypeStruct((n_sb * SLOT_BLOCK, D_MODEL), BF16),
        compiler_params=pltpu.CompilerParams(dimension_semantics=("arbitrary",)),
        name="dispatch",
    )(pj, pi, first, valid, hn, dest)


def _expert_kernel(e_ref, act_ref, x_ref, wgu_ref, wd_ref, bg_ref, bl_ref, bd_ref, o_ref,
                   wg_c, wl_c, wd_c):
    mb = pl.program_id(0)
    new_expert = jnp.logical_or(mb == 0, e_ref[mb] != e_ref[jnp.maximum(mb - 1, 0)])

    @pl.when(jnp.logical_and(new_expert, act_ref[mb] == 1))
    def _():
        lane = _iota((D_MODEL, 128), 1)
        half = lane < 64
        idx = jnp.where(half, 2 * lane, 2 * (lane - 64) + 1)
        for m in range(D_MODEL // 128):
            a = jnp.take_along_axis(wgu_ref[0, :, (2 * m) * 128:(2 * m + 1) * 128], idx, axis=1)
            b = jnp.take_along_axis(wgu_ref[0, :, (2 * m + 1) * 128:(2 * m + 2) * 128], idx, axis=1)
            cols = slice(m * 128, (m + 1) * 128)
            wg_c[:, cols] = jnp.where(half, a, pltpu.roll(b, 64, 1)).astype(BF16)
            wl_c[:, cols] = jnp.where(half, pltpu.roll(a, 64, 1), b).astype(BF16)
        wd_c[...] = wd_ref[0].astype(BF16)

    @pl.when(act_ref[mb] == 0)
    def _():
        o_ref[...] = jnp.zeros_like(o_ref)

    @pl.when(act_ref[mb] == 1)
    def _():
        x = x_ref[...]
        glu = jnp.dot(x, wg_c[...], preferred_element_type=F32) + bg_ref[0]
        lin = jnp.dot(x, wl_c[...], preferred_element_type=F32) + bl_ref[0]
        glu = jnp.minimum(glu, SWIGLU_LIMIT)
        lin = jnp.clip(lin, -SWIGLU_LIMIT, SWIGLU_LIMIT)
        act = glu * jax.nn.sigmoid(SWIGLU_ALPHA * glu) * (lin + 1.0)
        o_ref[...] = (_mm(act, wd_c[...]) + bd_ref[0]).astype(BF16)


def _experts(xb, mb_expert, mb_active, w_gu, w_down, bg, bl, bd, n_mb):
    d_ff = w_down.shape[1]
    assert d_ff == D_MODEL and w_gu.shape[1:] == (D_MODEL, 2 * d_ff)
    bspec = pl.BlockSpec((1, 1, D_MODEL), lambda m, e, a: (e[m], 0, 0))
    xspec = pl.BlockSpec((EXPERT_BLOCK, D_MODEL), lambda m, e, a: (m, 0))
    wcache = pltpu.VMEM((D_MODEL, D_MODEL), BF16)
    return pl.pallas_call(
        _expert_kernel,
        grid_spec=pltpu.PrefetchScalarGridSpec(
            num_scalar_prefetch=2,
            grid=(n_mb,),
            in_specs=[xspec,
                      pl.BlockSpec((1, D_MODEL, 2 * d_ff), lambda m, e, a: (e[m], 0, 0)),
                      pl.BlockSpec((1, d_ff, D_MODEL), lambda m, e, a: (e[m], 0, 0)),
                      bspec, bspec, bspec],
            out_specs=xspec,
            scratch_shapes=[wcache, wcache, wcache]),
        out_shape=jax.ShapeDtypeStruct(xb.shape, BF16),
        compiler_params=pltpu.CompilerParams(dimension_semantics=("arbitrary",),
                                             vmem_limit_bytes=VMEM_LIMIT),
        name="experts",
    )(mb_expert, mb_active, xb, w_gu, w_down, bg, bl, bd)


def _combine_kernel(ci_ref, cj_ref, first_ref, last_ref, valid_ref, yb_ref, dest_ref, gate_ref, x1_ref,
                    nw_ref, o_ref, acc_ref, *, tt):
    p = pl.program_id(0)

    @pl.when(first_ref[p] == 1)
    def _():
        acc_ref[...] = jnp.zeros_like(acc_ref)

    @pl.when(valid_ref[p] == 1)
    def _():
        slot = cj_ref[p] * SLOT_BLOCK + _iota((tt, SLOT_BLOCK), 1)
        d = dest_ref[...]
        g = gate_ref[...]
        w = jnp.zeros((tt, SLOT_BLOCK), F32)
        for k in range(TOP_K):
            w = w + jnp.where(d[:, k:k + 1] == slot, g[:, k:k + 1], 0.0)
        acc_ref[...] += jnp.dot(w.astype(BF16), yb_ref[...], preferred_element_type=F32)

    @pl.when(last_ref[p] == 1)
    def _():
        y = x1_ref[...] + acc_ref[...]
        o_ref[...] = y * lax.rsqrt(jnp.mean(y * y, axis=-1, keepdims=True) + EPS) * nw_ref[...]


def _combine(yb, dest_t, gate_t, x1, norm_w, plan, tt):
    ci, cj, first, last, valid = plan
    t = x1.shape[0]
    tile = lambda shape: pl.BlockSpec(shape, lambda p, ci, cj, f, l, v: (ci[p], 0))
    return pl.pallas_call(
        functools.partial(_combine_kernel, tt=tt),
        grid_spec=pltpu.PrefetchScalarGridSpec(
            num_scalar_prefetch=5,
            grid=(ci.shape[0],),
            in_specs=[pl.BlockSpec((SLOT_BLOCK, D_MODEL), lambda p, ci, cj, f, l, v: (cj[p], 0)),
                      tile((tt, TOP_K)), tile((tt, TOP_K)), tile((tt, D_MODEL)),
                      pl.BlockSpec((1, D_MODEL), lambda p, ci, cj, f, l, v: (0, 0))],
            out_specs=tile((tt, D_MODEL)),
            scratch_shapes=[pltpu.VMEM((tt, D_MODEL), F32)]),
        out_shape=jax.ShapeDtypeStruct((t, D_MODEL), F32),
        compiler_params=pltpu.CompilerParams(dimension_semantics=("arbitrary",)),
        name="combine",
    )(ci, cj, first, last, valid, yb, dest_t, gate_t, x1, norm_w.reshape(1, D_MODEL))


def _moe(x1, hn, logits_t, w_gu, b_gu, w_down, b_down, norm_final):
    t = x1.shape[0]
    tt = min(512, t)
    eidx, gate, rank, base, cnt = _route(logits_t, tt)
    base = base[:, :, 0].astype(I32)
    cnt = cnt[:, :, 0].astype(I32)
    dest, mb_expert, mb_active, d_plan, c_plan, n_mb, n_sb = _routing_plan(base, cnt, eidx, rank, tt)
    xb = _dispatch(hn, dest, d_plan, n_sb, tt)
    bg = b_gu[:, None, 0::2]
    bl = b_gu[:, None, 1::2]
    yb = _experts(xb, mb_expert, mb_active, w_gu, w_down, bg, bl, b_down[:, None, :], n_mb)
    return _combine(yb, dest.T, gate.T, x1, norm_final, c_plan, tt)


def kernel(x, norm_mix, w_in, gdn_conv, gdn_A_log, gdn_dt_bias, gdn_norm, rwkv_mu, rwkv_w0, rwkv_w2, rwkv_a0, rwkv_a2, rwkv_g2, rwkv_k_k, rwkv_k_a, rwkv_r_k, rwkv_ln_w, rwkv_ln_b, proj_a, proj_b, w_out, norm_ffn, router_w, router_b, w_gate_up, b_gate_up, w_down, b_down, norm_final):
    bsz, seq, d = x.shape
    depth = w_in.shape[0]
    x2 = x.reshape(bsz * seq, d)
    out = None
    for l in range(depth):
        w = w_in[l]
        w_pack = jnp.concatenate([w[:, 0:4096], w[:, 4112:7184], w[:, 7440:9488], w[:, 7184:7440],
                                  w[:, 4096:4112], jnp.zeros((d, PACK_COLS - 9488), w.dtype)],
                                 axis=1).astype(BF16)
        p = _in_proj(x2, norm_mix[l], w_pack)
        ya = _gdn(p, gdn_conv[l], gdn_A_log[l], gdn_dt_bias[l], gdn_norm[l], bsz, seq)
        yb = _rwkv(p, rwkv_mu[l], rwkv_w0[l], rwkv_w2[l], rwkv_a0[l], rwkv_a2[l], rwkv_g2[l],
                   rwkv_k_k[l], rwkv_k_a[l], rwkv_r_k[l], rwkv_ln_w[l], rwkv_ln_b[l], bsz, seq)
        x1, hn, logits_t = _merge(x2, ya, yb, p, proj_a[l], proj_b[l], w_out[l], norm_ffn[l],
                                  router_w[l], router_b[l])
        assert l == depth - 1, "only the final layer's residual is fused with the output norm"
        out = _moe(x1, hn, logits_t, w_gate_up[l], b_gate_up[l], w_down[l], b_down[l], norm_final)
    return out.reshape(bsz, seq, d)
```

```python
import functools

import jax
import jax.numpy as jnp
import numpy as np
from jax import lax
from jax.experimental import pallas as pl
from jax.experimental.pallas import tpu as pltpu
from jax.experimental.pallas import tpu_sc as plsc

F32 = jnp.float32
BF16 = jnp.bfloat16
I32 = jnp.int32
U32 = jnp.uint32
HIGHEST = lax.Precision.HIGHEST

D_MODEL = 1024
EPS = 1e-6
CHUNK = 64
GDN_HEADS = 8
GDN_DK = 128
CONV_W = 4
RWKV_HEADS = 16
RWKV_N = 64
RWKV_GN_EPS = 64e-5
LORA_COLS = 256
N_EXPERTS = 32
TOP_K = 4
SWIGLU_ALPHA = 1.702
SWIGLU_LIMIT = 7.0

PACK_COLS = 9600
COL_BA = 9472
HEADS_PER_GROUP = 4
PACK_W = HEADS_PER_GROUP * CHUNK
GDN_CHUNKS_PER_ITER = 2

SC_GATHER_ROWS = 64
EXPERT_BLOCK = 512
VMEM_LIMIT = 48 * 1024 * 1024


def _mm(a, b):
    return jnp.dot(a.astype(BF16), b.astype(BF16), preferred_element_type=F32)


def _mm_nt(a, b):
    return lax.dot_general(a.astype(BF16), b.astype(BF16), (((1,), (1,)), ((), ())),
                           preferred_element_type=F32)


def _mm_tn(a, b):
    return lax.dot_general(a.astype(BF16), b.astype(BF16), (((0,), (0,)), ((), ())),
                           preferred_element_type=F32)


def _split_bf16(x, terms):
    parts = []
    for _ in range(terms - 1):
        hi = x.astype(BF16)
        parts.append(hi)
        x = x - hi.astype(F32)
    parts.append(x.astype(BF16))
    return parts


def _mm_exact_lhs(a_bf16, b):
    out = None
    for part in _split_bf16(b, 3):
        d = jnp.dot(a_bf16, part, preferred_element_type=F32)
        out = d if out is None else out + d
    return out


def _pack_bf16_pairs(x):
    half = x.shape[1] // 2
    lo = lax.bitcast_convert_type(x[:, :half].astype(BF16).astype(F32), U32) >> 16
    hi = lax.bitcast_convert_type(x[:, half:].astype(BF16).astype(F32), U32) & jnp.uint32(0xFFFF0000)
    return lo | hi


def _unpack_bf16_pairs(w):
    lo = lax.bitcast_convert_type(w << 16, F32)
    hi = lax.bitcast_convert_type(w & jnp.uint32(0xFFFF0000), F32)
    return jnp.concatenate([lo, hi], axis=1)


def _iota(shape, dim):
    return lax.broadcasted_iota(I32, shape, dim)


def _softplus(x):
    return jnp.maximum(x, 0.0) + jnp.log(1.0 + jnp.exp(-jnp.abs(x)))


def _stack_rows(x, n):
    return jnp.concatenate([x] * n, axis=0)


def _tri_inv_packed(ms, eye, bdmask):
    def bd(x):
        return jnp.where(bdmask, _stack_rows(x, HEADS_PER_GROUP), jnp.zeros((), BF16))

    def mul(a, b):
        ah, al = _split_bf16(a, 2)
        bh, bl = _split_bf16(b, 2)
        bdh = bd(bh)
        return (jnp.dot(ah, bdh, preferred_element_type=F32) + jnp.dot(al, bdh, preferred_element_type=F32)
                + jnp.dot(ah, bd(bl), preferred_element_type=F32))

    ts = [eye - m for m in ms]
    xs = [mul(m, m) for m in ms]
    for _ in range(4):
        rs = [mul(jnp.concatenate([t, x], axis=0), x) for t, x in zip(ts, xs)]
        ts = [t + r[:CHUNK] for t, r in zip(ts, rs)]
        xs = [r[CHUNK:] for r in rs]
    return [t + mul(t, x) for t, x in zip(ts, xs)]


def _shift_rows(x, prev8, k, row8):
    r = pltpu.roll(x, k, 0)
    pr = pltpu.roll(prev8, k, 0)
    head = jnp.where(row8 < k, pr, r[:8])
    return jnp.concatenate([head, r[8:]], axis=0)


def _in_proj_kernel(x_ref, nw_ref, w_ref, o_ref, h_ref):
    @pl.when(pl.program_id(1) == 0)
    def _():
        x = x_ref[...]
        y = x * lax.rsqrt(jnp.mean(x * x, axis=-1, keepdims=True) + EPS)
        h_ref[...] = (y * nw_ref[...]).astype(BF16)

    o_ref[...] = jnp.dot(h_ref[...], w_ref[...], preferred_element_type=F32)


def _in_proj(x2, norm_w, w_pack):
    t = x2.shape[0]
    tm = min(512, t)
    tn = 1920
    return pl.pallas_call(
        _in_proj_kernel,
        grid=(t // tm, PACK_COLS // tn),
        in_specs=[pl.BlockSpec((tm, D_MODEL), lambda i, j: (i, 0)),
                  pl.BlockSpec((1, D_MODEL), lambda i, j: (0, 0)),
                  pl.BlockSpec((D_MODEL, tn), lambda i, j: (0, j))],
        out_specs=pl.BlockSpec((tm, tn), lambda i, j: (i, j)),
        out_shape=jax.ShapeDtypeStruct((t, PACK_COLS), F32),
        scratch_shapes=[pltpu.VMEM((tm, D_MODEL), BF16)],
        compiler_params=pltpu.CompilerParams(dimension_semantics=("parallel", "arbitrary"),
                                             vmem_limit_bytes=VMEM_LIMIT),
        name="in_proj",
    )(x2, norm_w.reshape(1, D_MODEL), w_pack)


def _gdn_kernel(q_ref, k_ref, v_ref, z_ref, ba_ref, conv_ref, alog_ref, dtb_ref, nw_ref, o_ref,
                carry_ref, qn_ref, kn_ref, vv_ref, beta_ref, g_ref, s_ref, *, ts):
    @pl.when(pl.program_id(1) == 0)
    def _():
        carry_ref[...] = jnp.zeros_like(carry_ref)
        s_ref[...] = jnp.zeros_like(s_ref)

    row8 = _iota((8, D_MODEL), 0)

    def conv_silu(idx, x_ref):
        x = x_ref[...]
        prev8 = carry_ref[idx]
        w4 = conv_ref[:, idx * D_MODEL:(idx + 1) * D_MODEL]
        y = x * w4[CONV_W - 1:CONV_W]
        for k in range(1, CONV_W):
            y = y + _shift_rows(x, prev8, k, row8) * w4[CONV_W - 1 - k:CONV_W - k]
        carry_ref[idx] = x[ts - 8:ts]
        return y * jax.nn.sigmoid(y)

    def l2norm_heads(x, scale):
        parts = []
        for h in range(GDN_HEADS):
            xh = x[:, h * GDN_DK:(h + 1) * GDN_DK]
            parts.append(xh * (lax.rsqrt(jnp.sum(xh * xh, axis=-1, keepdims=True) + EPS) * scale))
        return jnp.concatenate(parts, axis=1)

    qn_ref[...] = l2norm_heads(conv_silu(0, q_ref), GDN_DK ** -0.5)
    kn_ref[...] = l2norm_heads(conv_silu(1, k_ref), 1.0)
    vv_ref[...] = conv_silu(2, v_ref)
    ba = ba_ref[...]
    beta_ref[...] = jax.nn.sigmoid(ba)
    g_ref[...] = -jnp.exp(alog_ref[...]) * _softplus(ba + dtb_ref[...])

    gw = HEADS_PER_GROUP * GDN_DK
    lane_p = _iota((CHUNK, PACK_W), 1)
    row_p = _iota((CHUNK, PACK_W), 0)
    eye = jnp.where((lane_p % CHUNK) == row_p, 1.0, 0.0).astype(F32)
    strict = (lane_p % CHUNK) < row_p
    incl = (lane_p % CHUNK) <= row_p
    bd_pp = (_iota((PACK_W, PACK_W), 0) // CHUNK) == (_iota((PACK_W, PACK_W), 1) // CHUNK)
    bd_pn = (_iota((PACK_W, gw), 0) // CHUNK) == (_iota((PACK_W, gw), 1) // GDN_DK)
    bd_pn2 = (_iota((PACK_W, 2 * gw), 0) // CHUNK) == ((_iota((PACK_W, 2 * gw), 1) % gw) // GDN_DK)
    ltri = jnp.where(_iota((CHUNK, CHUNK), 1) <= _iota((CHUNK, CHUNK), 0), 1.0, 0.0).astype(BF16)
    ones_cc = jnp.ones((CHUNK, CHUNK), BF16)
    lane128 = _iota((CHUNK, 128), 1)
    nw = nw_ref[...]

    n_groups = GDN_HEADS // HEADS_PER_GROUP
    hcols = [slice(j * GDN_DK, (j + 1) * GDN_DK) for j in range(HEADS_PER_GROUP)]

    def intra_chunk(rows_list):
        units = [(i, gi) for i in range(len(rows_list)) for gi in range(n_groups)]
        gcs = [_mm_exact_lhs(ltri, g_ref[rows, :]) for rows in rows_list]
        beta = [beta_ref[rows, :] for rows in rows_list]
        pre = []
        for i, gi in units:
            rows = rows_list[i]
            heads = [gi * HEADS_PER_GROUP + j for j in range(HEADS_PER_GROUP)]
            cols = slice(gi * gw, (gi + 1) * gw)
            beta_nat = jnp.concatenate(
                [jnp.broadcast_to(beta[i][:, h:h + 1], (CHUNK, GDN_DK)) for h in heads], axis=1)
            gcol = [jnp.broadcast_to(gcs[i][:, 8 + h:9 + h], (CHUNK, GDN_DK)) for h in heads]
            gcol_p = jnp.concatenate([jnp.where(lane128 < CHUNK, gcol[0], gcol[1]),
                                      jnp.where(lane128 < CHUNK, gcol[2], gcol[3])], axis=1)
            pre.append(dict(qn=qn_ref[rows, cols], kn=kn_ref[rows, cols], vv=vv_ref[rows, cols],
                            beta_nat=beta_nat, gc_nat=jnp.concatenate(gcol, axis=1), gcol_p=gcol_p))
        grow = [_mm_exact_lhs(ones_cc, u["gcol_p"] * eye) for u in pre]
        aas = [_mm_nt(jnp.concatenate([u["kn"] * u["beta_nat"], u["qn"]], axis=0),
                      jnp.where(bd_pn, _stack_rows(u["kn"], HEADS_PER_GROUP), 0.0)) for u in pre]
        decay = [jnp.exp(jnp.minimum(u["gcol_p"] - gr, 0.0)) for u, gr in zip(pre, grow)]
        t_inv = _tri_inv_packed([jnp.where(strict, aa[:CHUNK] * d, 0.0) for aa, d in zip(aas, decay)],
                                eye, bd_pp)
        out = []
        for u, aa, d, ti in zip(pre, aas, decay, t_inv):
            egc = jnp.exp(u["gc_nat"])
            data = jnp.concatenate([u["vv"] * u["beta_nat"], u["kn"] * u["beta_nat"] * egc], axis=1)
            uw = _mm(ti, jnp.where(bd_pn2, _stack_rows(data, HEADS_PER_GROUP), 0.0))
            gl = u["gc_nat"][CHUNK - 1:CHUNK, :]
            out.append(dict(uw=uw, attn=jnp.where(incl, aa[CHUNK:] * d, 0.0), qd=u["qn"] * egc,
                            kd=u["kn"] * jnp.exp(gl - u["gc_nat"]), egl=jnp.exp(gl)))
        return out

    def recurrence(rows, group_units):
        wq = {}
        for gi, u in enumerate(group_units):
            for j, hc in enumerate(hcols):
                h = gi * HEADS_PER_GROUP + j
                wq[h] = _mm(jnp.concatenate([u["uw"][:, gw + j * GDN_DK:gw + (j + 1) * GDN_DK],
                                             u["qd"][:, hc]], axis=0), s_ref[h])
        vnew = {}
        for gi, u in enumerate(group_units):
            for j, hc in enumerate(hcols):
                h = gi * HEADS_PER_GROUP + j
                vnew[h] = u["uw"][:, hc] - wq[h][:CHUNK]
                s_ref[h] = s_ref[h] * u["egl"][:, hc] + _mm_tn(u["kd"][:, hc], vnew[h])
        for gi, u in enumerate(group_units):
            heads = [gi * HEADS_PER_GROUP + j for j in range(HEADS_PER_GROUP)]
            vn = jnp.concatenate([vnew[h] for h in heads], axis=1)
            o = jnp.concatenate([wq[h][CHUNK:] for h in heads], axis=1) + _mm(
                u["attn"], jnp.where(bd_pn, _stack_rows(vn, HEADS_PER_GROUP), 0.0))
            for j, h in enumerate(heads):
                oh = o[:, hcols[j]]
                oh = oh * lax.rsqrt(jnp.mean(oh * oh, axis=-1, keepdims=True) + EPS) * nw
                zh = z_ref[rows, h * GDN_DK:(h + 1) * GDN_DK]
                o_ref[rows, h * GDN_DK:(h + 1) * GDN_DK] = (oh * (zh * jax.nn.sigmoid(zh))).astype(BF16)

    def chunk_pair_body(c, carry):
        rows_list = [pl.ds(pl.multiple_of((c * GDN_CHUNKS_PER_ITER + i) * CHUNK, CHUNK), CHUNK)
                     for i in range(GDN_CHUNKS_PER_ITER)]
        units = intra_chunk(rows_list)
        for i, rows in enumerate(rows_list):
            recurrence(rows, units[i * n_groups:(i + 1) * n_groups])
        return carry

    lax.fori_loop(0, ts // (CHUNK * GDN_CHUNKS_PER_ITER), chunk_pair_body, 0)


def _gdn(p, conv_w, a_log, dt_bias, norm_w, bsz, seq):
    ts = min(256, seq)
    ns = seq // ts
    pad8 = lambda v: jnp.zeros((1, 128), F32).at[0, 8:16].set(v)
    tok = lambda col: pl.BlockSpec((ts, D_MODEL), lambda b, s, col=col: (b * ns + s, col))
    const = lambda shape: pl.BlockSpec(shape, lambda b, s: (0,) * len(shape))
    return pl.pallas_call(
        functools.partial(_gdn_kernel, ts=ts),
        grid=(bsz, ns),
        in_specs=[tok(0), tok(1), tok(2), tok(3),
                  pl.BlockSpec((ts, 128), lambda b, s: (b * ns + s, COL_BA // 128)),
                  const((CONV_W, 3 * D_MODEL)), const((1, 128)), const((1, 128)), const((1, GDN_DK))],
        out_specs=pl.BlockSpec((ts, D_MODEL), lambda b, s: (b * ns + s, 0)),
        out_shape=jax.ShapeDtypeStruct((bsz * seq, D_MODEL), BF16),
        scratch_shapes=[pltpu.VMEM((3, 8, D_MODEL), F32),
                        pltpu.VMEM((ts, D_MODEL), F32), pltpu.VMEM((ts, D_MODEL), F32),
                        pltpu.VMEM((ts, D_MODEL), F32),
                        pltpu.VMEM((ts, 128), F32), pltpu.VMEM((ts, 128), F32),
                        pltpu.VMEM((GDN_HEADS, GDN_DK, GDN_DK), F32)],
        compiler_params=pltpu.CompilerParams(dimension_semantics=("parallel", "arbitrary"),
                                             vmem_limit_bytes=VMEM_LIMIT),
        name="gdn",
    )(p, p, p, p, p, conv_w, pad8(a_log), pad8(dt_bias), norm_w.reshape(1, GDN_DK))


def _rwkv_kernel(r_ref, k_ref, v_ref, lo_ref, mu_ref, w0_ref, w2_ref, a0_ref, a2_ref, g2_ref,
                 kk_ref, ka_ref, rk_ref, lnw_ref, lnb_ref, seg_ref, o_ref,
                 carry_ref, rs_ref, ks_ref, vs_ref, kks_ref, bs_ref, ld_ref, gg_ref, os_ref, st_ref,
                 *, ts):
    @pl.when(pl.program_id(1) == 0)
    def _():
        carry_ref[...] = jnp.zeros_like(carry_ref)
        st_ref[...] = jnp.zeros_like(st_ref)

    width = D_MODEL
    row8 = _iota((8, width), 0)
    row8l = _iota((8, LORA_COLS), 0)

    def seg_sum(x):
        hi = x.astype(BF16)
        lo = (x - hi.astype(F32)).astype(BF16)
        parts = []
        for g in range(width // PACK_W):
            cs = slice(g * PACK_W, (g + 1) * PACK_W)
            parts.append(jnp.dot(hi[:, cs], seg_ref[...], preferred_element_type=F32)
                         + jnp.dot(lo[:, cs], seg_ref[...], preferred_element_type=F32))
        return jnp.concatenate(parts, axis=1)

    def lerp(idx, x_ref, mu, r8):
        x = x_ref[...]
        xs = x + (_shift_rows(x, carry_ref[idx, :, :x.shape[1]], 1, r8) - x) * mu
        carry_ref[idx, :, :x.shape[1]] = x[ts - 8:ts]
        return xs

    r = lerp(0, r_ref, mu_ref[:, 0:width], row8)
    k = lerp(1, k_ref, mu_ref[:, width:2 * width], row8)
    v = lerp(2, v_ref, mu_ref[:, 2 * width:3 * width], row8)
    lo = lerp(3, lo_ref, mu_ref[:, 3 * width:3 * width + LORA_COLS], row8l)
    lo_a = lo[:, :128]
    w_log = -_softplus(-(w0_ref[...] + _mm(jnp.tanh(lo_a), w2_ref[...]))) - 0.5
    ld_ref[...] = -jnp.exp(w_log)
    aa = jax.nn.sigmoid(a0_ref[...] + _mm(lo_a, a2_ref[...]))
    gg_ref[...] = _mm(jax.nn.sigmoid(lo[:, 128:]), g2_ref[...])
    kx = k * kk_ref[...]
    kk = kx * lax.rsqrt(seg_sum(kx * kx) + EPS)
    k = k * (1.0 + (aa - 1.0) * ka_ref[...])
    rs_ref[...] = r
    ks_ref[...] = k
    vs_ref[...] = v
    kks_ref[...] = kk
    bs_ref[...] = kk * aa

    lane_p = _iota((CHUNK, PACK_W), 1)
    row_p = _iota((CHUNK, PACK_W), 0)
    eye = jnp.where((lane_p % CHUNK) == row_p, 1.0, 0.0).astype(F32)
    strict = (lane_p % CHUNK) < row_p
    incl = (lane_p % CHUNK) <= row_p
    bd = (_iota((PACK_W, PACK_W), 0) // CHUNK) == (_iota((PACK_W, PACK_W), 1) // CHUNK)
    bd2 = jnp.concatenate([bd, bd], axis=1)
    ltri = jnp.where(_iota((CHUNK, CHUNK), 1) <= _iota((CHUNK, CHUNK), 0), 1.0, 0.0).astype(BF16)

    def bdiag(x):
        return jnp.where(bd, _stack_rows(x, HEADS_PER_GROUP), 0.0)

    n_groups = width // PACK_W
    gcols = [slice(g * PACK_W, (g + 1) * PACK_W) for g in range(n_groups)]

    def chunk_body(c, carry):
        rows = pl.ds(pl.multiple_of(c * CHUNK, CHUNK), CHUNK)
        ld = [ld_ref[rows, cols] for cols in gcols]
        cs = [_mm_exact_lhs(ltri, x) for x in ld]
        pre = []
        for g, cols in enumerate(gcols):
            cl = cs[g][CHUNK - 1:CHUNK, :]
            e_neg = jnp.exp(-cs[g])
            e_dec = jnp.exp(cl - cs[g])
            kc, bc = ks_ref[rows, cols], bs_ref[rows, cols]
            rt = rs_ref[rows, cols] * jnp.exp(cs[g])
            kt = kks_ref[rows, cols] * jnp.exp(cs[g] - ld[g])
            pre.append(dict(cl=cl, rt=rt, kt=kt, vc=vs_ref[rows, cols], kh=kc * e_neg, bh=bc * e_neg,
                            kdec=kc * e_dec, bdec=bc * e_dec))
        akb = [_mm_nt(jnp.concatenate([u["kt"], u["rt"]], axis=0),
                      jnp.concatenate([bdiag(u["kh"]), bdiag(u["bh"])], axis=0)) for u in pre]
        t_inv = _tri_inv_packed([jnp.where(strict, a[:CHUNK, PACK_W:], 0.0) for a in akb], eye, bd)
        xo = [_mm(jnp.concatenate([jnp.where(strict, a[:CHUNK, :PACK_W], 0.0),
                                   jnp.where(incl, a[CHUNK:, :PACK_W], 0.0)], axis=0), bdiag(u["vc"]))
              for a, u in zip(akb, pre)]
        tt = [_mm(ti, jnp.where(bd2, _stack_rows(jnp.concatenate([x[:CHUNK], u["kt"]], axis=1),
                                                 HEADS_PER_GROUP), 0.0))
              for ti, x, u in zip(t_inv, xo, pre)]
        rr = [_mm(jnp.where(incl, a[CHUNK:, PACK_W:], 0.0),
                  jnp.where(bd2, _stack_rows(jnp.concatenate([t[:, PACK_W:], t[:, :PACK_W]], axis=1),
                                             HEADS_PER_GROUP), 0.0))
              for a, t in zip(akb, tt)]
        uo = [_mm_nt(jnp.concatenate([t[:, PACK_W:], u["rt"] - r[:, :PACK_W]], axis=0), st_ref[g])
              for g, (t, r, u) in enumerate(zip(tt, rr, pre))]
        upd = []
        for g, cols in enumerate(gcols):
            os_ref[rows, cols] = xo[g][CHUNK:] - rr[g][:, PACK_W:] + uo[g][CHUNK:]
            u_full = tt[g][:, :PACK_W] + uo[g][:CHUNK]
            upd.append(_mm_tn(jnp.concatenate([pre[g]["vc"], u_full], axis=0),
                              jnp.concatenate([pre[g]["kdec"], -pre[g]["bdec"]], axis=0)))
        for g in range(n_groups):
            st_ref[g] = st_ref[g] * jnp.exp(pre[g]["cl"]) + jnp.where(bd, upd[g], 0.0)
        return carry

    lax.fori_loop(0, ts // CHUNK, chunk_body, 0)

    o = os_ref[...]
    inv_n = 1.0 / RWKV_N
    mean = seg_sum(o) * inv_n
    cen = o - mean
    var = seg_sum(cen * cen) * inv_n
    o = cen * lax.rsqrt(var + RWKV_GN_EPS) * lnw_ref[...] + lnb_ref[...]
    bonus = seg_sum(rs_ref[...] * ks_ref[...] * rk_ref[...]) * vs_ref[...]
    o_ref[...] = ((o + bonus) * gg_ref[...]).astype(BF16)


def _rwkv(p, mu, w0, w2, a0, a2, g2, k_k, k_a, r_k, ln_w, ln_b, bsz, seq):
    ts = min(256, seq)
    ns = seq // ts
    width = D_MODEL
    row = lambda v: v.reshape(1, -1)
    w2p = jnp.concatenate([w2, jnp.zeros_like(w2)], axis=0).astype(BF16)
    a2p = jnp.concatenate([jnp.zeros_like(a2), a2], axis=0).astype(BF16)
    seg = (np.arange(PACK_W)[:, None] // RWKV_N == np.arange(PACK_W)[None, :] // RWKV_N)
    seg = jnp.asarray(seg, BF16)
    tok = lambda col: pl.BlockSpec((ts, width), lambda b, s, col=col: (b * ns + s, col))
    const = lambda shape: pl.BlockSpec(shape, lambda b, s: (0,) * len(shape))
    fbuf = lambda: pltpu.VMEM((ts, width), F32)
    return pl.pallas_call(
        functools.partial(_rwkv_kernel, ts=ts),
        grid=(bsz, ns),
        in_specs=[tok(4), tok(5), tok(6),
                  pl.BlockSpec((ts, LORA_COLS), lambda b, s: (b * ns + s, 9216 // LORA_COLS)),
                  const((1, 3 * width + LORA_COLS)), const((1, width)), const((128, width)),
                  const((1, width)), const((128, width)), const((128, width)),
                  const((1, width)), const((1, width)), const((1, width)), const((1, width)),
                  const((1, width)), const((PACK_W, PACK_W))],
        out_specs=pl.BlockSpec((ts, width), lambda b, s: (b * ns + s, 0)),
        out_shape=jax.ShapeDtypeStruct((bsz * seq, width), BF16),
        scratch_shapes=[pltpu.VMEM((4, 8, width), F32),
                        fbuf(), fbuf(), fbuf(), fbuf(), fbuf(), fbuf(), fbuf(), fbuf(),
                        pltpu.VMEM((width // PACK_W, PACK_W, PACK_W), F32)],
        compiler_params=pltpu.CompilerParams(dimension_semantics=("parallel", "arbitrary"),
                                             vmem_limit_bytes=VMEM_LIMIT),
        name="rwkv",
    )(p, p, p, p, row(mu), row(w0), w2p, row(a0), a2p, g2.astype(BF16), row(k_k), row(k_a),
      row(r_k), row(ln_w), row(ln_b), seg)


def _merge_kernel(x_ref, ya_ref, yb_ref, ga_ref, gb_ref, pa_ref, pb_ref, wo_ref, nw_ref, rw_ref, rb_ref,
                  x1_ref, hn_ref, lg_ref):
    merged = (jax.nn.sigmoid(ga_ref[...]) * jnp.dot(ya_ref[...], pa_ref[...], preferred_element_type=F32)
              + jax.nn.sigmoid(gb_ref[...]) * jnp.dot(yb_ref[...], pb_ref[...], preferred_element_type=F32))
    x1 = x_ref[...] + _mm(merged, wo_ref[...])
    x1_ref[...] = x1
    hn = x1 * lax.rsqrt(jnp.mean(x1 * x1, axis=-1, keepdims=True) + EPS) * nw_ref[...]
    hn_ref[...] = _pack_bf16_pairs(hn)
    lg_ref[...] = lax.dot_general(rw_ref[...], hn, (((1,), (1,)), ((), ())),
                                  preferred_element_type=F32, precision=HIGHEST) + rb_ref[...]


def _merge(x2, ya, yb, p, proj_a, proj_b, w_out, norm_w, router_w, router_b):
    t = x2.shape[0]
    tm = min(512, t)
    tok = lambda col: pl.BlockSpec((tm, D_MODEL), lambda i, col=col: (i, col))
    const = lambda shape: pl.BlockSpec(shape, lambda i: (0,) * len(shape))
    return pl.pallas_call(
        _merge_kernel,
        grid=(t // tm,),
        in_specs=[tok(0), tok(0), tok(0), tok(7), tok(8),
                  const((D_MODEL, D_MODEL)), const((D_MODEL, D_MODEL)), const((D_MODEL, D_MODEL)),
                  const((1, D_MODEL)), const((N_EXPERTS, D_MODEL)), const((N_EXPERTS, 1))],
        out_specs=[tok(0), pl.BlockSpec((tm, D_MODEL // 2), lambda i: (i, 0)),
                   pl.BlockSpec((N_EXPERTS, tm), lambda i: (0, i))],
        out_shape=[jax.ShapeDtypeStruct((t, D_MODEL), F32), jax.ShapeDtypeStruct((t, D_MODEL // 2), U32),
                   jax.ShapeDtypeStruct((N_EXPERTS, t), F32)],
        compiler_params=pltpu.CompilerParams(dimension_semantics=("parallel",),
                                             vmem_limit_bytes=VMEM_LIMIT),
        name="merge",
    )(x2, ya, yb, p, p, proj_a.astype(BF16), proj_b.astype(BF16), w_out.astype(BF16),
      norm_w.reshape(1, D_MODEL), router_w.T, router_b.reshape(N_EXPERTS, 1))


def _route_kernel(lg_ref, eidx_ref, gate_ref, rank_ref, base_ref, cnt_ref, carry_ref, *, tt):
    @pl.when(pl.program_id(0) == 0)
    def _():
        carry_ref[...] = jnp.zeros_like(carry_ref)

    l = lg_ref[...]
    ie = _iota((N_EXPERTS, tt), 0)
    vals, hots, idxs = [], [], []
    for _ in range(TOP_K):
        m = jnp.max(l, axis=0, keepdims=True)
        idx = jnp.min(jnp.where(l == m, ie, N_EXPERTS), axis=0, keepdims=True)
        hot = ie == idx
        vals.append(m)
        hots.append(hot)
        idxs.append(idx)
        l = jnp.where(hot, -jnp.inf, l)
    exps = [jnp.exp(v - vals[0]) for v in vals]
    den = exps[0] + exps[1] + exps[2] + exps[3]
    gate_ref[...] = jnp.concatenate([e / den for e in exps], axis=0)
    eidx_ref[...] = jnp.concatenate(idxs, axis=0)

    sel = jnp.zeros((N_EXPERTS, tt), F32)
    for hot in hots:
        sel = sel + jnp.where(hot, 1.0, 0.0)
    before = jnp.where(_iota((tt, tt), 0) < _iota((tt, tt), 1), 1.0, 0.0).astype(BF16)
    carry = carry_ref[...]
    prefix = jnp.dot(sel.astype(BF16), before, preferred_element_type=F32) + carry[:, 0:1]
    rank_ref[...] = jnp.concatenate(
        [jnp.sum(jnp.where(hot, prefix, 0.0), axis=0, keepdims=True) for hot in hots], axis=0).astype(I32)
    cnt = jnp.broadcast_to(jnp.sum(sel, axis=1, keepdims=True), (N_EXPERTS, 128))
    base_ref[0] = carry
    cnt_ref[0] = cnt
    carry_ref[...] = carry + cnt


def _route(logits_t, tt):
    t = logits_t.shape[1]
    nt = t // tt
    row4 = pl.BlockSpec((TOP_K, tt), lambda i: (0, i))
    per_tile = pl.BlockSpec((1, N_EXPERTS, 128), lambda i: (i, 0, 0))
    return pl.pallas_call(
        functools.partial(_route_kernel, tt=tt),
        grid=(nt,),
        in_specs=[pl.BlockSpec((N_EXPERTS, tt), lambda i: (0, i))],
        out_specs=[row4, row4, row4, per_tile, per_tile],
        out_shape=[jax.ShapeDtypeStruct((TOP_K, t), I32), jax.ShapeDtypeStruct((TOP_K, t), F32),
                   jax.ShapeDtypeStruct((TOP_K, t), I32),
                   jax.ShapeDtypeStruct((nt, N_EXPERTS, 128), F32),
                   jax.ShapeDtypeStruct((nt, N_EXPERTS, 128), F32)],
        scratch_shapes=[pltpu.VMEM((N_EXPERTS, 128), F32)],
        compiler_params=pltpu.CompilerParams(dimension_semantics=("arbitrary",)),
        name="route",
    )(logits_t)


def _count_le(sorted_vals, queries):
    return jnp.sum((sorted_vals[None, :] <= queries[:, None]).astype(I32), axis=1)


def _routing_plan(cnt, eidx, rank, tt):
    t = cnt.shape[0] * tt
    n_mb = (t * TOP_K) // EXPERT_BLOCK + N_EXPERTS
    counts = jnp.sum(cnt, axis=0)
    padded = ((counts + EXPERT_BLOCK - 1) // EXPERT_BLOCK) * EXPERT_BLOCK
    end_pad = jnp.cumsum(padded)
    start_pad = end_pad - padded
    hot = eidx[:, :, None] == jnp.arange(N_EXPERTS, dtype=I32)[None, None, :]
    dest = jnp.sum(jnp.where(hot, start_pad[None, None, :], 0), axis=-1) + rank

    mb_start = jnp.arange(n_mb, dtype=I32) * EXPERT_BLOCK
    mb_expert = jnp.minimum(_count_le(end_pad, mb_start), N_EXPERTS - 1).astype(I32)
    mb_active = (mb_start < end_pad[-1]).astype(I32)

    tok = jnp.broadcast_to(jnp.arange(t, dtype=I32)[None, :], dest.shape)
    slot_tok = jnp.zeros((n_mb * EXPERT_BLOCK,), I32).at[dest.reshape(-1)].set(
        tok.reshape(-1), unique_indices=True, mode="promise_in_bounds")
    return dest.astype(I32), slot_tok, mb_expert, mb_active, n_mb


def _sc_gather(table, idx):
    info = plsc.get_sparse_core_info()
    nc, ns = info.num_cores, info.num_subcores
    n_rows, width = idx.shape[0], table.shape[1]
    per_worker = n_rows // (nc * ns)
    steps = per_worker // SC_GATHER_ROWS
    assert per_worker * nc * ns == n_rows and steps * SC_GATHER_ROWS == per_worker
    mesh = plsc.VectorSubcoreMesh(core_axis_name="c", subcore_axis_name="s")

    @functools.partial(
        pl.kernel, mesh=mesh,
        out_type=jax.ShapeDtypeStruct((n_rows, width), table.dtype),
        scratch_types=[pltpu.VMEM((SC_GATHER_ROWS,), I32),
                       pltpu.VMEM((SC_GATHER_ROWS, width), table.dtype),
                       pltpu.SemaphoreType.DMA],
    )
    def gather(table_hbm, idx_hbm, out_hbm, idx_v, rows_v, sem):
        base = (lax.axis_index("s") * nc + lax.axis_index("c")) * per_worker

        @pl.loop(0, steps)
        def _(i):
            off = pl.multiple_of(base + i * SC_GATHER_ROWS, 8)
            pltpu.sync_copy(idx_hbm.at[pl.ds(off, SC_GATHER_ROWS)], idx_v)
            pltpu.async_copy(table_hbm.at[idx_v], rows_v, sem).wait()
            pltpu.sync_copy(rows_v, out_hbm.at[pl.ds(off, SC_GATHER_ROWS)])

    return gather(table, idx)


def _expert_kernel(e_ref, act_ref, x_ref, wgu_ref, wd_ref, bg_ref, bl_ref, bd_ref, o_ref,
                   wg_c, wl_c, wd_c):
    mb = pl.program_id(0)
    new_expert = jnp.logical_or(mb == 0, e_ref[mb] != e_ref[jnp.maximum(mb - 1, 0)])

    @pl.when(jnp.logical_and(new_expert, act_ref[mb] == 1))
    def _():
        lane = _iota((D_MODEL, 128), 1)
        half = lane < 64
        idx = jnp.where(half, 2 * lane, 2 * (lane - 64) + 1)
        for m in range(D_MODEL // 128):
            a = jnp.take_along_axis(wgu_ref[0, :, (2 * m) * 128:(2 * m + 1) * 128], idx, axis=1)
            b = jnp.take_along_axis(wgu_ref[0, :, (2 * m + 1) * 128:(2 * m + 2) * 128], idx, axis=1)
            cols = slice(m * 128, (m + 1) * 128)
            wg_c[:, cols] = jnp.where(half, a, pltpu.roll(b, 64, 1)).astype(BF16)
            wl_c[:, cols] = jnp.where(half, pltpu.roll(a, 64, 1), b).astype(BF16)
        wd_c[...] = wd_ref[0].astype(BF16)

    @pl.when(act_ref[mb] == 0)
    def _():
        o_ref[...] = jnp.zeros_like(o_ref)

    @pl.when(act_ref[mb] == 1)
    def _():
        x = _unpack_bf16_pairs(x_ref[...]).astype(BF16)
        glu = jnp.dot(x, wg_c[...], preferred_element_type=F32) + bg_ref[0]
        lin = jnp.dot(x, wl_c[...], preferred_element_type=F32) + bl_ref[0]
        glu = jnp.minimum(glu, SWIGLU_LIMIT)
        lin = jnp.clip(lin, -SWIGLU_LIMIT, SWIGLU_LIMIT)
        act = glu * jax.nn.sigmoid(SWIGLU_ALPHA * glu) * (lin + 1.0)
        o_ref[...] = _pack_bf16_pairs(_mm(act, wd_c[...]) + bd_ref[0])


def _experts(xb, mb_expert, mb_active, w_gu, w_down, bg, bl, bd, n_mb):
    d_ff = w_down.shape[1]
    assert d_ff == D_MODEL and w_gu.shape[1:] == (D_MODEL, 2 * d_ff)
    bspec = pl.BlockSpec((1, 1, D_MODEL), lambda m, e, a: (e[m], 0, 0))
    xspec = pl.BlockSpec((EXPERT_BLOCK, D_MODEL // 2), lambda m, e, a: (m, 0))
    wcache = pltpu.VMEM((D_MODEL, D_MODEL), BF16)
    return pl.pallas_call(
        _expert_kernel,
        grid_spec=pltpu.PrefetchScalarGridSpec(
            num_scalar_prefetch=2,
            grid=(n_mb,),
            in_specs=[xspec,
                      pl.BlockSpec((1, D_MODEL, 2 * d_ff), lambda m, e, a: (e[m], 0, 0)),
                      pl.BlockSpec((1, d_ff, D_MODEL), lambda m, e, a: (e[m], 0, 0)),
                      bspec, bspec, bspec],
            out_specs=xspec,
            scratch_shapes=[wcache, wcache, wcache]),
        out_shape=jax.ShapeDtypeStruct(xb.shape, U32),
        compiler_params=pltpu.CompilerParams(dimension_semantics=("arbitrary",),
                                             vmem_limit_bytes=VMEM_LIMIT),
        name="experts",
    )(mb_expert, mb_active, xb, w_gu, w_down, bg, bl, bd)


def _combine_kernel(y4_ref, gate_ref, x1_ref, nw_ref, o_ref):
    g = gate_ref[...]
    y = x1_ref[...]
    for k in range(TOP_K):
        y = y + g[:, k:k + 1] * _unpack_bf16_pairs(y4_ref[k])
    o_ref[...] = y * lax.rsqrt(jnp.mean(y * y, axis=-1, keepdims=True) + EPS) * nw_ref[...]


def _combine(y4, gate_t, x1, norm_w):
    t = x1.shape[0]
    tm = min(512, t)
    return pl.pallas_call(
        _combine_kernel,
        grid=(t // tm,),
        in_specs=[pl.BlockSpec((TOP_K, tm, D_MODEL // 2), lambda i: (0, i, 0)),
                  pl.BlockSpec((tm, TOP_K), lambda i: (i, 0)),
                  pl.BlockSpec((tm, D_MODEL), lambda i: (i, 0)),
                  pl.BlockSpec((1, D_MODEL), lambda i: (0, 0))],
        out_specs=pl.BlockSpec((tm, D_MODEL), lambda i: (i, 0)),
        out_shape=jax.ShapeDtypeStruct((t, D_MODEL), F32),
        compiler_params=pltpu.CompilerParams(dimension_semantics=("parallel",)),
        name="combine",
    )(y4, gate_t, x1, norm_w.reshape(1, D_MODEL))


def _moe(x1, hn, logits_t, w_gu, b_gu, w_down, b_down, norm_final):
    t = x1.shape[0]
    tt = min(512, t)
    eidx, gate, rank, base, cnt = _route(logits_t, tt)
    cnt = cnt[:, :, 0].astype(I32)
    dest, slot_tok, mb_expert, mb_active, n_mb = _routing_plan(cnt, eidx, rank, tt)
    xb = _sc_gather(hn, slot_tok)
    bg = b_gu[:, None, 0::2]
    bl = b_gu[:, None, 1::2]
    yb = _experts(xb, mb_expert, mb_active, w_gu, w_down, bg, bl, b_down[:, None, :], n_mb)
    y4 = _sc_gather(yb, dest.reshape(-1)).reshape(TOP_K, t, D_MODEL // 2)
    return _combine(y4, gate.T, x1, norm_final)


def kernel(x, norm_mix, w_in, gdn_conv, gdn_A_log, gdn_dt_bias, gdn_norm, rwkv_mu, rwkv_w0, rwkv_w2, rwkv_a0, rwkv_a2, rwkv_g2, rwkv_k_k, rwkv_k_a, rwkv_r_k, rwkv_ln_w, rwkv_ln_b, proj_a, proj_b, w_out, norm_ffn, router_w, router_b, w_gate_up, b_gate_up, w_down, b_down, norm_final):
    bsz, seq, d = x.shape
    depth = w_in.shape[0]
    x2 = x.reshape(bsz * seq, d)
    out = None
    for l in range(depth):
        w = w_in[l]
        w_pack = jnp.concatenate([w[:, 0:4096], w[:, 4112:7184], w[:, 7440:9488], w[:, 7184:7440],
                                  w[:, 4096:4112], jnp.zeros((d, PACK_COLS - 9488), w.dtype)],
                                 axis=1).astype(BF16)
        p = _in_proj(x2, norm_mix[l], w_pack)
        ya = _gdn(p, gdn_conv[l], gdn_A_log[l], gdn_dt_bias[l], gdn_norm[l], bsz, seq)
        yb = _rwkv(p, rwkv_mu[l], rwkv_w0[l], rwkv_w2[l], rwkv_a0[l], rwkv_a2[l], rwkv_g2[l],
                   rwkv_k_k[l], rwkv_k_a[l], rwkv_r_k[l], rwkv_ln_w[l], rwkv_ln_b[l], bsz, seq)
        x1, hn, logits_t = _merge(x2, ya, yb, p, proj_a[l], proj_b[l], w_out[l], norm_ffn[l],
                                  router_w[l], router_b[l])
        assert l == depth - 1, "only the final layer's residual is fused with the output norm"
        out = _moe(x1, hn, logits_t, w_gate_up[l], b_gate_up[l], w_down[l], b_down[l], norm_final)
    return out.reshape(bsz, seq, d)
```

```python
import functools

import jax
import jax.numpy as jnp
import numpy as np
from jax import lax
from jax.experimental import pallas as pl
from jax.experimental.pallas import tpu as pltpu
from jax.experimental.pallas import tpu_sc as plsc

F32 = jnp.float32
BF16 = jnp.bfloat16
I32 = jnp.int32
U32 = jnp.uint32
HIGHEST = lax.Precision.HIGHEST

D_MODEL = 1024
EPS = 1e-6
CHUNK = 64
GDN_HEADS = 8
GDN_DK = 128
CONV_W = 4
RWKV_HEADS = 16
RWKV_N = 64
RWKV_GN_EPS = 64e-5
LORA_COLS = 256
N_EXPERTS = 32
TOP_K = 4
SWIGLU_ALPHA = 1.702
SWIGLU_LIMIT = 7.0

MAIN_COLS = 9216
AUX_COLS = 384
HEADS_PER_GROUP = 4
PACK_W = HEADS_PER_GROUP * CHUNK
GDN_CHUNKS_PER_ITER = 2

SC_GATHER_ROWS = 64
EXPERT_BLOCK = 512
VMEM_LIMIT = 48 * 1024 * 1024


def _mm(a, b):
    return jnp.dot(a.astype(BF16), b.astype(BF16), preferred_element_type=F32)


def _mm_nt(a, b):
    return lax.dot_general(a.astype(BF16), b.astype(BF16), (((1,), (1,)), ((), ())),
                           preferred_element_type=F32)


def _mm_tn(a, b):
    return lax.dot_general(a.astype(BF16), b.astype(BF16), (((0,), (0,)), ((), ())),
                           preferred_element_type=F32)


def _split_bf16(x, terms):
    parts = []
    for _ in range(terms - 1):
        hi = x.astype(BF16)
        parts.append(hi)
        x = x - hi.astype(F32)
    parts.append(x.astype(BF16))
    return parts


def _mm_exact_lhs(a_bf16, b):
    out = None
    for part in _split_bf16(b, 3):
        d = jnp.dot(a_bf16, part, preferred_element_type=F32)
        out = d if out is None else out + d
    return out


def _pack_bf16_pairs(x):
    half = x.shape[1] // 2
    lo = lax.bitcast_convert_type(x[:, :half].astype(BF16).astype(F32), U32) >> 16
    hi = lax.bitcast_convert_type(x[:, half:].astype(BF16).astype(F32), U32) & jnp.uint32(0xFFFF0000)
    return lo | hi


def _unpack_bf16_pairs(w):
    lo = lax.bitcast_convert_type(w << 16, F32)
    hi = lax.bitcast_convert_type(w & jnp.uint32(0xFFFF0000), F32)
    return jnp.concatenate([lo, hi], axis=1)


def _iota(shape, dim):
    return lax.broadcasted_iota(I32, shape, dim)


def _softplus(x):
    return jnp.maximum(x, 0.0) + jnp.log(1.0 + jnp.exp(-jnp.abs(x)))


def _stack_rows(x, n):
    return jnp.concatenate([x] * n, axis=0)


def _tri_inv_packed(ms, eye, bdmask):
    def bd(x):
        return jnp.where(bdmask, _stack_rows(x, HEADS_PER_GROUP), jnp.zeros((), BF16))

    def mul(a, b):
        ah, al = _split_bf16(a, 2)
        bh, bl = _split_bf16(b, 2)
        bdh = bd(bh)
        return (jnp.dot(ah, bdh, preferred_element_type=F32) + jnp.dot(al, bdh, preferred_element_type=F32)
                + jnp.dot(ah, bd(bl), preferred_element_type=F32))

    ts = [eye - m for m in ms]
    xs = [mul(m, m) for m in ms]
    for _ in range(4):
        rs = [mul(jnp.concatenate([t, x], axis=0), x) for t, x in zip(ts, xs)]
        ts = [t + r[:CHUNK] for t, r in zip(ts, rs)]
        xs = [r[CHUNK:] for r in rs]
    return [t + mul(t, x) for t, x in zip(ts, xs)]


def _shift_rows(x, prev8, k, row8):
    r = pltpu.roll(x, k, 0)
    pr = pltpu.roll(prev8, k, 0)
    head = jnp.where(row8 < k, pr, r[:8])
    return jnp.concatenate([head, r[8:]], axis=0)


def _in_proj_kernel(x_ref, nw_ref, w_ref, wa_ref, o_ref, oa_ref, h_ref):
    @pl.when(pl.program_id(1) == 0)
    def _():
        x = x_ref[...]
        y = x * lax.rsqrt(jnp.mean(x * x, axis=-1, keepdims=True) + EPS)
        h_ref[...] = (y * nw_ref[...]).astype(BF16)
        oa_ref[...] = jnp.dot(h_ref[...], wa_ref[...], preferred_element_type=F32)

    o_ref[...] = jnp.dot(h_ref[...], w_ref[...], preferred_element_type=F32).astype(BF16)


def _in_proj(x2, norm_w, w_main, w_aux):
    t = x2.shape[0]
    tm = min(1024, t)
    tn = 2304
    return pl.pallas_call(
        _in_proj_kernel,
        grid=(t // tm, MAIN_COLS // tn),
        in_specs=[pl.BlockSpec((tm, D_MODEL), lambda i, j: (i, 0)),
                  pl.BlockSpec((1, D_MODEL), lambda i, j: (0, 0)),
                  pl.BlockSpec((D_MODEL, tn), lambda i, j: (0, j)),
                  pl.BlockSpec((D_MODEL, AUX_COLS), lambda i, j: (0, 0))],
        out_specs=[pl.BlockSpec((tm, tn), lambda i, j: (i, j)),
                   pl.BlockSpec((tm, AUX_COLS), lambda i, j: (i, 0))],
        out_shape=[jax.ShapeDtypeStruct((t, MAIN_COLS), BF16), jax.ShapeDtypeStruct((t, AUX_COLS), F32)],
        scratch_shapes=[pltpu.VMEM((tm, D_MODEL), BF16)],
        compiler_params=pltpu.CompilerParams(dimension_semantics=("parallel", "arbitrary"),
                                             vmem_limit_bytes=VMEM_LIMIT),
        name="in_proj",
    )(x2, norm_w.reshape(1, D_MODEL), w_main, w_aux)


def _gdn_kernel(q_ref, k_ref, v_ref, z_ref, ba_ref, conv_ref, alog_ref, dtb_ref, nw_ref, o_ref,
                carry_ref, qn_ref, kn_ref, vv_ref, beta_ref, g_ref, s_ref, *, ts):
    @pl.when(pl.program_id(1) == 0)
    def _():
        carry_ref[...] = jnp.zeros_like(carry_ref)
        s_ref[...] = jnp.zeros_like(s_ref)

    row8 = _iota((8, D_MODEL), 0)

    def conv_silu(idx, x_ref):
        x = x_ref[...].astype(F32)
        prev8 = carry_ref[idx]
        w4 = conv_ref[:, idx * D_MODEL:(idx + 1) * D_MODEL]
        y = x * w4[CONV_W - 1:CONV_W]
        for k in range(1, CONV_W):
            y = y + _shift_rows(x, prev8, k, row8) * w4[CONV_W - 1 - k:CONV_W - k]
        carry_ref[idx] = x[ts - 8:ts]
        return y * jax.nn.sigmoid(y)

    def l2norm_heads(x, scale):
        parts = []
        for h in range(GDN_HEADS):
            xh = x[:, h * GDN_DK:(h + 1) * GDN_DK]
            parts.append(xh * (lax.rsqrt(jnp.sum(xh * xh, axis=-1, keepdims=True) + EPS) * scale))
        return jnp.concatenate(parts, axis=1)

    qn_ref[...] = l2norm_heads(conv_silu(0, q_ref), GDN_DK ** -0.5)
    kn_ref[...] = l2norm_heads(conv_silu(1, k_ref), 1.0)
    vv_ref[...] = conv_silu(2, v_ref)
    ba = ba_ref[...]
    beta_ref[...] = jax.nn.sigmoid(ba)
    g_ref[...] = -jnp.exp(alog_ref[...]) * _softplus(ba + dtb_ref[...])

    gw = HEADS_PER_GROUP * GDN_DK
    lane_p = _iota((CHUNK, PACK_W), 1)
    row_p = _iota((CHUNK, PACK_W), 0)
    eye = jnp.where((lane_p % CHUNK) == row_p, 1.0, 0.0).astype(F32)
    strict = (lane_p % CHUNK) < row_p
    incl = (lane_p % CHUNK) <= row_p
    bd_pp = (_iota((PACK_W, PACK_W), 0) // CHUNK) == (_iota((PACK_W, PACK_W), 1) // CHUNK)
    bd_pn = (_iota((PACK_W, gw), 0) // CHUNK) == (_iota((PACK_W, gw), 1) // GDN_DK)
    bd_pn2 = (_iota((PACK_W, 2 * gw), 0) // CHUNK) == ((_iota((PACK_W, 2 * gw), 1) % gw) // GDN_DK)
    ltri = jnp.where(_iota((CHUNK, CHUNK), 1) <= _iota((CHUNK, CHUNK), 0), 1.0, 0.0).astype(BF16)
    ones_cc = jnp.ones((CHUNK, CHUNK), BF16)
    lane128 = _iota((CHUNK, 128), 1)
    nw = nw_ref[...]

    n_groups = GDN_HEADS // HEADS_PER_GROUP
    hcols = [slice(j * GDN_DK, (j + 1) * GDN_DK) for j in range(HEADS_PER_GROUP)]

    def intra_chunk(rows_list):
        units = [(i, gi) for i in range(len(rows_list)) for gi in range(n_groups)]
        gcs = [_mm_exact_lhs(ltri, g_ref[rows, :]) for rows in rows_list]
        beta = [beta_ref[rows, :] for rows in rows_list]
        pre = []
        for i, gi in units:
            rows = rows_list[i]
            heads = [gi * HEADS_PER_GROUP + j for j in range(HEADS_PER_GROUP)]
            cols = slice(gi * gw, (gi + 1) * gw)
            beta_nat = jnp.concatenate(
                [jnp.broadcast_to(beta[i][:, h:h + 1], (CHUNK, GDN_DK)) for h in heads], axis=1)
            gcol = [jnp.broadcast_to(gcs[i][:, 8 + h:9 + h], (CHUNK, GDN_DK)) for h in heads]
            gcol_p = jnp.concatenate([jnp.where(lane128 < CHUNK, gcol[0], gcol[1]),
                                      jnp.where(lane128 < CHUNK, gcol[2], gcol[3])], axis=1)
            pre.append(dict(qn=qn_ref[rows, cols], kn=kn_ref[rows, cols], vv=vv_ref[rows, cols],
                            beta_nat=beta_nat, gc_nat=jnp.concatenate(gcol, axis=1), gcol_p=gcol_p))
        grow = [_mm_exact_lhs(ones_cc, u["gcol_p"] * eye) for u in pre]
        aas = [_mm_nt(jnp.concatenate([u["kn"] * u["beta_nat"], u["qn"]], axis=0),
                      jnp.where(bd_pn, _stack_rows(u["kn"], HEADS_PER_GROUP), 0.0)) for u in pre]
        decay = [jnp.exp(jnp.minimum(u["gcol_p"] - gr, 0.0)) for u, gr in zip(pre, grow)]
        t_inv = _tri_inv_packed([jnp.where(strict, aa[:CHUNK] * d, 0.0) for aa, d in zip(aas, decay)],
                                eye, bd_pp)
        out = []
        for u, aa, d, ti in zip(pre, aas, decay, t_inv):
            egc = jnp.exp(u["gc_nat"])
            data = jnp.concatenate([u["vv"] * u["beta_nat"], u["kn"] * u["beta_nat"] * egc], axis=1)
            uw = _mm(ti, jnp.where(bd_pn2, _stack_rows(data, HEADS_PER_GROUP), 0.0))
            gl = u["gc_nat"][CHUNK - 1:CHUNK, :]
            out.append(dict(uw=uw, attn=jnp.where(incl, aa[CHUNK:] * d, 0.0), qd=u["qn"] * egc,
                            kd=u["kn"] * jnp.exp(gl - u["gc_nat"]), egl=jnp.exp(gl)))
        return out

    def recurrence(rows, group_units):
        wq = {}
        for gi, u in enumerate(group_units):
            for j, hc in enumerate(hcols):
                h = gi * HEADS_PER_GROUP + j
                wq[h] = _mm(jnp.concatenate([u["uw"][:, gw + j * GDN_DK:gw + (j + 1) * GDN_DK],
                                             u["qd"][:, hc]], axis=0), s_ref[h])
        vnew = {}
        for gi, u in enumerate(group_units):
            for j, hc in enumerate(hcols):
                h = gi * HEADS_PER_GROUP + j
                vnew[h] = u["uw"][:, hc] - wq[h][:CHUNK]
                s_ref[h] = s_ref[h] * u["egl"][:, hc] + _mm_tn(u["kd"][:, hc], vnew[h])
        for gi, u in enumerate(group_units):
            heads = [gi * HEADS_PER_GROUP + j for j in range(HEADS_PER_GROUP)]
            vn = jnp.concatenate([vnew[h] for h in heads], axis=1)
            o = jnp.concatenate([wq[h][CHUNK:] for h in heads], axis=1) + _mm(
                u["attn"], jnp.where(bd_pn, _stack_rows(vn, HEADS_PER_GROUP), 0.0))
            for j, h in enumerate(heads):
                oh = o[:, hcols[j]]
                oh = oh * lax.rsqrt(jnp.mean(oh * oh, axis=-1, keepdims=True) + EPS) * nw
                zh = z_ref[rows, h * GDN_DK:(h + 1) * GDN_DK].astype(F32)
                o_ref[rows, h * GDN_DK:(h + 1) * GDN_DK] = (oh * (zh * jax.nn.sigmoid(zh))).astype(BF16)

    def chunk_pair_body(c, carry):
        rows_list = [pl.ds(pl.multiple_of((c * GDN_CHUNKS_PER_ITER + i) * CHUNK, CHUNK), CHUNK)
                     for i in range(GDN_CHUNKS_PER_ITER)]
        units = intra_chunk(rows_list)
        for i, rows in enumerate(rows_list):
            recurrence(rows, units[i * n_groups:(i + 1) * n_groups])
        return carry

    lax.fori_loop(0, ts // (CHUNK * GDN_CHUNKS_PER_ITER), chunk_pair_body, 0)


def _gdn(p, p_aux, conv_w, a_log, dt_bias, norm_w, bsz, seq):
    ts = min(256, seq)
    ns = seq // ts
    pad8 = lambda v: jnp.zeros((1, 128), F32).at[0, 8:16].set(v)
    tok = lambda col: pl.BlockSpec((ts, D_MODEL), lambda b, s, col=col: (b * ns + s, col))
    const = lambda shape: pl.BlockSpec(shape, lambda b, s: (0,) * len(shape))
    return pl.pallas_call(
        functools.partial(_gdn_kernel, ts=ts),
        grid=(bsz, ns),
        in_specs=[tok(0), tok(1), tok(2), tok(3),
                  pl.BlockSpec((ts, 128), lambda b, s: (b * ns + s, LORA_COLS // 128)),
                  const((CONV_W, 3 * D_MODEL)), const((1, 128)), const((1, 128)), const((1, GDN_DK))],
        out_specs=pl.BlockSpec((ts, D_MODEL), lambda b, s: (b * ns + s, 0)),
        out_shape=jax.ShapeDtypeStruct((bsz * seq, D_MODEL), BF16),
        scratch_shapes=[pltpu.VMEM((3, 8, D_MODEL), F32),
                        pltpu.VMEM((ts, D_MODEL), F32), pltpu.VMEM((ts, D_MODEL), F32),
                        pltpu.VMEM((ts, D_MODEL), F32),
                        pltpu.VMEM((ts, 128), F32), pltpu.VMEM((ts, 128), F32),
                        pltpu.VMEM((GDN_HEADS, GDN_DK, GDN_DK), F32)],
        compiler_params=pltpu.CompilerParams(dimension_semantics=("parallel", "arbitrary"),
                                             vmem_limit_bytes=VMEM_LIMIT),
        name="gdn",
    )(p, p, p, p, p_aux, conv_w, pad8(a_log), pad8(dt_bias), norm_w.reshape(1, GDN_DK))


def _rwkv_kernel(r_ref, k_ref, v_ref, lo_ref, mu_ref, w0_ref, w2_ref, a0_ref, a2_ref, g2_ref,
                 kk_ref, ka_ref, rk_ref, lnw_ref, lnb_ref, seg_ref, o_ref,
                 carry_ref, rs_ref, ks_ref, vs_ref, kks_ref, bs_ref, ld_ref, gg_ref, os_ref, st_ref,
                 *, ts):
    @pl.when(pl.program_id(1) == 0)
    def _():
        carry_ref[...] = jnp.zeros_like(carry_ref)
        st_ref[...] = jnp.zeros_like(st_ref)

    width = D_MODEL
    row8 = _iota((8, width), 0)
    row8l = _iota((8, LORA_COLS), 0)

    def seg_sum(x):
        hi = x.astype(BF16)
        lo = (x - hi.astype(F32)).astype(BF16)
        parts = []
        for g in range(width // PACK_W):
            cs = slice(g * PACK_W, (g + 1) * PACK_W)
            parts.append(jnp.dot(hi[:, cs], seg_ref[...], preferred_element_type=F32)
                         + jnp.dot(lo[:, cs], seg_ref[...], preferred_element_type=F32))
        return jnp.concatenate(parts, axis=1)

    def lerp(idx, x_ref, mu, r8):
        x = x_ref[...].astype(F32)
        xs = x + (_shift_rows(x, carry_ref[idx, :, :x.shape[1]], 1, r8) - x) * mu
        carry_ref[idx, :, :x.shape[1]] = x[ts - 8:ts]
        return xs

    r = lerp(0, r_ref, mu_ref[:, 0:width], row8)
    k = lerp(1, k_ref, mu_ref[:, width:2 * width], row8)
    v = lerp(2, v_ref, mu_ref[:, 2 * width:3 * width], row8)
    lo = lerp(3, lo_ref, mu_ref[:, 3 * width:3 * width + LORA_COLS], row8l)
    lo_a = lo[:, :128]
    w_log = -_softplus(-(w0_ref[...] + _mm(jnp.tanh(lo_a), w2_ref[...]))) - 0.5
    ld_ref[...] = -jnp.exp(w_log)
    aa = jax.nn.sigmoid(a0_ref[...] + _mm(lo_a, a2_ref[...]))
    gg_ref[...] = _mm(jax.nn.sigmoid(lo[:, 128:]), g2_ref[...])
    kx = k * kk_ref[...]
    kk = kx * lax.rsqrt(seg_sum(kx * kx) + EPS)
    k = k * (1.0 + (aa - 1.0) * ka_ref[...])
    rs_ref[...] = r
    ks_ref[...] = k
    vs_ref[...] = v
    kks_ref[...] = kk
    bs_ref[...] = kk * aa

    lane_p = _iota((CHUNK, PACK_W), 1)
    row_p = _iota((CHUNK, PACK_W), 0)
    eye = jnp.where((lane_p % CHUNK) == row_p, 1.0, 0.0).astype(F32)
    strict = (lane_p % CHUNK) < row_p
    incl = (lane_p % CHUNK) <= row_p
    bd = (_iota((PACK_W, PACK_W), 0) // CHUNK) == (_iota((PACK_W, PACK_W), 1) // CHUNK)
    bd2 = jnp.concatenate([bd, bd], axis=1)
    ltri = jnp.where(_iota((CHUNK, CHUNK), 1) <= _iota((CHUNK, CHUNK), 0), 1.0, 0.0).astype(BF16)

    def bdiag(x):
        return jnp.where(bd, _stack_rows(x, HEADS_PER_GROUP), 0.0)

    n_groups = width // PACK_W
    gcols = [slice(g * PACK_W, (g + 1) * PACK_W) for g in range(n_groups)]

    def chunk_body(c, carry):
        rows = pl.ds(pl.multiple_of(c * CHUNK, CHUNK), CHUNK)
        ld = [ld_ref[rows, cols] for cols in gcols]
        cs = [_mm_exact_lhs(ltri, x) for x in ld]
        pre = []
        for g, cols in enumerate(gcols):
            cl = cs[g][CHUNK - 1:CHUNK, :]
            e_neg = jnp.exp(-cs[g])
            e_dec = jnp.exp(cl - cs[g])
            kc, bc = ks_ref[rows, cols], bs_ref[rows, cols]
            rt = rs_ref[rows, cols] * jnp.exp(cs[g])
            kt = kks_ref[rows, cols] * jnp.exp(cs[g] - ld[g])
            pre.append(dict(cl=cl, rt=rt, kt=kt, vc=vs_ref[rows, cols], kh=kc * e_neg, bh=bc * e_neg,
                            kdec=kc * e_dec, bdec=bc * e_dec))
        akb = [_mm_nt(jnp.concatenate([u["kt"], u["rt"]], axis=0),
                      jnp.concatenate([bdiag(u["kh"]), bdiag(u["bh"])], axis=0)) for u in pre]
        t_inv = _tri_inv_packed([jnp.where(strict, a[:CHUNK, PACK_W:], 0.0) for a in akb], eye, bd)
        xo = [_mm(jnp.concatenate([jnp.where(strict, a[:CHUNK, :PACK_W], 0.0),
                                   jnp.where(incl, a[CHUNK:, :PACK_W], 0.0)], axis=0), bdiag(u["vc"]))
              for a, u in zip(akb, pre)]
        tt = [_mm(ti, jnp.where(bd2, _stack_rows(jnp.concatenate([x[:CHUNK], u["kt"]], axis=1),
                                                 HEADS_PER_GROUP), 0.0))
              for ti, x, u in zip(t_inv, xo, pre)]
        rr = [_mm(jnp.where(incl, a[CHUNK:, PACK_W:], 0.0),
                  jnp.where(bd2, _stack_rows(jnp.concatenate([t[:, PACK_W:], t[:, :PACK_W]], axis=1),
                                             HEADS_PER_GROUP), 0.0))
              for a, t in zip(akb, tt)]
        uo = [_mm_nt(jnp.concatenate([t[:, PACK_W:], u["rt"] - r[:, :PACK_W]], axis=0), st_ref[g])
              for g, (t, r, u) in enumerate(zip(tt, rr, pre))]
        upd = []
        for g, cols in enumerate(gcols):
            os_ref[rows, cols] = xo[g][CHUNK:] - rr[g][:, PACK_W:] + uo[g][CHUNK:]
            u_full = tt[g][:, :PACK_W] + uo[g][:CHUNK]
            upd.append(_mm_tn(jnp.concatenate([pre[g]["vc"], u_full], axis=0),
                              jnp.concatenate([pre[g]["kdec"], -pre[g]["bdec"]], axis=0)))
        for g in range(n_groups):
            st_ref[g] = st_ref[g] * jnp.exp(pre[g]["cl"]) + jnp.where(bd, upd[g], 0.0)
        return carry

    lax.fori_loop(0, ts // CHUNK, chunk_body, 0)

    o = os_ref[...]
    inv_n = 1.0 / RWKV_N
    mean = seg_sum(o) * inv_n
    cen = o - mean
    var = seg_sum(cen * cen) * inv_n
    o = cen * lax.rsqrt(var + RWKV_GN_EPS) * lnw_ref[...] + lnb_ref[...]
    bonus = seg_sum(rs_ref[...] * ks_ref[...] * rk_ref[...]) * vs_ref[...]
    o_ref[...] = ((o + bonus) * gg_ref[...]).astype(BF16)


def _rwkv(p, p_aux, mu, w0, w2, a0, a2, g2, k_k, k_a, r_k, ln_w, ln_b, bsz, seq):
    ts = min(256, seq)
    ns = seq // ts
    width = D_MODEL
    row = lambda v: v.reshape(1, -1)
    w2p = jnp.concatenate([w2, jnp.zeros_like(w2)], axis=0).astype(BF16)
    a2p = jnp.concatenate([jnp.zeros_like(a2), a2], axis=0).astype(BF16)
    seg = (np.arange(PACK_W)[:, None] // RWKV_N == np.arange(PACK_W)[None, :] // RWKV_N)
    seg = jnp.asarray(seg, BF16)
    tok = lambda col: pl.BlockSpec((ts, width), lambda b, s, col=col: (b * ns + s, col))
    const = lambda shape: pl.BlockSpec(shape, lambda b, s: (0,) * len(shape))
    fbuf = lambda: pltpu.VMEM((ts, width), F32)
    return pl.pallas_call(
        functools.partial(_rwkv_kernel, ts=ts),
        grid=(bsz, ns),
        in_specs=[tok(4), tok(5), tok(6),
                  pl.BlockSpec((ts, LORA_COLS), lambda b, s: (b * ns + s, 0)),
                  const((1, 3 * width + LORA_COLS)), const((1, width)), const((128, width)),
                  const((1, width)), const((128, width)), const((128, width)),
                  const((1, width)), const((1, width)), const((1, width)), const((1, width)),
                  const((1, width)), const((PACK_W, PACK_W))],
        out_specs=pl.BlockSpec((ts, width), lambda b, s: (b * ns + s, 0)),
        out_shape=jax.ShapeDtypeStruct((bsz * seq, width), BF16),
        scratch_shapes=[pltpu.VMEM((4, 8, width), F32),
                        fbuf(), fbuf(), fbuf(), fbuf(), fbuf(), fbuf(), fbuf(), fbuf(),
                        pltpu.VMEM((width // PACK_W, PACK_W, PACK_W), F32)],
        compiler_params=pltpu.CompilerParams(dimension_semantics=("parallel", "arbitrary"),
                                             vmem_limit_bytes=VMEM_LIMIT),
        name="rwkv",
    )(p, p, p, p_aux, row(mu), row(w0), w2p, row(a0), a2p, g2.astype(BF16), row(k_k), row(k_a),
      row(r_k), row(ln_w), row(ln_b), seg)


def _merge_kernel(x_ref, ya_ref, yb_ref, ga_ref, gb_ref, pa_ref, pb_ref, wo_ref, nw_ref, rw_ref, rb_ref,
                  x1_ref, hn_ref, lg_ref):
    merged = (jax.nn.sigmoid(ga_ref[...].astype(F32)) * jnp.dot(ya_ref[...], pa_ref[...], preferred_element_type=F32)
              + jax.nn.sigmoid(gb_ref[...].astype(F32)) * jnp.dot(yb_ref[...], pb_ref[...], preferred_element_type=F32))
    x1 = x_ref[...] + _mm(merged, wo_ref[...])
    x1_ref[...] = x1
    hn = x1 * lax.rsqrt(jnp.mean(x1 * x1, axis=-1, keepdims=True) + EPS) * nw_ref[...]
    hn_ref[...] = _pack_bf16_pairs(hn)
    lg_ref[...] = lax.dot_general(rw_ref[...], hn, (((1,), (1,)), ((), ())),
                                  preferred_element_type=F32, precision=HIGHEST) + rb_ref[...]


def _merge(x2, ya, yb, p, proj_a, proj_b, w_out, norm_w, router_w, router_b):
    t = x2.shape[0]
    tm = min(512, t)
    tok = lambda col: pl.BlockSpec((tm, D_MODEL), lambda i, col=col: (i, col))
    const = lambda shape: pl.BlockSpec(shape, lambda i: (0,) * len(shape))
    return pl.pallas_call(
        _merge_kernel,
        grid=(t // tm,),
        in_specs=[tok(0), tok(0), tok(0), tok(7), tok(8),
                  const((D_MODEL, D_MODEL)), const((D_MODEL, D_MODEL)), const((D_MODEL, D_MODEL)),
                  const((1, D_MODEL)), const((N_EXPERTS, D_MODEL)), const((N_EXPERTS, 1))],
        out_specs=[tok(0), pl.BlockSpec((tm, D_MODEL // 2), lambda i: (i, 0)),
                   pl.BlockSpec((N_EXPERTS, tm), lambda i: (0, i))],
        out_shape=[jax.ShapeDtypeStruct((t, D_MODEL), F32), jax.ShapeDtypeStruct((t, D_MODEL // 2), U32),
                   jax.ShapeDtypeStruct((N_EXPERTS, t), F32)],
        compiler_params=pltpu.CompilerParams(dimension_semantics=("parallel",),
                                             vmem_limit_bytes=VMEM_LIMIT),
        name="merge",
    )(x2, ya, yb, p, p, proj_a.astype(BF16), proj_b.astype(BF16), w_out.astype(BF16),
      norm_w.reshape(1, D_MODEL), router_w.T, router_b.reshape(N_EXPERTS, 1))


def _route_kernel(lg_ref, eidx_ref, gate_ref, rank_ref, base_ref, cnt_ref, carry_ref, *, tt):
    @pl.when(pl.program_id(0) == 0)
    def _():
        carry_ref[...] = jnp.zeros_like(carry_ref)

    l = lg_ref[...]
    ie = _iota((N_EXPERTS, tt), 0)
    vals, hots, idxs = [], [], []
    for _ in range(TOP_K):
        m = jnp.max(l, axis=0, keepdims=True)
        idx = jnp.min(jnp.where(l == m, ie, N_EXPERTS), axis=0, keepdims=True)
        hot = ie == idx
        vals.append(m)
        hots.append(hot)
        idxs.append(idx)
        l = jnp.where(hot, -jnp.inf, l)
    exps = [jnp.exp(v - vals[0]) for v in vals]
    den = exps[0] + exps[1] + exps[2] + exps[3]
    gate_ref[...] = jnp.concatenate([e / den for e in exps], axis=0)
    eidx_ref[...] = jnp.concatenate(idxs, axis=0)

    sel = jnp.zeros((N_EXPERTS, tt), F32)
    for hot in hots:
        sel = sel + jnp.where(hot, 1.0, 0.0)
    before = jnp.where(_iota((tt, tt), 0) < _iota((tt, tt), 1), 1.0, 0.0).astype(BF16)
    carry = carry_ref[...]
    prefix = jnp.dot(sel.astype(BF16), before, preferred_element_type=F32) + carry[:, 0:1]
    rank_ref[...] = jnp.concatenate(
        [jnp.sum(jnp.where(hot, prefix, 0.0), axis=0, keepdims=True) for hot in hots], axis=0).astype(I32)
    cnt = jnp.broadcast_to(jnp.sum(sel, axis=1, keepdims=True), (N_EXPERTS, 128))
    base_ref[0] = carry
    cnt_ref[0] = cnt
    carry_ref[...] = carry + cnt


def _route(logits_t, tt):
    t = logits_t.shape[1]
    nt = t // tt
    row4 = pl.BlockSpec((TOP_K, tt), lambda i: (0, i))
    per_tile = pl.BlockSpec((1, N_EXPERTS, 128), lambda i: (i, 0, 0))
    return pl.pallas_call(
        functools.partial(_route_kernel, tt=tt),
        grid=(nt,),
        in_specs=[pl.BlockSpec((N_EXPERTS, tt), lambda i: (0, i))],
        out_specs=[row4, row4, row4, per_tile, per_tile],
        out_shape=[jax.ShapeDtypeStruct((TOP_K, t), I32), jax.ShapeDtypeStruct((TOP_K, t), F32),
                   jax.ShapeDtypeStruct((TOP_K, t), I32),
                   jax.ShapeDtypeStruct((nt, N_EXPERTS, 128), F32),
                   jax.ShapeDtypeStruct((nt, N_EXPERTS, 128), F32)],
        scratch_shapes=[pltpu.VMEM((N_EXPERTS, 128), F32)],
        compiler_params=pltpu.CompilerParams(dimension_semantics=("arbitrary",)),
        name="route",
    )(logits_t)


def _count_le(sorted_vals, queries):
    return jnp.sum((sorted_vals[None, :] <= queries[:, None]).astype(I32), axis=1)


def _routing_plan(cnt, eidx, rank, tt):
    t = cnt.shape[0] * tt
    n_mb = (t * TOP_K) // EXPERT_BLOCK + N_EXPERTS
    counts = jnp.sum(cnt, axis=0)
    padded = ((counts + EXPERT_BLOCK - 1) // EXPERT_BLOCK) * EXPERT_BLOCK
    end_pad = jnp.cumsum(padded)
    start_pad = end_pad - padded
    hot = eidx[:, :, None] == jnp.arange(N_EXPERTS, dtype=I32)[None, None, :]
    dest = jnp.sum(jnp.where(hot, start_pad[None, None, :], 0), axis=-1) + rank

    mb_start = jnp.arange(n_mb, dtype=I32) * EXPERT_BLOCK
    mb_expert = jnp.minimum(_count_le(end_pad, mb_start), N_EXPERTS - 1).astype(I32)
    mb_active = (mb_start < end_pad[-1]).astype(I32)

    tok = jnp.broadcast_to(jnp.arange(t, dtype=I32)[None, :], dest.shape)
    slot_tok = (jnp.arange(n_mb * EXPERT_BLOCK, dtype=I32) % t).at[dest.reshape(-1)].set(
        tok.reshape(-1), unique_indices=True, mode="promise_in_bounds")
    return dest.astype(I32), slot_tok, mb_expert, mb_active, n_mb


def _sc_gather(table, idx):
    info = plsc.get_sparse_core_info()
    nc, ns = info.num_cores, info.num_subcores
    n_rows, width = idx.shape[0], table.shape[1]
    per_worker = n_rows // (nc * ns)
    steps = per_worker // SC_GATHER_ROWS
    assert per_worker * nc * ns == n_rows and steps * SC_GATHER_ROWS == per_worker
    mesh = plsc.VectorSubcoreMesh(core_axis_name="c", subcore_axis_name="s")

    @functools.partial(
        pl.kernel, mesh=mesh,
        out_type=jax.ShapeDtypeStruct((n_rows, width), table.dtype),
        scratch_types=[pltpu.VMEM((SC_GATHER_ROWS,), I32),
                       pltpu.VMEM((SC_GATHER_ROWS, width), table.dtype),
                       pltpu.SemaphoreType.DMA],
    )
    def gather(table_hbm, idx_hbm, out_hbm, idx_v, rows_v, sem):
        base = (lax.axis_index("s") * nc + lax.axis_index("c")) * per_worker

        @pl.loop(0, steps)
        def _(i):
            off = pl.multiple_of(base + i * SC_GATHER_ROWS, 8)
            pltpu.sync_copy(idx_hbm.at[pl.ds(off, SC_GATHER_ROWS)], idx_v)
            pltpu.async_copy(table_hbm.at[idx_v], rows_v, sem).wait()
            pltpu.sync_copy(rows_v, out_hbm.at[pl.ds(off, SC_GATHER_ROWS)])

    return gather(table, idx)


def _expert_kernel(e_ref, act_ref, x_ref, wgu_ref, wd_ref, bg_ref, bl_ref, bd_ref, o_ref,
                   wg_c, wl_c, wd_c):
    mb = pl.program_id(0)
    new_expert = jnp.logical_or(mb == 0, e_ref[mb] != e_ref[jnp.maximum(mb - 1, 0)])

    @pl.when(jnp.logical_and(new_expert, act_ref[mb] == 1))
    def _():
        lane = _iota((D_MODEL, 128), 1)
        half = lane < 64
        idx = jnp.where(half, 2 * lane, 2 * (lane - 64) + 1)
        for m in range(D_MODEL // 128):
            a = jnp.take_along_axis(wgu_ref[0, :, (2 * m) * 128:(2 * m + 1) * 128], idx, axis=1)
            b = jnp.take_along_axis(wgu_ref[0, :, (2 * m + 1) * 128:(2 * m + 2) * 128], idx, axis=1)
            cols = slice(m * 128, (m + 1) * 128)
            wg_c[:, cols] = jnp.where(half, a, pltpu.roll(b, 64, 1)).astype(BF16)
            wl_c[:, cols] = jnp.where(half, pltpu.roll(a, 64, 1), b).astype(BF16)
        wd_c[...] = wd_ref[0].astype(BF16)

    @pl.when(act_ref[mb] == 0)
    def _():
        o_ref[...] = jnp.zeros_like(o_ref)

    @pl.when(act_ref[mb] == 1)
    def _():
        x = _unpack_bf16_pairs(x_ref[...]).astype(BF16)
        glu = jnp.dot(x, wg_c[...], preferred_element_type=F32) + bg_ref[0]
        lin = jnp.dot(x, wl_c[...], preferred_element_type=F32) + bl_ref[0]
        glu = jnp.minimum(glu, SWIGLU_LIMIT)
        lin = jnp.clip(lin, -SWIGLU_LIMIT, SWIGLU_LIMIT)
        act = glu * jax.nn.sigmoid(SWIGLU_ALPHA * glu) * (lin + 1.0)
        o_ref[...] = _pack_bf16_pairs(_mm(act, wd_c[...]) + bd_ref[0])


def _experts(xb, mb_expert, mb_active, w_gu, w_down, bg, bl, bd, n_mb):
    d_ff = w_down.shape[1]
    assert d_ff == D_MODEL and w_gu.shape[1:] == (D_MODEL, 2 * d_ff)
    bspec = pl.BlockSpec((1, 1, D_MODEL), lambda m, e, a: (e[m], 0, 0))
    xspec = pl.BlockSpec((EXPERT_BLOCK, D_MODEL // 2), lambda m, e, a: (m, 0))
    wcache = pltpu.VMEM((D_MODEL, D_MODEL), BF16)
    return pl.pallas_call(
        _expert_kernel,
        grid_spec=pltpu.PrefetchScalarGridSpec(
            num_scalar_prefetch=2,
            grid=(n_mb,),
            in_specs=[xspec,
                      pl.BlockSpec((1, D_MODEL, 2 * d_ff), lambda m, e, a: (e[m], 0, 0)),
                      pl.BlockSpec((1, d_ff, D_MODEL), lambda m, e, a: (e[m], 0, 0)),
                      bspec, bspec, bspec],
            out_specs=xspec,
            scratch_shapes=[wcache, wcache, wcache]),
        out_shape=jax.ShapeDtypeStruct(xb.shape, U32),
        compiler_params=pltpu.CompilerParams(dimension_semantics=("arbitrary",),
                                             vmem_limit_bytes=VMEM_LIMIT),
        name="experts",
    )(mb_expert, mb_active, xb, w_gu, w_down, bg, bl, bd)


def _combine_kernel(y4_ref, gate_ref, x1_ref, nw_ref, o_ref):
    g = gate_ref[...]
    y = x1_ref[...]
    for k in range(TOP_K):
        y = y + g[:, k:k + 1] * _unpack_bf16_pairs(y4_ref[k])
    o_ref[...] = y * lax.rsqrt(jnp.mean(y * y, axis=-1, keepdims=True) + EPS) * nw_ref[...]


def _combine(y4, gate_t, x1, norm_w):
    t = x1.shape[0]
    tm = min(512, t)
    return pl.pallas_call(
        _combine_kernel,
        grid=(t // tm,),
        in_specs=[pl.BlockSpec((TOP_K, tm, D_MODEL // 2), lambda i: (0, i, 0)),
                  pl.BlockSpec((tm, TOP_K), lambda i: (i, 0)),
                  pl.BlockSpec((tm, D_MODEL), lambda i: (i, 0)),
                  pl.BlockSpec((1, D_MODEL), lambda i: (0, 0))],
        out_specs=pl.BlockSpec((tm, D_MODEL), lambda i: (i, 0)),
        out_shape=jax.ShapeDtypeStruct((t, D_MODEL), F32),
        compiler_params=pltpu.CompilerParams(dimension_semantics=("parallel",)),
        name="combine",
    )(y4, gate_t, x1, norm_w.reshape(1, D_MODEL))


def _moe(x1, hn, logits_t, w_gu, b_gu, w_down, b_down, norm_final):
    t = x1.shape[0]
    tt = min(512, t)
    eidx, gate, rank, base, cnt = _route(logits_t, tt)
    cnt = cnt[:, :, 0].astype(I32)
    dest, slot_tok, mb_expert, mb_active, n_mb = _routing_plan(cnt, eidx, rank, tt)
    xb = _sc_gather(hn, slot_tok)
    bg = b_gu[:, None, 0::2]
    bl = b_gu[:, None, 1::2]
    yb = _experts(xb, mb_expert, mb_active, w_gu, w_down, bg, bl, b_down[:, None, :], n_mb)
    y4 = _sc_gather(yb, dest.reshape(-1)).reshape(TOP_K, t, D_MODEL // 2)
    return _combine(y4, gate.T, x1, norm_final)


def kernel(x, norm_mix, w_in, gdn_conv, gdn_A_log, gdn_dt_bias, gdn_norm, rwkv_mu, rwkv_w0, rwkv_w2, rwkv_a0, rwkv_a2, rwkv_g2, rwkv_k_k, rwkv_k_a, rwkv_r_k, rwkv_ln_w, rwkv_ln_b, proj_a, proj_b, w_out, norm_ffn, router_w, router_b, w_gate_up, b_gate_up, w_down, b_down, norm_final):
    bsz, seq, d = x.shape
    depth = w_in.shape[0]
    x2 = x.reshape(bsz * seq, d)
    out = None
    for l in range(depth):
        w = w_in[l]
        w_main = jnp.concatenate([w[:, 0:4096], w[:, 4112:7184], w[:, 7440:9488]], axis=1).astype(BF16)
        w_aux = jnp.concatenate([w[:, 7184:7440], w[:, 4096:4112],
                                 jnp.zeros((d, AUX_COLS - LORA_COLS - 16), w.dtype)], axis=1).astype(BF16)
        p, p_aux = _in_proj(x2, norm_mix[l], w_main, w_aux)
        ya = _gdn(p, p_aux, gdn_conv[l], gdn_A_log[l], gdn_dt_bias[l], gdn_norm[l], bsz, seq)
        yb = _rwkv(p, p_aux, rwkv_mu[l], rwkv_w0[l], rwkv_w2[l], rwkv_a0[l], rwkv_a2[l], rwkv_g2[l],
                   rwkv_k_k[l], rwkv_k_a[l], rwkv_r_k[l], rwkv_ln_w[l], rwkv_ln_b[l], bsz, seq)
        x1, hn, logits_t = _merge(x2, ya, yb, p, proj_a[l], proj_b[l], w_out[l], norm_ffn[l],
                                  router_w[l], router_b[l])
        assert l == depth - 1, "only the final layer's residual is fused with the output norm"
        out = _moe(x1, hn, logits_t, w_gate_up[l], b_gate_up[l], w_down[l], b_down[l], norm_final)
    return out.reshape(bsz, seq, d)
```

```python
import functools

import jax
import jax.numpy as jnp
import numpy as np
from jax import lax
from jax.experimental import pallas as pl
from jax.experimental.pallas import tpu as pltpu
from jax.experimental.pallas import tpu_sc as plsc

F32 = jnp.float32
BF16 = jnp.bfloat16
I32 = jnp.int32
U32 = jnp.uint32
HIGHEST = lax.Precision.HIGHEST

D_MODEL = 1024
EPS = 1e-6
CHUNK = 64
GDN_HEADS = 8
GDN_DK = 128
CONV_W = 4
RWKV_HEADS = 16
RWKV_N = 64
RWKV_GN_EPS = 64e-5
LORA_COLS = 256
N_EXPERTS = 32
TOP_K = 4
SWIGLU_ALPHA = 1.702
SWIGLU_LIMIT = 7.0

MAIN_COLS = 9216
AUX_COLS = 384
HEADS_PER_GROUP = 4
PACK_W = HEADS_PER_GROUP * CHUNK
INV_LEAF = 16
GDN_CHUNKS_PER_ITER = 2

SC_GATHER_ROWS = 64
SC_INDEX_CHUNK = 2048
EXPERT_BLOCK = 512
VMEM_LIMIT = 48 * 1024 * 1024


def _mm(a, b):
    return jnp.dot(a.astype(BF16), b.astype(BF16), preferred_element_type=F32)


def _mm_nt(a, b):
    return lax.dot_general(a.astype(BF16), b.astype(BF16), (((1,), (1,)), ((), ())),
                           preferred_element_type=F32)


def _mm_tn(a, b):
    return lax.dot_general(a.astype(BF16), b.astype(BF16), (((0,), (0,)), ((), ())),
                           preferred_element_type=F32)


def _split_bf16(x, terms):
    parts = []
    for _ in range(terms - 1):
        hi = x.astype(BF16)
        parts.append(hi)
        x = x - hi.astype(F32)
    parts.append(x.astype(BF16))
    return parts


def _mm_exact_lhs(a_bf16, b):
    out = None
    for part in _split_bf16(b, 3):
        d = jnp.dot(a_bf16, part, preferred_element_type=F32)
        out = d if out is None else out + d
    return out


def _pack_bf16_pairs(x):
    half = x.shape[1] // 2
    lo = lax.bitcast_convert_type(x[:, :half].astype(BF16).astype(F32), U32) >> 16
    hi = lax.bitcast_convert_type(x[:, half:].astype(BF16).astype(F32), U32) & jnp.uint32(0xFFFF0000)
    return lo | hi


def _unpack_bf16_pairs(w):
    lo = lax.bitcast_convert_type(w << 16, F32)
    hi = lax.bitcast_convert_type(w & jnp.uint32(0xFFFF0000), F32)
    return jnp.concatenate([lo, hi], axis=1)


def _iota(shape, dim):
    return lax.broadcasted_iota(I32, shape, dim)


def _softplus(x):
    return jnp.maximum(x, 0.0) + jnp.log(1.0 + jnp.exp(-jnp.abs(x)))


def _stack_rows(x, n):
    return jnp.concatenate([x] * n, axis=0)


def _tri_inv_packed(ms, eye, bdmask):
    def bd(x):
        return jnp.where(bdmask, _stack_rows(x, HEADS_PER_GROUP), jnp.zeros((), BF16))

    def mul(a, b):
        ah, al = _split_bf16(a, 2)
        bh, bl = _split_bf16(b, 2)
        bdh = bd(bh)
        return (jnp.dot(ah, bdh, preferred_element_type=F32) + jnp.dot(al, bdh, preferred_element_type=F32)
                + jnp.dot(ah, bd(bl), preferred_element_type=F32))

    lane = _iota((CHUNK, PACK_W), 1) % CHUNK
    leaf = (lane // INV_LEAF) == (_iota((CHUNK, PACK_W), 0) // INV_LEAF)
    ds = [jnp.where(leaf, m, 0.0) for m in ms]
    ls = [m - d for m, d in zip(ms, ds)]
    ts = [eye - d for d in ds]
    xs = [mul(d, d) for d in ds]
    for _ in range(2):
        rs = [mul(jnp.concatenate([t, x], axis=0), x) for t, x in zip(ts, xs)]
        ts = [t + r[:CHUNK] for t, r in zip(ts, rs)]
        xs = [r[CHUNK:] for r in rs]
    bs = [t + mul(t, x) for t, x in zip(ts, xs)]
    ns = [mul(b, l) for b, l in zip(bs, ls)]
    n2 = [mul(n, n) for n in ns]
    ps = [(eye - n) + mul(eye - n, q) for n, q in zip(ns, n2)]
    return [mul(p, b) for p, b in zip(ps, bs)]


def _shift_rows(x, prev8, k, row8):
    r = pltpu.roll(x, k, 0)
    pr = pltpu.roll(prev8, k, 0)
    head = jnp.where(row8 < k, pr, r[:8])
    return jnp.concatenate([head, r[8:]], axis=0)


def _in_proj_kernel(x_ref, nw_ref, w_ref, wa_ref, o_ref, oa_ref, h_ref):
    @pl.when(pl.program_id(1) == 0)
    def _():
        x = x_ref[...]
        y = x * lax.rsqrt(jnp.mean(x * x, axis=-1, keepdims=True) + EPS)
        h_ref[...] = (y * nw_ref[...]).astype(BF16)
        oa_ref[...] = jnp.dot(h_ref[...], wa_ref[...], preferred_element_type=F32)

    o_ref[...] = jnp.dot(h_ref[...], w_ref[...], preferred_element_type=F32).astype(BF16)


def _in_proj(x2, norm_w, w_main, w_aux):
    t = x2.shape[0]
    tm = min(1024, t)
    tn = 2304
    return pl.pallas_call(
        _in_proj_kernel,
        grid=(t // tm, MAIN_COLS // tn),
        in_specs=[pl.BlockSpec((tm, D_MODEL), lambda i, j: (i, 0)),
                  pl.BlockSpec((1, D_MODEL), lambda i, j: (0, 0)),
                  pl.BlockSpec((D_MODEL, tn), lambda i, j: (0, j)),
                  pl.BlockSpec((D_MODEL, AUX_COLS), lambda i, j: (0, 0))],
        out_specs=[pl.BlockSpec((tm, tn), lambda i, j: (i, j)),
                   pl.BlockSpec((tm, AUX_COLS), lambda i, j: (i, 0))],
        out_shape=[jax.ShapeDtypeStruct((t, MAIN_COLS), BF16), jax.ShapeDtypeStruct((t, AUX_COLS), F32)],
        scratch_shapes=[pltpu.VMEM((tm, D_MODEL), BF16)],
        compiler_params=pltpu.CompilerParams(dimension_semantics=("parallel", "arbitrary"),
                                             vmem_limit_bytes=VMEM_LIMIT),
        name="in_proj",
    )(x2, norm_w.reshape(1, D_MODEL), w_main, w_aux)


def _gdn_kernel(q_ref, k_ref, v_ref, z_ref, ba_ref, conv_ref, alog_ref, dtb_ref, nw_ref, o_ref,
                carry_ref, qn_ref, kn_ref, vv_ref, beta_ref, g_ref, s_ref, *, ts):
    @pl.when(pl.program_id(1) == 0)
    def _():
        carry_ref[...] = jnp.zeros_like(carry_ref)
        s_ref[...] = jnp.zeros_like(s_ref)

    row8 = _iota((8, D_MODEL), 0)

    def conv_silu(idx, x_ref):
        x = x_ref[...].astype(F32)
        prev8 = carry_ref[idx]
        w4 = conv_ref[:, idx * D_MODEL:(idx + 1) * D_MODEL]
        y = x * w4[CONV_W - 1:CONV_W]
        for k in range(1, CONV_W):
            y = y + _shift_rows(x, prev8, k, row8) * w4[CONV_W - 1 - k:CONV_W - k]
        carry_ref[idx] = x[ts - 8:ts]
        return y * jax.nn.sigmoid(y)

    def l2norm_heads(x, scale):
        parts = []
        for h in range(GDN_HEADS):
            xh = x[:, h * GDN_DK:(h + 1) * GDN_DK]
            parts.append(xh * (lax.rsqrt(jnp.sum(xh * xh, axis=-1, keepdims=True) + EPS) * scale))
        return jnp.concatenate(parts, axis=1)

    qn_ref[...] = l2norm_heads(conv_silu(0, q_ref), GDN_DK ** -0.5)
    kn_ref[...] = l2norm_heads(conv_silu(1, k_ref), 1.0)
    vv_ref[...] = conv_silu(2, v_ref)
    ba = ba_ref[...]
    beta_ref[...] = jax.nn.sigmoid(ba)
    g_ref[...] = -jnp.exp(alog_ref[...]) * _softplus(ba + dtb_ref[...])

    gw = HEADS_PER_GROUP * GDN_DK
    lane_p = _iota((CHUNK, PACK_W), 1)
    row_p = _iota((CHUNK, PACK_W), 0)
    eye = jnp.where((lane_p % CHUNK) == row_p, 1.0, 0.0).astype(F32)
    strict = (lane_p % CHUNK) < row_p
    incl = (lane_p % CHUNK) <= row_p
    bd_pp = (_iota((PACK_W, PACK_W), 0) // CHUNK) == (_iota((PACK_W, PACK_W), 1) // CHUNK)
    bd_pn = (_iota((PACK_W, gw), 0) // CHUNK) == (_iota((PACK_W, gw), 1) // GDN_DK)
    bd_pn2 = (_iota((PACK_W, 2 * gw), 0) // CHUNK) == ((_iota((PACK_W, 2 * gw), 1) % gw) // GDN_DK)
    ltri = jnp.where(_iota((CHUNK, CHUNK), 1) <= _iota((CHUNK, CHUNK), 0), 1.0, 0.0).astype(BF16)
    ones_cc = jnp.ones((CHUNK, CHUNK), BF16)
    lane128 = _iota((CHUNK, 128), 1)
    nw = nw_ref[...]

    n_groups = GDN_HEADS // HEADS_PER_GROUP
    hcols = [slice(j * GDN_DK, (j + 1) * GDN_DK) for j in range(HEADS_PER_GROUP)]

    def intra_chunk(rows_list):
        units = [(i, gi) for i in range(len(rows_list)) for gi in range(n_groups)]
        gcs = [_mm_exact_lhs(ltri, g_ref[rows, :]) for rows in rows_list]
        beta = [beta_ref[rows, :] for rows in rows_list]
        pre = []
        for i, gi in units:
            rows = rows_list[i]
            heads = [gi * HEADS_PER_GROUP + j for j in range(HEADS_PER_GROUP)]
            cols = slice(gi * gw, (gi + 1) * gw)
            beta_nat = jnp.concatenate(
                [jnp.broadcast_to(beta[i][:, h:h + 1], (CHUNK, GDN_DK)) for h in heads], axis=1)
            gcol = [jnp.broadcast_to(gcs[i][:, 8 + h:9 + h], (CHUNK, GDN_DK)) for h in heads]
            gcol_p = jnp.concatenate([jnp.where(lane128 < CHUNK, gcol[0], gcol[1]),
                                      jnp.where(lane128 < CHUNK, gcol[2], gcol[3])], axis=1)
            pre.append(dict(qn=qn_ref[rows, cols], kn=kn_ref[rows, cols], vv=vv_ref[rows, cols],
                            beta_nat=beta_nat, gc_nat=jnp.concatenate(gcol, axis=1), gcol_p=gcol_p))
        grow = [_mm_exact_lhs(ones_cc, u["gcol_p"] * eye) for u in pre]
        aas = [_mm_nt(jnp.concatenate([u["kn"] * u["beta_nat"], u["qn"]], axis=0),
                      jnp.where(bd_pn, _stack_rows(u["kn"], HEADS_PER_GROUP), 0.0)) for u in pre]
        decay = [jnp.exp(jnp.minimum(u["gcol_p"] - gr, 0.0)) for u, gr in zip(pre, grow)]
        t_inv = _tri_inv_packed([jnp.where(strict, aa[:CHUNK] * d, 0.0) for aa, d in zip(aas, decay)],
                                eye, bd_pp)
        out = []
        for u, aa, d, ti in zip(pre, aas, decay, t_inv):
            egc = jnp.exp(u["gc_nat"])
            data = jnp.concatenate([u["vv"] * u["beta_nat"], u["kn"] * u["beta_nat"] * egc], axis=1)
            uw = _mm(ti, jnp.where(bd_pn2, _stack_rows(data, HEADS_PER_GROUP), 0.0))
            gl = u["gc_nat"][CHUNK - 1:CHUNK, :]
            out.append(dict(uw=uw, attn=jnp.where(incl, aa[CHUNK:] * d, 0.0), qd=u["qn"] * egc,
                            kd=u["kn"] * jnp.exp(gl - u["gc_nat"]), egl=jnp.exp(gl)))
        return out

    def recurrence(rows, group_units):
        wq = {}
        for gi, u in enumerate(group_units):
            for j, hc in enumerate(hcols):
                h = gi * HEADS_PER_GROUP + j
                wq[h] = _mm(jnp.concatenate([u["uw"][:, gw + j * GDN_DK:gw + (j + 1) * GDN_DK],
                                             u["qd"][:, hc]], axis=0), s_ref[h])
        vnew = {}
        for gi, u in enumerate(group_units):
            for j, hc in enumerate(hcols):
                h = gi * HEADS_PER_GROUP + j
                vnew[h] = u["uw"][:, hc] - wq[h][:CHUNK]
                s_ref[h] = s_ref[h] * u["egl"][:, hc] + _mm_tn(u["kd"][:, hc], vnew[h])
        for gi, u in enumerate(group_units):
            heads = [gi * HEADS_PER_GROUP + j for j in range(HEADS_PER_GROUP)]
            vn = jnp.concatenate([vnew[h] for h in heads], axis=1)
            o = jnp.concatenate([wq[h][CHUNK:] for h in heads], axis=1) + _mm(
                u["attn"], jnp.where(bd_pn, _stack_rows(vn, HEADS_PER_GROUP), 0.0))
            for j, h in enumerate(heads):
                oh = o[:, hcols[j]]
                oh = oh * lax.rsqrt(jnp.mean(oh * oh, axis=-1, keepdims=True) + EPS) * nw
                zh = z_ref[rows, h * GDN_DK:(h + 1) * GDN_DK].astype(F32)
                o_ref[rows, h * GDN_DK:(h + 1) * GDN_DK] = (oh * (zh * jax.nn.sigmoid(zh))).astype(BF16)

    def chunk_pair_body(c, carry):
        rows_list = [pl.ds(pl.multiple_of((c * GDN_CHUNKS_PER_ITER + i) * CHUNK, CHUNK), CHUNK)
                     for i in range(GDN_CHUNKS_PER_ITER)]
        units = intra_chunk(rows_list)
        for i, rows in enumerate(rows_list):
            recurrence(rows, units[i * n_groups:(i + 1) * n_groups])
        return carry

    lax.fori_loop(0, ts // (CHUNK * GDN_CHUNKS_PER_ITER), chunk_pair_body, 0)


def _gdn(p, p_aux, conv_w, a_log, dt_bias, norm_w, bsz, seq):
    ts = min(256, seq)
    ns = seq // ts
    pad8 = lambda v: jnp.zeros((1, 128), F32).at[0, 8:16].set(v)
    tok = lambda col: pl.BlockSpec((ts, D_MODEL), lambda b, s, col=col: (b * ns + s, col))
    const = lambda shape: pl.BlockSpec(shape, lambda b, s: (0,) * len(shape))
    return pl.pallas_call(
        functools.partial(_gdn_kernel, ts=ts),
        grid=(bsz, ns),
        in_specs=[tok(0), tok(1), tok(2), tok(3),
                  pl.BlockSpec((ts, 128), lambda b, s: (b * ns + s, LORA_COLS // 128)),
                  const((CONV_W, 3 * D_MODEL)), const((1, 128)), const((1, 128)), const((1, GDN_DK))],
        out_specs=pl.BlockSpec((ts, D_MODEL), lambda b, s: (b * ns + s, 0)),
        out_shape=jax.ShapeDtypeStruct((bsz * seq, D_MODEL), BF16),
        scratch_shapes=[pltpu.VMEM((3, 8, D_MODEL), F32),
                        pltpu.VMEM((ts, D_MODEL), F32), pltpu.VMEM((ts, D_MODEL), F32),
                        pltpu.VMEM((ts, D_MODEL), F32),
                        pltpu.VMEM((ts, 128), F32), pltpu.VMEM((ts, 128), F32),
                        pltpu.VMEM((GDN_HEADS, GDN_DK, GDN_DK), F32)],
        compiler_params=pltpu.CompilerParams(dimension_semantics=("parallel", "arbitrary"),
                                             vmem_limit_bytes=VMEM_LIMIT),
        name="gdn",
    )(p, p, p, p, p_aux, conv_w, pad8(a_log), pad8(dt_bias), norm_w.reshape(1, GDN_DK))


def _rwkv_kernel(r_ref, k_ref, v_ref, lo_ref, mu_ref, w0_ref, w2_ref, a0_ref, a2_ref, g2_ref,
                 kk_ref, ka_ref, rk_ref, lnw_ref, lnb_ref, seg_ref, o_ref,
                 carry_ref, rs_ref, ks_ref, vs_ref, kks_ref, bs_ref, ld_ref, gg_ref, os_ref, st_ref,
                 *, ts):
    @pl.when(pl.program_id(1) == 0)
    def _():
        carry_ref[...] = jnp.zeros_like(carry_ref)
        st_ref[...] = jnp.zeros_like(st_ref)

    width = D_MODEL
    row8 = _iota((8, width), 0)
    row8l = _iota((8, LORA_COLS), 0)

    def seg_sum(x):
        hi = x.astype(BF16)
        lo = (x - hi.astype(F32)).astype(BF16)
        parts = []
        for g in range(width // PACK_W):
            cs = slice(g * PACK_W, (g + 1) * PACK_W)
            parts.append(jnp.dot(hi[:, cs], seg_ref[...], preferred_element_type=F32)
                         + jnp.dot(lo[:, cs], seg_ref[...], preferred_element_type=F32))
        return jnp.concatenate(parts, axis=1)

    def lerp(idx, x_ref, mu, r8):
        x = x_ref[...].astype(F32)
        xs = x + (_shift_rows(x, carry_ref[idx, :, :x.shape[1]], 1, r8) - x) * mu
        carry_ref[idx, :, :x.shape[1]] = x[ts - 8:ts]
        return xs

    r = lerp(0, r_ref, mu_ref[:, 0:width], row8)
    k = lerp(1, k_ref, mu_ref[:, width:2 * width], row8)
    v = lerp(2, v_ref, mu_ref[:, 2 * width:3 * width], row8)
    lo = lerp(3, lo_ref, mu_ref[:, 3 * width:3 * width + LORA_COLS], row8l)
    lo_a = lo[:, :128]
    w_log = -_softplus(-(w0_ref[...] + _mm(jnp.tanh(lo_a), w2_ref[...]))) - 0.5
    ld_ref[...] = -jnp.exp(w_log)
    aa = jax.nn.sigmoid(a0_ref[...] + _mm(lo_a, a2_ref[...]))
    gg_ref[...] = _mm(jax.nn.sigmoid(lo[:, 128:]), g2_ref[...])
    kx = k * kk_ref[...]
    kk = kx * lax.rsqrt(seg_sum(kx * kx) + EPS)
    k = k * (1.0 + (aa - 1.0) * ka_ref[...])
    rs_ref[...] = r
    ks_ref[...] = k
    vs_ref[...] = v
    kks_ref[...] = kk
    bs_ref[...] = kk * aa

    lane_p = _iota((CHUNK, PACK_W), 1)
    row_p = _iota((CHUNK, PACK_W), 0)
    eye = jnp.where((lane_p % CHUNK) == row_p, 1.0, 0.0).astype(F32)
    strict = (lane_p % CHUNK) < row_p
    incl = (lane_p % CHUNK) <= row_p
    bd = (_iota((PACK_W, PACK_W), 0) // CHUNK) == (_iota((PACK_W, PACK_W), 1) // CHUNK)
    bd2 = jnp.concatenate([bd, bd], axis=1)
    ltri = jnp.where(_iota((CHUNK, CHUNK), 1) <= _iota((CHUNK, CHUNK), 0), 1.0, 0.0).astype(BF16)

    def bdiag(x):
        return jnp.where(bd, _stack_rows(x, HEADS_PER_GROUP), 0.0)

    n_groups = width // PACK_W
    gcols = [slice(g * PACK_W, (g + 1) * PACK_W) for g in range(n_groups)]

    def chunk_body(c, carry):
        rows = pl.ds(pl.multiple_of(c * CHUNK, CHUNK), CHUNK)
        ld = [ld_ref[rows, cols] for cols in gcols]
        cs = [_mm_exact_lhs(ltri, x) for x in ld]
        pre = []
        for g, cols in enumerate(gcols):
            cl = cs[g][CHUNK - 1:CHUNK, :]
            e_neg = jnp.exp(-cs[g])
            e_dec = jnp.exp(cl - cs[g])
            kc, bc = ks_ref[rows, cols], bs_ref[rows, cols]
            rt = rs_ref[rows, cols] * jnp.exp(cs[g])
            kt = kks_ref[rows, cols] * jnp.exp(cs[g] - ld[g])
            pre.append(dict(cl=cl, rt=rt, kt=kt, vc=vs_ref[rows, cols], kh=kc * e_neg, bh=bc * e_neg,
                            kdec=kc * e_dec, bdec=bc * e_dec))
        akb = [_mm_nt(jnp.concatenate([u["kt"], u["rt"]], axis=0),
                      jnp.concatenate([bdiag(u["kh"]), bdiag(u["bh"])], axis=0)) for u in pre]
        t_inv = _tri_inv_packed([jnp.where(strict, a[:CHUNK, PACK_W:], 0.0) for a in akb], eye, bd)
        xo = [_mm(jnp.concatenate([jnp.where(strict, a[:CHUNK, :PACK_W], 0.0),
                                   jnp.where(incl, a[CHUNK:, :PACK_W], 0.0)], axis=0), bdiag(u["vc"]))
              for a, u in zip(akb, pre)]
        tt = [_mm(ti, jnp.where(bd2, _stack_rows(jnp.concatenate([x[:CHUNK], u["kt"]], axis=1),
                                                 HEADS_PER_GROUP), 0.0))
              for ti, x, u in zip(t_inv, xo, pre)]
        rr = [_mm(jnp.where(incl, a[CHUNK:, PACK_W:], 0.0),
                  jnp.where(bd2, _stack_rows(jnp.concatenate([t[:, PACK_W:], t[:, :PACK_W]], axis=1),
                                             HEADS_PER_GROUP), 0.0))
              for a, t in zip(akb, tt)]
        uo = [_mm_nt(jnp.concatenate([t[:, PACK_W:], u["rt"] - r[:, :PACK_W]], axis=0), st_ref[g])
              for g, (t, r, u) in enumerate(zip(tt, rr, pre))]
        upd = []
        for g, cols in enumerate(gcols):
            os_ref[rows, cols] = xo[g][CHUNK:] - rr[g][:, PACK_W:] + uo[g][CHUNK:]
            u_full = tt[g][:, :PACK_W] + uo[g][:CHUNK]
            upd.append(_mm_tn(jnp.concatenate([pre[g]["vc"], u_full], axis=0),
                              jnp.concatenate([pre[g]["kdec"], -pre[g]["bdec"]], axis=0)))
        for g in range(n_groups):
            st_ref[g] = st_ref[g] * jnp.exp(pre[g]["cl"]) + jnp.where(bd, upd[g], 0.0)
        return carry

    lax.fori_loop(0, ts // CHUNK, chunk_body, 0)

    o = os_ref[...]
    inv_n = 1.0 / RWKV_N
    mean = seg_sum(o) * inv_n
    cen = o - mean
    var = seg_sum(cen * cen) * inv_n
    o = cen * lax.rsqrt(var + RWKV_GN_EPS) * lnw_ref[...] + lnb_ref[...]
    bonus = seg_sum(rs_ref[...] * ks_ref[...] * rk_ref[...]) * vs_ref[...]
    o_ref[...] = ((o + bonus) * gg_ref[...]).astype(BF16)


def _rwkv(p, p_aux, mu, w0, w2, a0, a2, g2, k_k, k_a, r_k, ln_w, ln_b, bsz, seq):
    ts = min(256, seq)
    ns = seq // ts
    width = D_MODEL
    row = lambda v: v.reshape(1, -1)
    w2p = jnp.concatenate([w2, jnp.zeros_like(w2)], axis=0).astype(BF16)
    a2p = jnp.concatenate([jnp.zeros_like(a2), a2], axis=0).astype(BF16)
    seg = (np.arange(PACK_W)[:, None] // RWKV_N == np.arange(PACK_W)[None, :] // RWKV_N)
    seg = jnp.asarray(seg, BF16)
    tok = lambda col: pl.BlockSpec((ts, width), lambda b, s, col=col: (b * ns + s, col))
    const = lambda shape: pl.BlockSpec(shape, lambda b, s: (0,) * len(shape))
    fbuf = lambda: pltpu.VMEM((ts, width), F32)
    return pl.pallas_call(
        functools.partial(_rwkv_kernel, ts=ts),
        grid=(bsz, ns),
        in_specs=[tok(4), tok(5), tok(6),
                  pl.BlockSpec((ts, LORA_COLS), lambda b, s: (b * ns + s, 0)),
                  const((1, 3 * width + LORA_COLS)), const((1, width)), const((128, width)),
                  const((1, width)), const((128, width)), const((128, width)),
                  const((1, width)), const((1, width)), const((1, width)), const((1, width)),
                  const((1, width)), const((PACK_W, PACK_W))],
        out_specs=pl.BlockSpec((ts, width), lambda b, s: (b * ns + s, 0)),
        out_shape=jax.ShapeDtypeStruct((bsz * seq, width), BF16),
        scratch_shapes=[pltpu.VMEM((4, 8, width), F32),
                        fbuf(), fbuf(), fbuf(), fbuf(), fbuf(), fbuf(), fbuf(), fbuf(),
                        pltpu.VMEM((width // PACK_W, PACK_W, PACK_W), F32)],
        compiler_params=pltpu.CompilerParams(dimension_semantics=("parallel", "arbitrary"),
                                             vmem_limit_bytes=VMEM_LIMIT),
        name="rwkv",
    )(p, p, p, p_aux, row(mu), row(w0), w2p, row(a0), a2p, g2.astype(BF16), row(k_k), row(k_a),
      row(r_k), row(ln_w), row(ln_b), seg)


def _merge_kernel(x_ref, ya_ref, yb_ref, ga_ref, gb_ref, pa_ref, pb_ref, wo_ref, nw_ref, rw_ref, rb_ref,
                  x1_ref, hn_ref, lg_ref):
    merged = (jax.nn.sigmoid(ga_ref[...].astype(F32)) * jnp.dot(ya_ref[...], pa_ref[...], preferred_element_type=F32)
              + jax.nn.sigmoid(gb_ref[...].astype(F32)) * jnp.dot(yb_ref[...], pb_ref[...], preferred_element_type=F32))
    x1 = x_ref[...] + _mm(merged, wo_ref[...])
    x1_ref[...] = x1
    hn = x1 * lax.rsqrt(jnp.mean(x1 * x1, axis=-1, keepdims=True) + EPS) * nw_ref[...]
    hn_ref[...] = _pack_bf16_pairs(hn)
    lg_ref[...] = lax.dot_general(rw_ref[...], hn, (((1,), (1,)), ((), ())),
                                  preferred_element_type=F32, precision=HIGHEST) + rb_ref[...]


def _merge(x2, ya, yb, p, proj_a, proj_b, w_out, norm_w, router_w, router_b):
    t = x2.shape[0]
    tm = min(512, t)
    tok = lambda col: pl.BlockSpec((tm, D_MODEL), lambda i, col=col: (i, col))
    const = lambda shape: pl.BlockSpec(shape, lambda i: (0,) * len(shape))
    return pl.pallas_call(
        _merge_kernel,
        grid=(t // tm,),
        in_specs=[tok(0), tok(0), tok(0), tok(7), tok(8),
                  const((D_MODEL, D_MODEL)), const((D_MODEL, D_MODEL)), const((D_MODEL, D_MODEL)),
                  const((1, D_MODEL)), const((N_EXPERTS, D_MODEL)), const((N_EXPERTS, 1))],
        out_specs=[tok(0), pl.BlockSpec((tm, D_MODEL // 2), lambda i: (i, 0)),
                   pl.BlockSpec((N_EXPERTS, tm), lambda i: (0, i))],
        out_shape=[jax.ShapeDtypeStruct((t, D_MODEL), F32), jax.ShapeDtypeStruct((t, D_MODEL // 2), U32),
                   jax.ShapeDtypeStruct((N_EXPERTS, t), F32)],
        compiler_params=pltpu.CompilerParams(dimension_semantics=("parallel",),
                                             vmem_limit_bytes=VMEM_LIMIT),
        name="merge",
    )(x2, ya, yb, p, p, proj_a.astype(BF16), proj_b.astype(BF16), w_out.astype(BF16),
      norm_w.reshape(1, D_MODEL), router_w.T, router_b.reshape(N_EXPERTS, 1))


def _route_kernel(lg_ref, eidx_ref, gate_ref, rank_ref, base_ref, cnt_ref, carry_ref, *, tt):
    @pl.when(pl.program_id(0) == 0)
    def _():
        carry_ref[...] = jnp.zeros_like(carry_ref)

    l = lg_ref[...]
    ie = _iota((N_EXPERTS, tt), 0)
    vals, hots, idxs = [], [], []
    for _ in range(TOP_K):
        m = jnp.max(l, axis=0, keepdims=True)
        idx = jnp.min(jnp.where(l == m, ie, N_EXPERTS), axis=0, keepdims=True)
        hot = ie == idx
        vals.append(m)
        hots.append(hot)
        idxs.append(idx)
        l = jnp.where(hot, -jnp.inf, l)
    exps = [jnp.exp(v - vals[0]) for v in vals]
    den = exps[0] + exps[1] + exps[2] + exps[3]
    gate_ref[...] = jnp.concatenate([e / den for e in exps], axis=0)
    eidx_ref[...] = jnp.concatenate(idxs, axis=0)

    sel = jnp.zeros((N_EXPERTS, tt), F32)
    for hot in hots:
        sel = sel + jnp.where(hot, 1.0, 0.0)
    before = jnp.where(_iota((tt, tt), 0) < _iota((tt, tt), 1), 1.0, 0.0).astype(BF16)
    carry = carry_ref[...]
    prefix = jnp.dot(sel.astype(BF16), before, preferred_element_type=F32) + carry[:, 0:1]
    rank_ref[...] = jnp.concatenate(
        [jnp.sum(jnp.where(hot, prefix, 0.0), axis=0, keepdims=True) for hot in hots], axis=0).astype(I32)
    cnt = jnp.broadcast_to(jnp.sum(sel, axis=1, keepdims=True), (N_EXPERTS, 128))
    base_ref[0] = carry
    cnt_ref[0] = cnt
    carry_ref[...] = carry + cnt


def _route(logits_t, tt):
    t = logits_t.shape[1]
    nt = t // tt
    row4 = pl.BlockSpec((TOP_K, tt), lambda i: (0, i))
    per_tile = pl.BlockSpec((1, N_EXPERTS, 128), lambda i: (i, 0, 0))
    return pl.pallas_call(
        functools.partial(_route_kernel, tt=tt),
        grid=(nt,),
        in_specs=[pl.BlockSpec((N_EXPERTS, tt), lambda i: (0, i))],
        out_specs=[row4, row4, row4, per_tile, per_tile],
        out_shape=[jax.ShapeDtypeStruct((TOP_K, t), I32), jax.ShapeDtypeStruct((TOP_K, t), F32),
                   jax.ShapeDtypeStruct((TOP_K, t), I32),
                   jax.ShapeDtypeStruct((nt, N_EXPERTS, 128), F32),
                   jax.ShapeDtypeStruct((nt, N_EXPERTS, 128), F32)],
        scratch_shapes=[pltpu.VMEM((N_EXPERTS, 128), F32)],
        compiler_params=pltpu.CompilerParams(dimension_semantics=("arbitrary",)),
        name="route",
    )(logits_t)


def _count_le(sorted_vals, queries):
    return jnp.sum((sorted_vals[None, :] <= queries[:, None]).astype(I32), axis=1)


def _routing_plan(cnt, eidx, rank, tt):
    t = cnt.shape[0] * tt
    n_mb = (t * TOP_K) // EXPERT_BLOCK + N_EXPERTS
    counts = jnp.sum(cnt, axis=0)
    padded = ((counts + EXPERT_BLOCK - 1) // EXPERT_BLOCK) * EXPERT_BLOCK
    end_pad = jnp.cumsum(padded)
    start_pad = end_pad - padded
    hot = eidx[:, :, None] == jnp.arange(N_EXPERTS, dtype=I32)[None, None, :]
    dest = jnp.sum(jnp.where(hot, start_pad[None, None, :], 0), axis=-1) + rank

    mb_start = jnp.arange(n_mb, dtype=I32) * EXPERT_BLOCK
    mb_expert = jnp.minimum(_count_le(end_pad, mb_start), N_EXPERTS - 1).astype(I32)
    mb_active = (mb_start < end_pad[-1]).astype(I32)

    dest = dest.astype(I32)
    return dest, _sc_slot_tokens(dest, n_mb * EXPERT_BLOCK), mb_expert, mb_active, n_mb


def _sc_slot_tokens(dest, n_slots):
    info = plsc.get_sparse_core_info()
    nc, ns, nl = info.num_cores, info.num_subcores, info.num_lanes
    top_k, t = dest.shape
    per_worker = n_slots // (nc * ns)
    assert per_worker * nc * ns == n_slots and per_worker % nl == 0 and t % SC_INDEX_CHUNK == 0
    fill_mask = (1 << (t.bit_length() - 1)) - 1
    mesh = plsc.VectorSubcoreMesh(core_axis_name="c", subcore_axis_name="s")

    @functools.partial(
        pl.kernel, mesh=mesh,
        out_type=jax.ShapeDtypeStruct((n_slots,), I32),
        scratch_types=[pltpu.VMEM((per_worker,), I32), pltpu.VMEM((SC_INDEX_CHUNK,), I32)],
        compiler_params=pltpu.CompilerParams(needs_layout_passes=False),
    )
    def invert(dest_hbm, out_hbm, local_v, chunk_v):
        lo = (lax.axis_index("s") * nc + lax.axis_index("c")) * per_worker
        lane = lax.iota(I32, nl)

        @pl.loop(0, per_worker // nl)
        def _(j):
            local_v[pl.ds(j * nl, nl)] = (lo + j * nl + lane) & fill_mask

        for k in range(top_k):
            @pl.loop(0, t // SC_INDEX_CHUNK)
            def _(c):
                pltpu.sync_copy(dest_hbm.at[k, pl.ds(c * SC_INDEX_CHUNK, SC_INDEX_CHUNK)], chunk_v)

                @pl.loop(0, SC_INDEX_CHUNK // nl)
                def _(j):
                    d = chunk_v[pl.ds(j * nl, nl)] - lo
                    tok = c * SC_INDEX_CHUNK + j * nl + lane
                    plsc.store_scatter(local_v, [d], tok, mask=(d >= 0) & (d < per_worker))

        pltpu.sync_copy(local_v, out_hbm.at[pl.ds(lo, per_worker)])

    return invert(dest)


def _sc_gather(table, idx):
    info = plsc.get_sparse_core_info()
    nc, ns = info.num_cores, info.num_subcores
    n_rows, width = idx.shape[0], table.shape[1]
    per_worker = n_rows // (nc * ns)
    steps = per_worker // SC_GATHER_ROWS
    assert per_worker * nc * ns == n_rows and steps * SC_GATHER_ROWS == per_worker
    mesh = plsc.VectorSubcoreMesh(core_axis_name="c", subcore_axis_name="s")

    @functools.partial(
        pl.kernel, mesh=mesh,
        out_type=jax.ShapeDtypeStruct((n_rows, width), table.dtype),
        scratch_types=[pltpu.VMEM((SC_GATHER_ROWS,), I32),
                       pltpu.VMEM((SC_GATHER_ROWS, width), table.dtype),
                       pltpu.SemaphoreType.DMA],
    )
    def gather(table_hbm, idx_hbm, out_hbm, idx_v, rows_v, sem):
        base = (lax.axis_index("s") * nc + lax.axis_index("c")) * per_worker

        @pl.loop(0, steps)
        def _(i):
            off = pl.multiple_of(base + i * SC_GATHER_ROWS, 8)
            pltpu.sync_copy(idx_hbm.at[pl.ds(off, SC_GATHER_ROWS)], idx_v)
            pltpu.async_copy(table_hbm.at[idx_v], rows_v, sem).wait()
            pltpu.sync_copy(rows_v, out_hbm.at[pl.ds(off, SC_GATHER_ROWS)])

    return gather(table, idx)


def _expert_kernel(e_ref, act_ref, x_ref, wgu_ref, wd_ref, bg_ref, bl_ref, bd_ref, o_ref,
                   wg_c, wl_c, wd_c):
    mb = pl.program_id(0)
    new_expert = jnp.logical_or(mb == 0, e_ref[mb] != e_ref[jnp.maximum(mb - 1, 0)])

    @pl.when(jnp.logical_and(new_expert, act_ref[mb] == 1))
    def _():
        lane = _iota((D_MODEL, 128), 1)
        half = lane < 64
        idx = jnp.where(half, 2 * lane, 2 * (lane - 64) + 1)
        for m in range(D_MODEL // 128):
            a = jnp.take_along_axis(wgu_ref[0, :, (2 * m) * 128:(2 * m + 1) * 128], idx, axis=1)
            b = jnp.take_along_axis(wgu_ref[0, :, (2 * m + 1) * 128:(2 * m + 2) * 128], idx, axis=1)
            cols = slice(m * 128, (m + 1) * 128)
            wg_c[:, cols] = jnp.where(half, a, pltpu.roll(b, 64, 1)).astype(BF16)
            wl_c[:, cols] = jnp.where(half, pltpu.roll(a, 64, 1), b).astype(BF16)
        wd_c[...] = wd_ref[0].astype(BF16)

    @pl.when(act_ref[mb] == 0)
    def _():
        o_ref[...] = jnp.zeros_like(o_ref)

    @pl.when(act_ref[mb] == 1)
    def _():
        x = _unpack_bf16_pairs(x_ref[...]).astype(BF16)
        glu = jnp.dot(x, wg_c[...], preferred_element_type=F32) + bg_ref[0]
        lin = jnp.dot(x, wl_c[...], preferred_element_type=F32) + bl_ref[0]
        glu = jnp.minimum(glu, SWIGLU_LIMIT)
        lin = jnp.clip(lin, -SWIGLU_LIMIT, SWIGLU_LIMIT)
        act = glu * jax.nn.sigmoid(SWIGLU_ALPHA * glu) * (lin + 1.0)
        o_ref[...] = _pack_bf16_pairs(_mm(act, wd_c[...]) + bd_ref[0])


def _experts(xb, mb_expert, mb_active, w_gu, w_down, bg, bl, bd, n_mb):
    d_ff = w_down.shape[1]
    assert d_ff == D_MODEL and w_gu.shape[1:] == (D_MODEL, 2 * d_ff)
    bspec = pl.BlockSpec((1, 1, D_MODEL), lambda m, e, a: (e[m], 0, 0))
    xspec = pl.BlockSpec((EXPERT_BLOCK, D_MODEL // 2), lambda m, e, a: (m, 0))
    wcache = pltpu.VMEM((D_MODEL, D_MODEL), BF16)
    return pl.pallas_call(
        _expert_kernel,
        grid_spec=pltpu.PrefetchScalarGridSpec(
            num_scalar_prefetch=2,
            grid=(n_mb,),
            in_specs=[xspec,
                      pl.BlockSpec((1, D_MODEL, 2 * d_ff), lambda m, e, a: (e[m], 0, 0)),
                      pl.BlockSpec((1, d_ff, D_MODEL), lambda m, e, a: (e[m], 0, 0)),
                      bspec, bspec, bspec],
            out_specs=xspec,
            scratch_shapes=[wcache, wcache, wcache]),
        out_shape=jax.ShapeDtypeStruct(xb.shape, U32),
        compiler_params=pltpu.CompilerParams(dimension_semantics=("arbitrary",),
                                             vmem_limit_bytes=VMEM_LIMIT),
        name="experts",
    )(mb_expert, mb_active, xb, w_gu, w_down, bg, bl, bd)


def _combine_kernel(y4_ref, gate_ref, x1_ref, nw_ref, o_ref):
    g = gate_ref[...]
    y = x1_ref[...]
    for k in range(TOP_K):
        y = y + g[:, k:k + 1] * _unpack_bf16_pairs(y4_ref[k])
    o_ref[...] = y * lax.rsqrt(jnp.mean(y * y, axis=-1, keepdims=True) + EPS) * nw_ref[...]


def _combine(y4, gate_t, x1, norm_w):
    t = x1.shape[0]
    tm = min(512, t)
    return pl.pallas_call(
        _combine_kernel,
        grid=(t // tm,),
        in_specs=[pl.BlockSpec((TOP_K, tm, D_MODEL // 2), lambda i: (0, i, 0)),
                  pl.BlockSpec((tm, TOP_K), lambda i: (i, 0)),
                  pl.BlockSpec((tm, D_MODEL), lambda i: (i, 0)),
                  pl.BlockSpec((1, D_MODEL), lambda i: (0, 0))],
        out_specs=pl.BlockSpec((tm, D_MODEL), lambda i: (i, 0)),
        out_shape=jax.ShapeDtypeStruct((t, D_MODEL), F32),
        compiler_params=pltpu.CompilerParams(dimension_semantics=("parallel",)),
        name="combine",
    )(y4, gate_t, x1, norm_w.reshape(1, D_MODEL))


def _moe(x1, hn, logits_t, w_gu, b_gu, w_down, b_down, norm_final):
    t = x1.shape[0]
    tt = min(512, t)
    eidx, gate, rank, base, cnt = _route(logits_t, tt)
    cnt = cnt[:, :, 0].astype(I32)
    dest, slot_tok, mb_expert, mb_active, n_mb = _routing_plan(cnt, eidx, rank, tt)
    xb = _sc_gather(hn, slot_tok)
    bg = b_gu[:, None, 0::2]
    bl = b_gu[:, None, 1::2]
    yb = _experts(xb, mb_expert, mb_active, w_gu, w_down, bg, bl, b_down[:, None, :], n_mb)
    y4 = _sc_gather(yb, dest.reshape(-1)).reshape(TOP_K, t, D_MODEL // 2)
    return _combine(y4, gate.T, x1, norm_final)


def kernel(x, norm_mix, w_in, gdn_conv, gdn_A_log, gdn_dt_bias, gdn_norm, rwkv_mu, rwkv_w0, rwkv_w2, rwkv_a0, rwkv_a2, rwkv_g2, rwkv_k_k, rwkv_k_a, rwkv_r_k, rwkv_ln_w, rwkv_ln_b, proj_a, proj_b, w_out, norm_ffn, router_w, router_b, w_gate_up, b_gate_up, w_down, b_down, norm_final):
    bsz, seq, d = x.shape
    depth = w_in.shape[0]
    x2 = x.reshape(bsz * seq, d)
    out = None
    for l in range(depth):
        w = w_in[l]
        w_main = jnp.concatenate([w[:, 0:4096], w[:, 4112:7184], w[:, 7440:9488]], axis=1).astype(BF16)
        w_aux = jnp.concatenate([w[:, 7184:7440], w[:, 4096:4112],
                                 jnp.zeros((d, AUX_COLS - LORA_COLS - 16), w.dtype)], axis=1).astype(BF16)
        p, p_aux = _in_proj(x2, norm_mix[l], w_main, w_aux)
        ya = _gdn(p, p_aux, gdn_conv[l], gdn_A_log[l], gdn_dt_bias[l], gdn_norm[l], bsz, seq)
        yb = _rwkv(p, p_aux, rwkv_mu[l], rwkv_w0[l], rwkv_w2[l], rwkv_a0[l], rwkv_a2[l], rwkv_g2[l],
                   rwkv_k_k[l], rwkv_k_a[l], rwkv_r_k[l], rwkv_ln_w[l], rwkv_ln_b[l], bsz, seq)
        x1, hn, logits_t = _merge(x2, ya, yb, p, proj_a[l], proj_b[l], w_out[l], norm_ffn[l],
                                  router_w[l], router_b[l])
        assert l == depth - 1, "only the final layer's residual is fused with the output norm"
        out = _moe(x1, hn, logits_t, w_gate_up[l], b_gate_up[l], w_down[l], b_down[l], norm_final)
    return out.reshape(bsz, seq, d)
```

```python
import functools

import jax
import jax.numpy as jnp
import numpy as np
from jax import lax
from jax.experimental import pallas as pl
from jax.experimental.pallas import tpu as pltpu
from jax.experimental.pallas import tpu_sc as plsc

F32 = jnp.float32
BF16 = jnp.bfloat16
I32 = jnp.int32
U32 = jnp.uint32
HIGHEST = lax.Precision.HIGHEST

D_MODEL = 1024
EPS = 1e-6
CHUNK = 64
GDN_HEADS = 8
GDN_DK = 128
CONV_W = 4
RWKV_HEADS = 16
RWKV_N = 64
RWKV_GN_EPS = 64e-5
LORA_COLS = 256
N_EXPERTS = 32
TOP_K = 4
SWIGLU_ALPHA = 1.702
SWIGLU_LIMIT = 7.0

MAIN_COLS = 9216
AUX_COLS = 384
HEADS_PER_GROUP = 4
PACK_W = HEADS_PER_GROUP * CHUNK
INV_LEAF = 16
GDN_CHUNKS_PER_ITER = 4
RWKV_CHUNKS_PER_ITER = 4

SC_GATHER_ROWS = 64
SC_INDEX_CHUNK = 2048
EXPERT_BLOCK = 512
VMEM_LIMIT = 48 * 1024 * 1024


def _mm(a, b):
    return jnp.dot(a.astype(BF16), b.astype(BF16), preferred_element_type=F32)


def _mm_nt(a, b):
    return lax.dot_general(a.astype(BF16), b.astype(BF16), (((1,), (1,)), ((), ())),
                           preferred_element_type=F32)


def _mm_tn(a, b):
    return lax.dot_general(a.astype(BF16), b.astype(BF16), (((0,), (0,)), ((), ())),
                           preferred_element_type=F32)


def _split_bf16(x, terms):
    parts = []
    for _ in range(terms - 1):
        hi = x.astype(BF16)
        parts.append(hi)
        x = x - hi.astype(F32)
    parts.append(x.astype(BF16))
    return parts


def _mm_exact_lhs(a_bf16, b):
    out = None
    for part in _split_bf16(b, 3):
        d = jnp.dot(a_bf16, part, preferred_element_type=F32)
        out = d if out is None else out + d
    return out


def _pack_bf16_pairs(x):
    half = x.shape[1] // 2
    lo = lax.bitcast_convert_type(x[:, :half].astype(BF16).astype(F32), U32) >> 16
    hi = lax.bitcast_convert_type(x[:, half:].astype(BF16).astype(F32), U32) & jnp.uint32(0xFFFF0000)
    return lo | hi


def _unpack_bf16_pairs(w):
    lo = lax.bitcast_convert_type(w << 16, F32)
    hi = lax.bitcast_convert_type(w & jnp.uint32(0xFFFF0000), F32)
    return jnp.concatenate([lo, hi], axis=1)


def _iota(shape, dim):
    return lax.broadcasted_iota(I32, shape, dim)


def _softplus(x):
    return jnp.maximum(x, 0.0) + jnp.log(1.0 + jnp.exp(-jnp.abs(x)))


def _stack_rows(x, n):
    return jnp.concatenate([x] * n, axis=0)


def _tri_inv_packed(ms, eye, bdmask):
    def bd(x):
        return jnp.where(bdmask, _stack_rows(x, HEADS_PER_GROUP), jnp.zeros((), BF16))

    def mul(a, b):
        ah, al = _split_bf16(a, 2)
        bh, bl = _split_bf16(b, 2)
        bdh = bd(bh)
        return (jnp.dot(ah, bdh, preferred_element_type=F32) + jnp.dot(al, bdh, preferred_element_type=F32)
                + jnp.dot(ah, bd(bl), preferred_element_type=F32))

    lane = _iota((CHUNK, PACK_W), 1) % CHUNK
    leaf = (lane // INV_LEAF) == (_iota((CHUNK, PACK_W), 0) // INV_LEAF)
    ds = [jnp.where(leaf, m, 0.0) for m in ms]
    ls = [m - d for m, d in zip(ms, ds)]
    ts = [eye - d for d in ds]
    xs = [mul(d, d) for d in ds]
    for _ in range(2):
        rs = [mul(jnp.concatenate([t, x], axis=0), x) for t, x in zip(ts, xs)]
        ts = [t + r[:CHUNK] for t, r in zip(ts, rs)]
        xs = [r[CHUNK:] for r in rs]
    bs = [t + mul(t, x) for t, x in zip(ts, xs)]
    ns = [mul(b, l) for b, l in zip(bs, ls)]
    n2 = [mul(n, n) for n in ns]
    ps = [(eye - n) + mul(eye - n, q) for n, q in zip(ns, n2)]
    return [mul(p, b) for p, b in zip(ps, bs)]


def _shift_rows(x, prev8, k, row8):
    r = pltpu.roll(x, k, 0)
    pr = pltpu.roll(prev8, k, 0)
    head = jnp.where(row8 < k, pr, r[:8])
    return jnp.concatenate([head, r[8:]], axis=0)


def _in_proj_kernel(x_ref, nw_ref, w_ref, wa_ref, o_ref, oa_ref, h_ref):
    @pl.when(pl.program_id(1) == 0)
    def _():
        x = x_ref[...]
        y = x * lax.rsqrt(jnp.mean(x * x, axis=-1, keepdims=True) + EPS)
        h_ref[...] = (y * nw_ref[...]).astype(BF16)
        oa_ref[...] = jnp.dot(h_ref[...], wa_ref[...], preferred_element_type=F32)

    o_ref[...] = jnp.dot(h_ref[...], w_ref[...], preferred_element_type=F32).astype(BF16)


def _in_proj(x2, norm_w, w_main, w_aux):
    t = x2.shape[0]
    tm = min(1024, t)
    tn = 2304
    return pl.pallas_call(
        _in_proj_kernel,
        grid=(t // tm, MAIN_COLS // tn),
        in_specs=[pl.BlockSpec((tm, D_MODEL), lambda i, j: (i, 0)),
                  pl.BlockSpec((1, D_MODEL), lambda i, j: (0, 0)),
                  pl.BlockSpec((D_MODEL, tn), lambda i, j: (0, j)),
                  pl.BlockSpec((D_MODEL, AUX_COLS), lambda i, j: (0, 0))],
        out_specs=[pl.BlockSpec((tm, tn), lambda i, j: (i, j)),
                   pl.BlockSpec((tm, AUX_COLS), lambda i, j: (i, 0))],
        out_shape=[jax.ShapeDtypeStruct((t, MAIN_COLS), BF16), jax.ShapeDtypeStruct((t, AUX_COLS), F32)],
        scratch_shapes=[pltpu.VMEM((tm, D_MODEL), BF16)],
        compiler_params=pltpu.CompilerParams(dimension_semantics=("parallel", "arbitrary"),
                                             vmem_limit_bytes=VMEM_LIMIT),
        name="in_proj",
    )(x2, norm_w.reshape(1, D_MODEL), w_main, w_aux)


def _gdn_kernel(q_ref, k_ref, v_ref, z_ref, ba_ref, conv_ref, alog_ref, dtb_ref, nw_ref, o_ref,
                carry_ref, qn_ref, kn_ref, vv_ref, beta_ref, g_ref, s_ref, *, ts):
    @pl.when(pl.program_id(1) == 0)
    def _():
        carry_ref[...] = jnp.zeros_like(carry_ref)
        s_ref[...] = jnp.zeros_like(s_ref)

    row8 = _iota((8, D_MODEL), 0)

    def conv_silu(idx, x_ref):
        x = x_ref[...].astype(F32)
        prev8 = carry_ref[idx]
        w4 = conv_ref[:, idx * D_MODEL:(idx + 1) * D_MODEL]
        y = x * w4[CONV_W - 1:CONV_W]
        for k in range(1, CONV_W):
            y = y + _shift_rows(x, prev8, k, row8) * w4[CONV_W - 1 - k:CONV_W - k]
        carry_ref[idx] = x[ts - 8:ts]
        return y * jax.nn.sigmoid(y)

    def l2norm_heads(x, scale):
        parts = []
        for h in range(GDN_HEADS):
            xh = x[:, h * GDN_DK:(h + 1) * GDN_DK]
            parts.append(xh * (lax.rsqrt(jnp.sum(xh * xh, axis=-1, keepdims=True) + EPS) * scale))
        return jnp.concatenate(parts, axis=1)

    qn_ref[...] = l2norm_heads(conv_silu(0, q_ref), GDN_DK ** -0.5)
    kn_ref[...] = l2norm_heads(conv_silu(1, k_ref), 1.0)
    vv_ref[...] = conv_silu(2, v_ref)
    ba = ba_ref[...]
    beta_ref[...] = jax.nn.sigmoid(ba)
    g_ref[...] = -jnp.exp(alog_ref[...]) * _softplus(ba + dtb_ref[...])

    gw = HEADS_PER_GROUP * GDN_DK
    lane_p = _iota((CHUNK, PACK_W), 1)
    row_p = _iota((CHUNK, PACK_W), 0)
    eye = jnp.where((lane_p % CHUNK) == row_p, 1.0, 0.0).astype(F32)
    strict = (lane_p % CHUNK) < row_p
    incl = (lane_p % CHUNK) <= row_p
    bd_pp = (_iota((PACK_W, PACK_W), 0) // CHUNK) == (_iota((PACK_W, PACK_W), 1) // CHUNK)
    bd_pn = (_iota((PACK_W, gw), 0) // CHUNK) == (_iota((PACK_W, gw), 1) // GDN_DK)
    bd_pn2 = (_iota((PACK_W, 2 * gw), 0) // CHUNK) == ((_iota((PACK_W, 2 * gw), 1) % gw) // GDN_DK)
    ltri = jnp.where(_iota((CHUNK, CHUNK), 1) <= _iota((CHUNK, CHUNK), 0), 1.0, 0.0).astype(BF16)
    ones_cc = jnp.ones((CHUNK, CHUNK), BF16)
    lane128 = _iota((CHUNK, 128), 1)
    nw = nw_ref[...]

    n_groups = GDN_HEADS // HEADS_PER_GROUP
    hcols = [slice(j * GDN_DK, (j + 1) * GDN_DK) for j in range(HEADS_PER_GROUP)]

    def intra_chunk(rows_list):
        units = [(i, gi) for i in range(len(rows_list)) for gi in range(n_groups)]
        gcs = [_mm_exact_lhs(ltri, g_ref[rows, :]) for rows in rows_list]
        beta = [beta_ref[rows, :] for rows in rows_list]
        pre = []
        for i, gi in units:
            rows = rows_list[i]
            heads = [gi * HEADS_PER_GROUP + j for j in range(HEADS_PER_GROUP)]
            cols = slice(gi * gw, (gi + 1) * gw)
            beta_nat = jnp.concatenate(
                [jnp.broadcast_to(beta[i][:, h:h + 1], (CHUNK, GDN_DK)) for h in heads], axis=1)
            gcol = [jnp.broadcast_to(gcs[i][:, 8 + h:9 + h], (CHUNK, GDN_DK)) for h in heads]
            gcol_p = jnp.concatenate([jnp.where(lane128 < CHUNK, gcol[0], gcol[1]),
                                      jnp.where(lane128 < CHUNK, gcol[2], gcol[3])], axis=1)
            pre.append(dict(qn=qn_ref[rows, cols], kn=kn_ref[rows, cols], vv=vv_ref[rows, cols],
                            beta_nat=beta_nat, gc_nat=jnp.concatenate(gcol, axis=1), gcol_p=gcol_p))
        grow = [_mm_exact_lhs(ones_cc, u["gcol_p"] * eye) for u in pre]
        aas = [_mm_nt(jnp.concatenate([u["kn"] * u["beta_nat"], u["qn"]], axis=0),
                      jnp.where(bd_pn, _stack_rows(u["kn"], HEADS_PER_GROUP), 0.0)) for u in pre]
        decay = [jnp.exp(jnp.minimum(u["gcol_p"] - gr, 0.0)) for u, gr in zip(pre, grow)]
        t_inv = _tri_inv_packed([jnp.where(strict, aa[:CHUNK] * d, 0.0) for aa, d in zip(aas, decay)],
                                eye, bd_pp)
        out = []
        for u, aa, d, ti in zip(pre, aas, decay, t_inv):
            egc = jnp.exp(u["gc_nat"])
            data = jnp.concatenate([u["vv"] * u["beta_nat"], u["kn"] * u["beta_nat"] * egc], axis=1)
            uw = _mm(ti, jnp.where(bd_pn2, _stack_rows(data, HEADS_PER_GROUP), 0.0))
            gl = u["gc_nat"][CHUNK - 1:CHUNK, :]
            out.append(dict(uw=uw, attn=jnp.where(incl, aa[CHUNK:] * d, 0.0), qd=u["qn"] * egc,
                            kd=u["kn"] * jnp.exp(gl - u["gc_nat"]), egl=jnp.exp(gl)))
        return out

    def recurrence(rows, group_units):
        wq = {}
        for gi, u in enumerate(group_units):
            for j, hc in enumerate(hcols):
                h = gi * HEADS_PER_GROUP + j
                wq[h] = _mm(jnp.concatenate([u["uw"][:, gw + j * GDN_DK:gw + (j + 1) * GDN_DK],
                                             u["qd"][:, hc]], axis=0), s_ref[h])
        vnew = {}
        for gi, u in enumerate(group_units):
            for j, hc in enumerate(hcols):
                h = gi * HEADS_PER_GROUP + j
                vnew[h] = u["uw"][:, hc] - wq[h][:CHUNK]
                s_ref[h] = s_ref[h] * u["egl"][:, hc] + _mm_tn(u["kd"][:, hc], vnew[h])
        for gi, u in enumerate(group_units):
            heads = [gi * HEADS_PER_GROUP + j for j in range(HEADS_PER_GROUP)]
            vn = jnp.concatenate([vnew[h] for h in heads], axis=1)
            o = jnp.concatenate([wq[h][CHUNK:] for h in heads], axis=1) + _mm(
                u["attn"], jnp.where(bd_pn, _stack_rows(vn, HEADS_PER_GROUP), 0.0))
            for j, h in enumerate(heads):
                oh = o[:, hcols[j]]
                oh = oh * lax.rsqrt(jnp.mean(oh * oh, axis=-1, keepdims=True) + EPS) * nw
                zh = z_ref[rows, h * GDN_DK:(h + 1) * GDN_DK].astype(F32)
                o_ref[rows, h * GDN_DK:(h + 1) * GDN_DK] = (oh * (zh * jax.nn.sigmoid(zh))).astype(BF16)

    def chunk_pair_body(c, carry):
        rows_list = [pl.ds(pl.multiple_of((c * GDN_CHUNKS_PER_ITER + i) * CHUNK, CHUNK), CHUNK)
                     for i in range(GDN_CHUNKS_PER_ITER)]
        units = intra_chunk(rows_list)
        for i, rows in enumerate(rows_list):
            recurrence(rows, units[i * n_groups:(i + 1) * n_groups])
        return carry

    lax.fori_loop(0, ts // (CHUNK * GDN_CHUNKS_PER_ITER), chunk_pair_body, 0)


def _gdn(p, p_aux, conv_w, a_log, dt_bias, norm_w, bsz, seq):
    ts = min(256, seq)
    ns = seq // ts
    pad8 = lambda v: jnp.zeros((1, 128), F32).at[0, 8:16].set(v)
    tok = lambda col: pl.BlockSpec((ts, D_MODEL), lambda b, s, col=col: (b * ns + s, col))
    const = lambda shape: pl.BlockSpec(shape, lambda b, s: (0,) * len(shape))
    return pl.pallas_call(
        functools.partial(_gdn_kernel, ts=ts),
        grid=(bsz, ns),
        in_specs=[tok(0), tok(1), tok(2), tok(3),
                  pl.BlockSpec((ts, 128), lambda b, s: (b * ns + s, LORA_COLS // 128)),
                  const((CONV_W, 3 * D_MODEL)), const((1, 128)), const((1, 128)), const((1, GDN_DK))],
        out_specs=pl.BlockSpec((ts, D_MODEL), lambda b, s: (b * ns + s, 0)),
        out_shape=jax.ShapeDtypeStruct((bsz * seq, D_MODEL), BF16),
        scratch_shapes=[pltpu.VMEM((3, 8, D_MODEL), F32),
                        pltpu.VMEM((ts, D_MODEL), F32), pltpu.VMEM((ts, D_MODEL), F32),
                        pltpu.VMEM((ts, D_MODEL), F32),
                        pltpu.VMEM((ts, 128), F32), pltpu.VMEM((ts, 128), F32),
                        pltpu.VMEM((GDN_HEADS, GDN_DK, GDN_DK), F32)],
        compiler_params=pltpu.CompilerParams(dimension_semantics=("parallel", "arbitrary"),
                                             vmem_limit_bytes=VMEM_LIMIT),
        name="gdn",
    )(p, p, p, p, p_aux, conv_w, pad8(a_log), pad8(dt_bias), norm_w.reshape(1, GDN_DK))


def _rwkv_kernel(r_ref, k_ref, v_ref, lo_ref, mu_ref, w0_ref, w2_ref, a0_ref, a2_ref, g2_ref,
                 kk_ref, ka_ref, rk_ref, lnw_ref, lnb_ref, seg_ref, o_ref,
                 carry_ref, rs_ref, ks_ref, vs_ref, kks_ref, bs_ref, ld_ref, gg_ref, os_ref, st_ref,
                 *, ts):
    @pl.when(pl.program_id(1) == 0)
    def _():
        carry_ref[...] = jnp.zeros_like(carry_ref)
        st_ref[...] = jnp.zeros_like(st_ref)

    width = D_MODEL
    row8 = _iota((8, width), 0)
    row8l = _iota((8, LORA_COLS), 0)

    def seg_sum(x):
        xb = x.astype(BF16)
        return jnp.concatenate(
            [jnp.dot(xb[:, g * PACK_W:(g + 1) * PACK_W], seg_ref[...], preferred_element_type=F32)
             for g in range(width // PACK_W)], axis=1)

    def lerp(idx, x_ref, mu, r8):
        x = x_ref[...].astype(F32)
        xs = x + (_shift_rows(x, carry_ref[idx, :, :x.shape[1]], 1, r8) - x) * mu
        carry_ref[idx, :, :x.shape[1]] = x[ts - 8:ts]
        return xs

    r = lerp(0, r_ref, mu_ref[:, 0:width], row8)
    k = lerp(1, k_ref, mu_ref[:, width:2 * width], row8)
    v = lerp(2, v_ref, mu_ref[:, 2 * width:3 * width], row8)
    lo = lerp(3, lo_ref, mu_ref[:, 3 * width:3 * width + LORA_COLS], row8l)
    lo_a = lo[:, :128]
    w_log = -_softplus(-(w0_ref[...] + _mm(jnp.tanh(lo_a), w2_ref[...]))) - 0.5
    ld_ref[...] = -jnp.exp(w_log)
    aa = jax.nn.sigmoid(a0_ref[...] + _mm(lo_a, a2_ref[...]))
    gg_ref[...] = _mm(jax.nn.sigmoid(lo[:, 128:]), g2_ref[...])
    kx = k * kk_ref[...]
    kk = kx * lax.rsqrt(seg_sum(kx * kx) + EPS)
    k = k * (1.0 + (aa - 1.0) * ka_ref[...])
    rs_ref[...] = r
    ks_ref[...] = k
    vs_ref[...] = v
    kks_ref[...] = kk
    bs_ref[...] = kk * aa

    lane_p = _iota((CHUNK, PACK_W), 1)
    row_p = _iota((CHUNK, PACK_W), 0)
    eye = jnp.where((lane_p % CHUNK) == row_p, 1.0, 0.0).astype(F32)
    strict = (lane_p % CHUNK) < row_p
    incl = (lane_p % CHUNK) <= row_p
    bd = (_iota((PACK_W, PACK_W), 0) // CHUNK) == (_iota((PACK_W, PACK_W), 1) // CHUNK)
    bd2 = jnp.concatenate([bd, bd], axis=1)
    ltri = jnp.where(_iota((CHUNK, CHUNK), 1) <= _iota((CHUNK, CHUNK), 0), 1.0, 0.0).astype(BF16)

    def bdiag(x):
        return jnp.where(bd, _stack_rows(x, HEADS_PER_GROUP), 0.0)

    n_groups = width // PACK_W
    gcols = [slice(g * PACK_W, (g + 1) * PACK_W) for g in range(n_groups)]

    def intra_chunk(rows_list):
        units = [(rows, cols) for rows in rows_list for cols in gcols]
        ld = [ld_ref[rows, cols] for rows, cols in units]
        cs = [_mm_exact_lhs(ltri, x) for x in ld]
        pre = []
        for g, (rows, cols) in enumerate(units):
            cl = cs[g][CHUNK - 1:CHUNK, :]
            e_neg = jnp.exp(-cs[g])
            e_dec = jnp.exp(cl - cs[g])
            kc, bc = ks_ref[rows, cols], bs_ref[rows, cols]
            rt = rs_ref[rows, cols] * jnp.exp(cs[g])
            kt = kks_ref[rows, cols] * jnp.exp(cs[g] - ld[g])
            pre.append(dict(cl=cl, rt=rt, kt=kt, vc=vs_ref[rows, cols], kh=kc * e_neg, bh=bc * e_neg,
                            kdec=kc * e_dec, bdec=bc * e_dec))
        akb = [_mm_nt(jnp.concatenate([u["kt"], u["rt"]], axis=0),
                      jnp.concatenate([bdiag(u["kh"]), bdiag(u["bh"])], axis=0)) for u in pre]
        t_inv = _tri_inv_packed([jnp.where(strict, a[:CHUNK, PACK_W:], 0.0) for a in akb], eye, bd)
        xo = [_mm(jnp.concatenate([jnp.where(strict, a[:CHUNK, :PACK_W], 0.0),
                                   jnp.where(incl, a[CHUNK:, :PACK_W], 0.0)], axis=0), bdiag(u["vc"]))
              for a, u in zip(akb, pre)]
        tt = [_mm(ti, jnp.where(bd2, _stack_rows(jnp.concatenate([x[:CHUNK], u["kt"]], axis=1),
                                                 HEADS_PER_GROUP), 0.0))
              for ti, x, u in zip(t_inv, xo, pre)]
        rr = [_mm(jnp.where(incl, a[CHUNK:, PACK_W:], 0.0),
                  jnp.where(bd2, _stack_rows(jnp.concatenate([t[:, PACK_W:], t[:, :PACK_W]], axis=1),
                                             HEADS_PER_GROUP), 0.0))
              for a, t in zip(akb, tt)]
        return [dict(u, tk=t[:, PACK_W:], u0=t[:, :PACK_W], rq=u["rt"] - r[:, :PACK_W],
                     o1=x[CHUNK:] - r[:, PACK_W:]) for u, x, t, r in zip(pre, xo, tt, rr)]

    def recurrence(rows, group_units):
        uo = [_mm_nt(jnp.concatenate([u["tk"], u["rq"]], axis=0), st_ref[g])
              for g, u in enumerate(group_units)]
        upd = []
        for g, (cols, u) in enumerate(zip(gcols, group_units)):
            os_ref[rows, cols] = u["o1"] + uo[g][CHUNK:]
            upd.append(_mm_tn(jnp.concatenate([u["vc"], u["u0"] + uo[g][:CHUNK]], axis=0),
                              jnp.concatenate([u["kdec"], -u["bdec"]], axis=0)))
        for g, u in enumerate(group_units):
            st_ref[g] = st_ref[g] * jnp.exp(u["cl"]) + jnp.where(bd, upd[g], 0.0)

    def chunks_body(c, carry):
        rows_list = [pl.ds(pl.multiple_of((c * RWKV_CHUNKS_PER_ITER + i) * CHUNK, CHUNK), CHUNK)
                     for i in range(RWKV_CHUNKS_PER_ITER)]
        units = intra_chunk(rows_list)
        for i, rows in enumerate(rows_list):
            recurrence(rows, units[i * n_groups:(i + 1) * n_groups])
        return carry

    lax.fori_loop(0, ts // (CHUNK * RWKV_CHUNKS_PER_ITER), chunks_body, 0)

    o = os_ref[...]
    inv_n = 1.0 / RWKV_N
    mean = seg_sum(o) * inv_n
    cen = o - mean
    var = seg_sum(cen * cen) * inv_n
    o = cen * lax.rsqrt(var + RWKV_GN_EPS) * lnw_ref[...] + lnb_ref[...]
    bonus = seg_sum(rs_ref[...] * ks_ref[...] * rk_ref[...]) * vs_ref[...]
    o_ref[...] = ((o + bonus) * gg_ref[...]).astype(BF16)


def _rwkv(p, p_aux, mu, w0, w2, a0, a2, g2, k_k, k_a, r_k, ln_w, ln_b, bsz, seq):
    ts = min(256, seq)
    ns = seq // ts
    width = D_MODEL
    row = lambda v: v.reshape(1, -1)
    w2p = jnp.concatenate([w2, jnp.zeros_like(w2)], axis=0).astype(BF16)
    a2p = jnp.concatenate([jnp.zeros_like(a2), a2], axis=0).astype(BF16)
    seg = (np.arange(PACK_W)[:, None] // RWKV_N == np.arange(PACK_W)[None, :] // RWKV_N)
    seg = jnp.asarray(seg, BF16)
    tok = lambda col: pl.BlockSpec((ts, width), lambda b, s, col=col: (b * ns + s, col))
    const = lambda shape: pl.BlockSpec(shape, lambda b, s: (0,) * len(shape))
    fbuf = lambda: pltpu.VMEM((ts, width), F32)
    return pl.pallas_call(
        functools.partial(_rwkv_kernel, ts=ts),
        grid=(bsz, ns),
        in_specs=[tok(4), tok(5), tok(6),
                  pl.BlockSpec((ts, LORA_COLS), lambda b, s: (b * ns + s, 0)),
                  const((1, 3 * width + LORA_COLS)), const((1, width)), const((128, width)),
                  const((1, width)), const((128, width)), const((128, width)),
                  const((1, width)), const((1, width)), const((1, width)), const((1, width)),
                  const((1, width)), const((PACK_W, PACK_W))],
        out_specs=pl.BlockSpec((ts, width), lambda b, s: (b * ns + s, 0)),
        out_shape=jax.ShapeDtypeStruct((bsz * seq, width), BF16),
        scratch_shapes=[pltpu.VMEM((4, 8, width), F32),
                        fbuf(), fbuf(), fbuf(), fbuf(), fbuf(), fbuf(), fbuf(), fbuf(),
                        pltpu.VMEM((width // PACK_W, PACK_W, PACK_W), F32)],
        compiler_params=pltpu.CompilerParams(dimension_semantics=("parallel", "arbitrary"),
                                             vmem_limit_bytes=VMEM_LIMIT),
        name="rwkv",
    )(p, p, p, p_aux, row(mu), row(w0), w2p, row(a0), a2p, g2.astype(BF16), row(k_k), row(k_a),
      row(r_k), row(ln_w), row(ln_b), seg)


def _merge_kernel(x_ref, ya_ref, yb_ref, ga_ref, gb_ref, pa_ref, pb_ref, wo_ref, nw_ref, rw_ref, rb_ref,
                  x1_ref, hn_ref, lg_ref):
    merged = (jax.nn.sigmoid(ga_ref[...].astype(F32)) * jnp.dot(ya_ref[...], pa_ref[...], preferred_element_type=F32)
              + jax.nn.sigmoid(gb_ref[...].astype(F32)) * jnp.dot(yb_ref[...], pb_ref[...], preferred_element_type=F32))
    x1 = x_ref[...] + _mm(merged, wo_ref[...])
    x1_ref[...] = x1
    hn = x1 * lax.rsqrt(jnp.mean(x1 * x1, axis=-1, keepdims=True) + EPS) * nw_ref[...]
    hn_ref[...] = _pack_bf16_pairs(hn)
    lg_ref[...] = lax.dot_general(rw_ref[...], hn, (((1,), (1,)), ((), ())),
                                  preferred_element_type=F32, precision=HIGHEST) + rb_ref[...]


def _merge(x2, ya, yb, p, proj_a, proj_b, w_out, norm_w, router_w, router_b):
    t = x2.shape[0]
    tm = min(512, t)
    tok = lambda col: pl.BlockSpec((tm, D_MODEL), lambda i, col=col: (i, col))
    const = lambda shape: pl.BlockSpec(shape, lambda i: (0,) * len(shape))
    return pl.pallas_call(
        _merge_kernel,
        grid=(t // tm,),
        in_specs=[tok(0), tok(0), tok(0), tok(7), tok(8),
                  const((D_MODEL, D_MODEL)), const((D_MODEL, D_MODEL)), const((D_MODEL, D_MODEL)),
                  const((1, D_MODEL)), const((N_EXPERTS, D_MODEL)), const((N_EXPERTS, 1))],
        out_specs=[tok(0), pl.BlockSpec((tm, D_MODEL // 2), lambda i: (i, 0)),
                   pl.BlockSpec((N_EXPERTS, tm), lambda i: (0, i))],
        out_shape=[jax.ShapeDtypeStruct((t, D_MODEL), F32), jax.ShapeDtypeStruct((t, D_MODEL // 2), U32),
                   jax.ShapeDtypeStruct((N_EXPERTS, t), F32)],
        compiler_params=pltpu.CompilerParams(dimension_semantics=("parallel",),
                                             vmem_limit_bytes=VMEM_LIMIT),
        name="merge",
    )(x2, ya, yb, p, p, proj_a.astype(BF16), proj_b.astype(BF16), w_out.astype(BF16),
      norm_w.reshape(1, D_MODEL), router_w.T, router_b.reshape(N_EXPERTS, 1))


def _route_kernel(lg_ref, eidx_ref, gate_ref, rank_ref, base_ref, cnt_ref, carry_ref, *, tt):
    @pl.when(pl.program_id(0) == 0)
    def _():
        carry_ref[...] = jnp.zeros_like(carry_ref)

    l = lg_ref[...]
    ie = _iota((N_EXPERTS, tt), 0)
    vals, hots, idxs = [], [], []
    for _ in range(TOP_K):
        m = jnp.max(l, axis=0, keepdims=True)
        idx = jnp.min(jnp.where(l == m, ie, N_EXPERTS), axis=0, keepdims=True)
        hot = ie == idx
        vals.append(m)
        hots.append(hot)
        idxs.append(idx)
        l = jnp.where(hot, -jnp.inf, l)
    exps = [jnp.exp(v - vals[0]) for v in vals]
    den = exps[0] + exps[1] + exps[2] + exps[3]
    gate_ref[...] = jnp.concatenate([e / den for e in exps], axis=0)
    eidx_ref[...] = jnp.concatenate(idxs, axis=0)

    sel = jnp.zeros((N_EXPERTS, tt), F32)
    for hot in hots:
        sel = sel + jnp.where(hot, 1.0, 0.0)
    before = jnp.where(_iota((tt, tt), 0) < _iota((tt, tt), 1), 1.0, 0.0).astype(BF16)
    carry = carry_ref[...]
    prefix = jnp.dot(sel.astype(BF16), before, preferred_element_type=F32) + carry[:, 0:1]
    rank_ref[...] = jnp.concatenate(
        [jnp.sum(jnp.where(hot, prefix, 0.0), axis=0, keepdims=True) for hot in hots], axis=0).astype(I32)
    cnt = jnp.broadcast_to(jnp.sum(sel, axis=1, keepdims=True), (N_EXPERTS, 128))
    base_ref[0] = carry
    cnt_ref[0] = cnt
    carry_ref[...] = carry + cnt


def _route(logits_t, tt):
    t = logits_t.shape[1]
    nt = t // tt
    row4 = pl.BlockSpec((TOP_K, tt), lambda i: (0, i))
    per_tile = pl.BlockSpec((1, N_EXPERTS, 128), lambda i: (i, 0, 0))
    return pl.pallas_call(
        functools.partial(_route_kernel, tt=tt),
        grid=(nt,),
        in_specs=[pl.BlockSpec((N_EXPERTS, tt), lambda i: (0, i))],
        out_specs=[row4, row4, row4, per_tile, per_tile],
        out_shape=[jax.ShapeDtypeStruct((TOP_K, t), I32), jax.ShapeDtypeStruct((TOP_K, t), F32),
                   jax.ShapeDtypeStruct((TOP_K, t), I32),
                   jax.ShapeDtypeStruct((nt, N_EXPERTS, 128), F32),
                   jax.ShapeDtypeStruct((nt, N_EXPERTS, 128), F32)],
        scratch_shapes=[pltpu.VMEM((N_EXPERTS, 128), F32)],
        compiler_params=pltpu.CompilerParams(dimension_semantics=("arbitrary",)),
        name="route",
    )(logits_t)


def _count_le(sorted_vals, queries):
    return jnp.sum((sorted_vals[None, :] <= queries[:, None]).astype(I32), axis=1)


def _routing_plan(cnt, eidx, rank, tt):
    t = cnt.shape[0] * tt
    n_mb = (t * TOP_K) // EXPERT_BLOCK + N_EXPERTS
    counts = jnp.sum(cnt, axis=0)
    padded = ((counts + EXPERT_BLOCK - 1) // EXPERT_BLOCK) * EXPERT_BLOCK
    end_pad = jnp.cumsum(padded)
    start_pad = end_pad - padded
    hot = eidx[:, :, None] == jnp.arange(N_EXPERTS, dtype=I32)[None, None, :]
    dest = jnp.sum(jnp.where(hot, start_pad[None, None, :], 0), axis=-1) + rank

    mb_start = jnp.arange(n_mb, dtype=I32) * EXPERT_BLOCK
    mb_expert = jnp.minimum(_count_le(end_pad, mb_start), N_EXPERTS - 1).astype(I32)
    mb_active = (mb_start < end_pad[-1]).astype(I32)

    dest = dest.astype(I32)
    return dest, _sc_slot_tokens(dest, n_mb * EXPERT_BLOCK), mb_expert, mb_active, n_mb


def _sc_slot_tokens(dest, n_slots):
    info = plsc.get_sparse_core_info()
    nc, ns, nl = info.num_cores, info.num_subcores, info.num_lanes
    top_k, t = dest.shape
    per_worker = n_slots // (nc * ns)
    assert per_worker * nc * ns == n_slots and per_worker % nl == 0 and t % SC_INDEX_CHUNK == 0
    fill_mask = (1 << (t.bit_length() - 1)) - 1
    mesh = plsc.VectorSubcoreMesh(core_axis_name="c", subcore_axis_name="s")

    @functools.partial(
        pl.kernel, mesh=mesh,
        out_type=jax.ShapeDtypeStruct((n_slots,), I32),
        scratch_types=[pltpu.VMEM((per_worker,), I32), pltpu.VMEM((SC_INDEX_CHUNK,), I32)],
        compiler_params=pltpu.CompilerParams(needs_layout_passes=False),
    )
    def invert(dest_hbm, out_hbm, local_v, chunk_v):
        lo = (lax.axis_index("s") * nc + lax.axis_index("c")) * per_worker
        lane = lax.iota(I32, nl)

        @pl.loop(0, per_worker // nl)
        def _(j):
            local_v[pl.ds(j * nl, nl)] = (lo + j * nl + lane) & fill_mask

        for k in range(top_k):
            @pl.loop(0, t // SC_INDEX_CHUNK)
            def _(c):
                pltpu.sync_copy(dest_hbm.at[k, pl.ds(c * SC_INDEX_CHUNK, SC_INDEX_CHUNK)], chunk_v)

                @pl.loop(0, SC_INDEX_CHUNK // nl)
                def _(j):
                    d = chunk_v[pl.ds(j * nl, nl)] - lo
                    tok = c * SC_INDEX_CHUNK + j * nl + lane
                    plsc.store_scatter(local_v, [d], tok, mask=(d >= 0) & (d < per_worker))

        pltpu.sync_copy(local_v, out_hbm.at[pl.ds(lo, per_worker)])

    return invert(dest)


def _sc_gather(table, idx):
    info = plsc.get_sparse_core_info()
    nc, ns = info.num_cores, info.num_subcores
    n_rows, width = idx.shape[0], table.shape[1]
    per_worker = n_rows // (nc * ns)
    steps = per_worker // SC_GATHER_ROWS
    assert per_worker * nc * ns == n_rows and steps * SC_GATHER_ROWS == per_worker
    mesh = plsc.VectorSubcoreMesh(core_axis_name="c", subcore_axis_name="s")

    @functools.partial(
        pl.kernel, mesh=mesh,
        out_type=jax.ShapeDtypeStruct((n_rows, width), table.dtype),
        scratch_types=[pltpu.VMEM((SC_GATHER_ROWS,), I32),
                       pltpu.VMEM((SC_GATHER_ROWS, width), table.dtype),
                       pltpu.SemaphoreType.DMA],
    )
    def gather(table_hbm, idx_hbm, out_hbm, idx_v, rows_v, sem):
        base = (lax.axis_index("s") * nc + lax.axis_index("c")) * per_worker

        @pl.loop(0, steps)
        def _(i):
            off = pl.multiple_of(base + i * SC_GATHER_ROWS, 8)
            pltpu.sync_copy(idx_hbm.at[pl.ds(off, SC_GATHER_ROWS)], idx_v)
            pltpu.async_copy(table_hbm.at[idx_v], rows_v, sem).wait()
            pltpu.sync_copy(rows_v, out_hbm.at[pl.ds(off, SC_GATHER_ROWS)])

    return gather(table, idx)


def _expert_kernel(e_ref, act_ref, x_ref, wgu_ref, wd_ref, bg_ref, bl_ref, bd_ref, o_ref,
                   wg_c, wl_c, wd_c):
    mb = pl.program_id(0)
    new_expert = jnp.logical_or(mb == 0, e_ref[mb] != e_ref[jnp.maximum(mb - 1, 0)])

    @pl.when(jnp.logical_and(new_expert, act_ref[mb] == 1))
    def _():
        lane = _iota((D_MODEL, 128), 1)
        half = lane < 64
        idx = jnp.where(half, 2 * lane, 2 * (lane - 64) + 1)
        for m in range(D_MODEL // 128):
            a = jnp.take_along_axis(wgu_ref[0, :, (2 * m) * 128:(2 * m + 1) * 128], idx, axis=1)
            b = jnp.take_along_axis(wgu_ref[0, :, (2 * m + 1) * 128:(2 * m + 2) * 128], idx, axis=1)
            cols = slice(m * 128, (m + 1) * 128)
            wg_c[:, cols] = jnp.where(half, a, pltpu.roll(b, 64, 1)).astype(BF16)
            wl_c[:, cols] = jnp.where(half, pltpu.roll(a, 64, 1), b).astype(BF16)
        wd_c[...] = wd_ref[0].astype(BF16)

    @pl.when(act_ref[mb] == 0)
    def _():
        o_ref[...] = jnp.zeros_like(o_ref)

    @pl.when(act_ref[mb] == 1)
    def _():
        x = _unpack_bf16_pairs(x_ref[...]).astype(BF16)
        glu = jnp.dot(x, wg_c[...], preferred_element_type=F32) + bg_ref[0]
        lin = jnp.dot(x, wl_c[...], preferred_element_type=F32) + bl_ref[0]
        glu = jnp.minimum(glu, SWIGLU_LIMIT)
        lin = jnp.clip(lin, -SWIGLU_LIMIT, SWIGLU_LIMIT)
        act = glu * jax.nn.sigmoid(SWIGLU_ALPHA * glu) * (lin + 1.0)
        o_ref[...] = _pack_bf16_pairs(_mm(act, wd_c[...]) + bd_ref[0])


def _experts(xb, mb_expert, mb_active, w_gu, w_down, bg, bl, bd, n_mb):
    d_ff = w_down.shape[1]
    assert d_ff == D_MODEL and w_gu.shape[1:] == (D_MODEL, 2 * d_ff)
    bspec = pl.BlockSpec((1, 1, D_MODEL), lambda m, e, a: (e[m], 0, 0))
    xspec = pl.BlockSpec((EXPERT_BLOCK, D_MODEL // 2), lambda m, e, a: (m, 0))
    wcache = pltpu.VMEM((D_MODEL, D_MODEL), BF16)
    return pl.pallas_call(
        _expert_kernel,
        grid_spec=pltpu.PrefetchScalarGridSpec(
            num_scalar_prefetch=2,
            grid=(n_mb,),
            in_specs=[xspec,
                      pl.BlockSpec((1, D_MODEL, 2 * d_ff), lambda m, e, a: (e[m], 0, 0)),
                      pl.BlockSpec((1, d_ff, D_MODEL), lambda m, e, a: (e[m], 0, 0)),
                      bspec, bspec, bspec],
            out_specs=xspec,
            scratch_shapes=[wcache, wcache, wcache]),
        out_shape=jax.ShapeDtypeStruct(xb.shape, U32),
        compiler_params=pltpu.CompilerParams(dimension_semantics=("arbitrary",),
                                             vmem_limit_bytes=VMEM_LIMIT),
        name="experts",
    )(mb_expert, mb_active, xb, w_gu, w_down, bg, bl, bd)


def _combine_kernel(y4_ref, gate_ref, x1_ref, nw_ref, o_ref):
    g = gate_ref[...]
    y = x1_ref[...]
    for k in range(TOP_K):
        y = y + g[:, k:k + 1] * _unpack_bf16_pairs(y4_ref[k])
    o_ref[...] = y * lax.rsqrt(jnp.mean(y * y, axis=-1, keepdims=True) + EPS) * nw_ref[...]


def _combine(y4, gate_t, x1, norm_w):
    t = x1.shape[0]
    tm = min(512, t)
    return pl.pallas_call(
        _combine_kernel,
        grid=(t // tm,),
        in_specs=[pl.BlockSpec((TOP_K, tm, D_MODEL // 2), lambda i: (0, i, 0)),
                  pl.BlockSpec((tm, TOP_K), lambda i: (i, 0)),
                  pl.BlockSpec((tm, D_MODEL), lambda i: (i, 0)),
                  pl.BlockSpec((1, D_MODEL), lambda i: (0, 0))],
        out_specs=pl.BlockSpec((tm, D_MODEL), lambda i: (i, 0)),
        out_shape=jax.ShapeDtypeStruct((t, D_MODEL), F32),
        compiler_params=pltpu.CompilerParams(dimension_semantics=("parallel",)),
        name="combine",
    )(y4, gate_t, x1, norm_w.reshape(1, D_MODEL))


def _moe(x1, hn, logits_t, w_gu, b_gu, w_down, b_down, norm_final):
    t = x1.shape[0]
    tt = min(512, t)
    eidx, gate, rank, base, cnt = _route(logits_t, tt)
    cnt = cnt[:, :, 0].astype(I32)
    dest, slot_tok, mb_expert, mb_active, n_mb = _routing_plan(cnt, eidx, rank, tt)
    xb = _sc_gather(hn, slot_tok)
    bg = b_gu[:, None, 0::2]
    bl = b_gu[:, None, 1::2]
    yb = _experts(xb, mb_expert, mb_active, w_gu, w_down, bg, bl, b_down[:, None, :], n_mb)
    y4 = _sc_gather(yb, dest.reshape(-1)).reshape(TOP_K, t, D_MODEL // 2)
    return _combine(y4, gate.T, x1, norm_final)


def kernel(x, norm_mix, w_in, gdn_conv, gdn_A_log, gdn_dt_bias, gdn_norm, rwkv_mu, rwkv_w0, rwkv_w2, rwkv_a0, rwkv_a2, rwkv_g2, rwkv_k_k, rwkv_k_a, rwkv_r_k, rwkv_ln_w, rwkv_ln_b, proj_a, proj_b, w_out, norm_ffn, router_w, router_b, w_gate_up, b_gate_up, w_down, b_down, norm_final):
    bsz, seq, d = x.shape
    depth = w_in.shape[0]
    x2 = x.reshape(bsz * seq, d)
    out = None
    for l in range(depth):
        w = w_in[l]
        w_main = jnp.concatenate([w[:, 0:4096], w[:, 4112:7184], w[:, 7440:9488]], axis=1).astype(BF16)
        w_aux = jnp.concatenate([w[:, 7184:7440], w[:, 4096:4112],
                                 jnp.zeros((d, AUX_COLS - LORA_COLS - 16), w.dtype)], axis=1).astype(BF16)
        p, p_aux = _in_proj(x2, norm_mix[l], w_main, w_aux)
        ya = _gdn(p, p_aux, gdn_conv[l], gdn_A_log[l], gdn_dt_bias[l], gdn_norm[l], bsz, seq)
        yb = _rwkv(p, p_aux, rwkv_mu[l], rwkv_w0[l], rwkv_w2[l], rwkv_a0[l], rwkv_a2[l], rwkv_g2[l],
                   rwkv_k_k[l], rwkv_k_a[l], rwkv_r_k[l], rwkv_ln_w[l], rwkv_ln_b[l], bsz, seq)
        x1, hn, logits_t = _merge(x2, ya, yb, p, proj_a[l], proj_b[l], w_out[l], norm_ffn[l],
                                  router_w[l], router_b[l])
        assert l == depth - 1, "only the final layer's residual is fused with the output norm"
        out = _moe(x1, hn, logits_t, w_gate_up[l], b_gate_up[l], w_down[l], b_down[l], norm_final)
    return out.reshape(bsz, seq, d)
```

```python
import functools

import jax
import jax.numpy as jnp
import numpy as np
from jax import lax
from jax.experimental import pallas as pl
from jax.experimental.pallas import tpu as pltpu
from jax.experimental.pallas import tpu_sc as plsc

F32 = jnp.float32
BF16 = jnp.bfloat16
I32 = jnp.int32
U32 = jnp.uint32
HIGHEST = lax.Precision.HIGHEST

D_MODEL = 1024
EPS = 1e-6
CHUNK = 64
GDN_HEADS = 8
GDN_DK = 128
CONV_W = 4
RWKV_HEADS = 16
RWKV_N = 64
RWKV_GN_EPS = 64e-5
LORA_COLS = 256
N_EXPERTS = 32
TOP_K = 4
SWIGLU_ALPHA = 1.702
SWIGLU_LIMIT = 7.0

MAIN_COLS = 9216
AUX_COLS = 384
HEADS_PER_GROUP = 4
PACK_W = HEADS_PER_GROUP * CHUNK
INV_LEAF = 16
MIXER_TILE = 256
GDN_CHUNKS_PER_ITER = 4
RWKV_CHUNKS_PER_ITER = 4

SC_GATHER_ROWS = 64
SC_INDEX_CHUNK = 2048
EXPERT_BLOCK = 512
VMEM_LIMIT = 48 * 1024 * 1024


def _mm(a, b):
    return jnp.dot(a.astype(BF16), b.astype(BF16), preferred_element_type=F32)


def _mm_nt(a, b):
    return lax.dot_general(a.astype(BF16), b.astype(BF16), (((1,), (1,)), ((), ())),
                           preferred_element_type=F32)


def _mm_tn(a, b):
    return lax.dot_general(a.astype(BF16), b.astype(BF16), (((0,), (0,)), ((), ())),
                           preferred_element_type=F32)


def _split_bf16(x, terms):
    parts = []
    for _ in range(terms - 1):
        hi = x.astype(BF16)
        parts.append(hi)
        x = x - hi.astype(F32)
    parts.append(x.astype(BF16))
    return parts


def _mm_exact_lhs(a_bf16, b):
    out = None
    for part in _split_bf16(b, 3):
        d = jnp.dot(a_bf16, part, preferred_element_type=F32)
        out = d if out is None else out + d
    return out


def _pack_bf16_pairs(x):
    half = x.shape[1] // 2
    lo = lax.bitcast_convert_type(x[:, :half].astype(BF16).astype(F32), U32) >> 16
    hi = lax.bitcast_convert_type(x[:, half:].astype(BF16).astype(F32), U32) & jnp.uint32(0xFFFF0000)
    return lo | hi


def _unpack_bf16_pairs(w):
    lo = lax.bitcast_convert_type(w << 16, F32)
    hi = lax.bitcast_convert_type(w & jnp.uint32(0xFFFF0000), F32)
    return jnp.concatenate([lo, hi], axis=1)


def _iota(shape, dim):
    return lax.broadcasted_iota(I32, shape, dim)


def _softplus(x):
    return jnp.maximum(x, 0.0) + jnp.log(1.0 + jnp.exp(-jnp.abs(x)))


def _stack_rows(x, n):
    return jnp.concatenate([x] * n, axis=0)


def _tri_inv_packed(ms, eye, bdmask):
    def bd(x):
        return jnp.where(bdmask, _stack_rows(x, HEADS_PER_GROUP), jnp.zeros((), BF16))

    def mul(a, b):
        ah, al = _split_bf16(a, 2)
        bh, bl = _split_bf16(b, 2)
        bdh = bd(bh)
        return (jnp.dot(ah, bdh, preferred_element_type=F32) + jnp.dot(al, bdh, preferred_element_type=F32)
                + jnp.dot(ah, bd(bl), preferred_element_type=F32))

    lane = _iota((CHUNK, PACK_W), 1) % CHUNK
    leaf = (lane // INV_LEAF) == (_iota((CHUNK, PACK_W), 0) // INV_LEAF)
    ds = [jnp.where(leaf, m, 0.0) for m in ms]
    ls = [m - d for m, d in zip(ms, ds)]
    ts = [eye - d for d in ds]
    xs = [mul(d, d) for d in ds]
    for _ in range(2):
        rs = [mul(jnp.concatenate([t, x], axis=0), x) for t, x in zip(ts, xs)]
        ts = [t + r[:CHUNK] for t, r in zip(ts, rs)]
        xs = [r[CHUNK:] for r in rs]
    bs = [t + mul(t, x) for t, x in zip(ts, xs)]
    ns = [mul(b, l) for b, l in zip(bs, ls)]
    n2 = [mul(n, n) for n in ns]
    ps = [(eye - n) + mul(eye - n, q) for n, q in zip(ns, n2)]
    return [mul(p, b) for p, b in zip(ps, bs)]


def _shift_rows(x, prev8, k, row8):
    r = pltpu.roll(x, k, 0)
    pr = pltpu.roll(prev8, k, 0)
    head = jnp.where(row8 < k, pr, r[:8])
    return jnp.concatenate([head, r[8:]], axis=0)


def _in_proj_kernel(x_ref, nw_ref, w_ref, wa_ref, o_ref, oa_ref, h_ref):
    @pl.when(pl.program_id(1) == 0)
    def _():
        x = x_ref[...]
        y = x * lax.rsqrt(jnp.mean(x * x, axis=-1, keepdims=True) + EPS)
        h_ref[...] = (y * nw_ref[...]).astype(BF16)
        oa_ref[...] = jnp.dot(h_ref[...], wa_ref[...], preferred_element_type=F32)

    o_ref[...] = jnp.dot(h_ref[...], w_ref[...], preferred_element_type=F32).astype(BF16)


def _in_proj(x2, norm_w, w_main, w_aux):
    t = x2.shape[0]
    tm = min(1024, t)
    tn = 2304
    return pl.pallas_call(
        _in_proj_kernel,
        grid=(t // tm, MAIN_COLS // tn),
        in_specs=[pl.BlockSpec((tm, D_MODEL), lambda i, j: (i, 0)),
                  pl.BlockSpec((1, D_MODEL), lambda i, j: (0, 0)),
                  pl.BlockSpec((D_MODEL, tn), lambda i, j: (0, j)),
                  pl.BlockSpec((D_MODEL, AUX_COLS), lambda i, j: (0, 0))],
        out_specs=[pl.BlockSpec((tm, tn), lambda i, j: (i, j)),
                   pl.BlockSpec((tm, AUX_COLS), lambda i, j: (i, 0))],
        out_shape=[jax.ShapeDtypeStruct((t, MAIN_COLS), BF16), jax.ShapeDtypeStruct((t, AUX_COLS), F32)],
        scratch_shapes=[pltpu.VMEM((tm, D_MODEL), BF16)],
        compiler_params=pltpu.CompilerParams(dimension_semantics=("parallel", "arbitrary"),
                                             vmem_limit_bytes=VMEM_LIMIT),
        name="in_proj",
    )(x2, norm_w.reshape(1, D_MODEL), w_main, w_aux)


def _gdn_kernel(q_ref, k_ref, v_ref, z_ref, ba_ref, conv_ref, alog_ref, dtb_ref, nw_ref, o_ref,
                carry_ref, qn_ref, kn_ref, vv_ref, beta_ref, g_ref, s_ref, *, ts):
    @pl.when(pl.program_id(1) == 0)
    def _():
        carry_ref[...] = jnp.zeros_like(carry_ref)
        s_ref[...] = jnp.zeros_like(s_ref)

    row8 = _iota((8, D_MODEL), 0)

    def conv_silu(idx, x_ref):
        x = x_ref[...].astype(F32)
        prev8 = carry_ref[idx]
        w4 = conv_ref[:, idx * D_MODEL:(idx + 1) * D_MODEL]
        y = x * w4[CONV_W - 1:CONV_W]
        for k in range(1, CONV_W):
            y = y + _shift_rows(x, prev8, k, row8) * w4[CONV_W - 1 - k:CONV_W - k]
        carry_ref[idx] = x[ts - 8:ts]
        return y * jax.nn.sigmoid(y)

    def l2norm_heads(x, scale):
        parts = []
        for h in range(GDN_HEADS):
            xh = x[:, h * GDN_DK:(h + 1) * GDN_DK]
            parts.append(xh * (lax.rsqrt(jnp.sum(xh * xh, axis=-1, keepdims=True) + EPS) * scale))
        return jnp.concatenate(parts, axis=1)

    qn_ref[...] = l2norm_heads(conv_silu(0, q_ref), GDN_DK ** -0.5)
    kn_ref[...] = l2norm_heads(conv_silu(1, k_ref), 1.0)
    vv_ref[...] = conv_silu(2, v_ref)
    ba = ba_ref[...]
    beta_ref[...] = jax.nn.sigmoid(ba)
    g_ref[...] = -jnp.exp(alog_ref[...]) * _softplus(ba + dtb_ref[...])

    gw = HEADS_PER_GROUP * GDN_DK
    lane_p = _iota((CHUNK, PACK_W), 1)
    row_p = _iota((CHUNK, PACK_W), 0)
    eye = jnp.where((lane_p % CHUNK) == row_p, 1.0, 0.0).astype(F32)
    strict = (lane_p % CHUNK) < row_p
    incl = (lane_p % CHUNK) <= row_p
    bd_pp = (_iota((PACK_W, PACK_W), 0) // CHUNK) == (_iota((PACK_W, PACK_W), 1) // CHUNK)
    bd_pn = (_iota((PACK_W, gw), 0) // CHUNK) == (_iota((PACK_W, gw), 1) // GDN_DK)
    bd_pn2 = (_iota((PACK_W, 2 * gw), 0) // CHUNK) == ((_iota((PACK_W, 2 * gw), 1) % gw) // GDN_DK)
    ltri = jnp.where(_iota((CHUNK, CHUNK), 1) <= _iota((CHUNK, CHUNK), 0), 1.0, 0.0).astype(BF16)
    ones_cc = jnp.ones((CHUNK, CHUNK), BF16)
    lane128 = _iota((CHUNK, 128), 1)
    nw = nw_ref[...]

    n_groups = GDN_HEADS // HEADS_PER_GROUP
    hcols = [slice(j * GDN_DK, (j + 1) * GDN_DK) for j in range(HEADS_PER_GROUP)]

    def intra_chunk(rows_list):
        units = [(i, gi) for i in range(len(rows_list)) for gi in range(n_groups)]
        gcs = [_mm_exact_lhs(ltri, g_ref[rows, :]) for rows in rows_list]
        beta = [beta_ref[rows, :] for rows in rows_list]
        pre = []
        for i, gi in units:
            rows = rows_list[i]
            heads = [gi * HEADS_PER_GROUP + j for j in range(HEADS_PER_GROUP)]
            cols = slice(gi * gw, (gi + 1) * gw)
            beta_nat = jnp.concatenate(
                [jnp.broadcast_to(beta[i][:, h:h + 1], (CHUNK, GDN_DK)) for h in heads], axis=1)
            gcol = [jnp.broadcast_to(gcs[i][:, 8 + h:9 + h], (CHUNK, GDN_DK)) for h in heads]
            gcol_p = jnp.concatenate([jnp.where(lane128 < CHUNK, gcol[0], gcol[1]),
                                      jnp.where(lane128 < CHUNK, gcol[2], gcol[3])], axis=1)
            pre.append(dict(qn=qn_ref[rows, cols], kn=kn_ref[rows, cols], vv=vv_ref[rows, cols],
                            beta_nat=beta_nat, gc_nat=jnp.concatenate(gcol, axis=1), gcol_p=gcol_p))
        grow = [_mm_exact_lhs(ones_cc, u["gcol_p"] * eye) for u in pre]
        aas = [_mm_nt(jnp.concatenate([u["kn"] * u["beta_nat"], u["qn"]], axis=0),
                      jnp.where(bd_pn, _stack_rows(u["kn"], HEADS_PER_GROUP), 0.0)) for u in pre]
        decay = [jnp.exp(jnp.minimum(u["gcol_p"] - gr, 0.0)) for u, gr in zip(pre, grow)]
        t_inv = _tri_inv_packed([jnp.where(strict, aa[:CHUNK] * d, 0.0) for aa, d in zip(aas, decay)],
                                eye, bd_pp)
        out = []
        for u, aa, d, ti in zip(pre, aas, decay, t_inv):
            egc = jnp.exp(u["gc_nat"])
            data = jnp.concatenate([u["vv"] * u["beta_nat"], u["kn"] * u["beta_nat"] * egc], axis=1)
            uw = _mm(ti, jnp.where(bd_pn2, _stack_rows(data, HEADS_PER_GROUP), 0.0))
            gl = u["gc_nat"][CHUNK - 1:CHUNK, :]
            out.append(dict(uw=uw, attn=jnp.where(incl, aa[CHUNK:] * d, 0.0), qd=u["qn"] * egc,
                            kd=u["kn"] * jnp.exp(gl - u["gc_nat"]), egl=jnp.exp(gl)))
        return out

    def recurrence(rows, group_units):
        wq = {}
        for gi, u in enumerate(group_units):
            for j, hc in enumerate(hcols):
                h = gi * HEADS_PER_GROUP + j
                wq[h] = _mm(jnp.concatenate([u["uw"][:, gw + j * GDN_DK:gw + (j + 1) * GDN_DK],
                                             u["qd"][:, hc]], axis=0), s_ref[h])
        vnew = {}
        for gi, u in enumerate(group_units):
            for j, hc in enumerate(hcols):
                h = gi * HEADS_PER_GROUP + j
                vnew[h] = u["uw"][:, hc] - wq[h][:CHUNK]
                s_ref[h] = s_ref[h] * u["egl"][:, hc] + _mm_tn(u["kd"][:, hc], vnew[h])
        for gi, u in enumerate(group_units):
            heads = [gi * HEADS_PER_GROUP + j for j in range(HEADS_PER_GROUP)]
            vn = jnp.concatenate([vnew[h] for h in heads], axis=1)
            o = jnp.concatenate([wq[h][CHUNK:] for h in heads], axis=1) + _mm(
                u["attn"], jnp.where(bd_pn, _stack_rows(vn, HEADS_PER_GROUP), 0.0))
            for j, h in enumerate(heads):
                oh = o[:, hcols[j]]
                oh = oh * lax.rsqrt(jnp.mean(oh * oh, axis=-1, keepdims=True) + EPS) * nw
                zh = z_ref[rows, h * GDN_DK:(h + 1) * GDN_DK].astype(F32)
                o_ref[rows, h * GDN_DK:(h + 1) * GDN_DK] = (oh * (zh * jax.nn.sigmoid(zh))).astype(BF16)

    def chunk_pair_body(c, carry):
        rows_list = [pl.ds(pl.multiple_of((c * GDN_CHUNKS_PER_ITER + i) * CHUNK, CHUNK), CHUNK)
                     for i in range(GDN_CHUNKS_PER_ITER)]
        units = intra_chunk(rows_list)
        for i, rows in enumerate(rows_list):
            recurrence(rows, units[i * n_groups:(i + 1) * n_groups])
        return carry

    lax.fori_loop(0, ts // (CHUNK * GDN_CHUNKS_PER_ITER), chunk_pair_body, 0)


def _gdn_parts(p, p_aux, conv_w, a_log, dt_bias, norm_w, ts, ns):
    pad8 = lambda v: jnp.zeros((1, 128), F32).at[0, 8:16].set(v)
    tok = lambda col: pl.BlockSpec((ts, D_MODEL), lambda b, s, col=col: (b * ns + s, col))
    const = lambda shape: pl.BlockSpec(shape, lambda b, s: (0,) * len(shape))
    in_specs = [tok(0), tok(1), tok(2), tok(3),
                pl.BlockSpec((ts, 128), lambda b, s: (b * ns + s, LORA_COLS // 128)),
                const((CONV_W, 3 * D_MODEL)), const((1, 128)), const((1, 128)), const((1, GDN_DK))]
    operands = [p, p, p, p, p_aux, conv_w, pad8(a_log), pad8(dt_bias), norm_w.reshape(1, GDN_DK)]
    scratch = [pltpu.VMEM((3, 8, D_MODEL), F32),
               pltpu.VMEM((ts, D_MODEL), F32), pltpu.VMEM((ts, D_MODEL), F32), pltpu.VMEM((ts, D_MODEL), F32),
               pltpu.VMEM((ts, 128), F32), pltpu.VMEM((ts, 128), F32),
               pltpu.VMEM((GDN_HEADS, GDN_DK, GDN_DK), F32)]
    return in_specs, operands, scratch


def _rwkv_kernel(r_ref, k_ref, v_ref, lo_ref, mu_ref, w0_ref, w2_ref, a0_ref, a2_ref, g2_ref,
                 kk_ref, ka_ref, rk_ref, lnw_ref, lnb_ref, seg_ref, o_ref,
                 carry_ref, rs_ref, ks_ref, vs_ref, kks_ref, bs_ref, ld_ref, gg_ref, os_ref, st_ref,
                 *, ts):
    @pl.when(pl.program_id(1) == 0)
    def _():
        carry_ref[...] = jnp.zeros_like(carry_ref)
        st_ref[...] = jnp.zeros_like(st_ref)

    width = D_MODEL
    row8 = _iota((8, width), 0)
    row8l = _iota((8, LORA_COLS), 0)

    def seg_sum(x):
        xb = x.astype(BF16)
        return jnp.concatenate(
            [jnp.dot(xb[:, g * PACK_W:(g + 1) * PACK_W], seg_ref[...], preferred_element_type=F32)
             for g in range(width // PACK_W)], axis=1)

    def lerp(idx, x_ref, mu, r8):
        x = x_ref[...].astype(F32)
        xs = x + (_shift_rows(x, carry_ref[idx, :, :x.shape[1]], 1, r8) - x) * mu
        carry_ref[idx, :, :x.shape[1]] = x[ts - 8:ts]
        return xs

    r = lerp(0, r_ref, mu_ref[:, 0:width], row8)
    k = lerp(1, k_ref, mu_ref[:, width:2 * width], row8)
    v = lerp(2, v_ref, mu_ref[:, 2 * width:3 * width], row8)
    lo = lerp(3, lo_ref, mu_ref[:, 3 * width:3 * width + LORA_COLS], row8l)
    lo_a = lo[:, :128]
    w_log = -_softplus(-(w0_ref[...] + _mm(jnp.tanh(lo_a), w2_ref[...]))) - 0.5
    ld_ref[...] = -jnp.exp(w_log)
    aa = jax.nn.sigmoid(a0_ref[...] + _mm(lo_a, a2_ref[...]))
    gg_ref[...] = _mm(jax.nn.sigmoid(lo[:, 128:]), g2_ref[...])
    kx = k * kk_ref[...]
    kk = kx * lax.rsqrt(seg_sum(kx * kx) + EPS)
    k = k * (1.0 + (aa - 1.0) * ka_ref[...])
    rs_ref[...] = r
    ks_ref[...] = k
    vs_ref[...] = v
    kks_ref[...] = kk
    bs_ref[...] = kk * aa

    lane_p = _iota((CHUNK, PACK_W), 1)
    row_p = _iota((CHUNK, PACK_W), 0)
    eye = jnp.where((lane_p % CHUNK) == row_p, 1.0, 0.0).astype(F32)
    strict = (lane_p % CHUNK) < row_p
    incl = (lane_p % CHUNK) <= row_p
    bd = (_iota((PACK_W, PACK_W), 0) // CHUNK) == (_iota((PACK_W, PACK_W), 1) // CHUNK)
    bd2 = jnp.concatenate([bd, bd], axis=1)
    ltri = jnp.where(_iota((CHUNK, CHUNK), 1) <= _iota((CHUNK, CHUNK), 0), 1.0, 0.0).astype(BF16)

    def bdiag(x):
        return jnp.where(bd, _stack_rows(x, HEADS_PER_GROUP), 0.0)

    n_groups = width // PACK_W
    gcols = [slice(g * PACK_W, (g + 1) * PACK_W) for g in range(n_groups)]

    def intra_chunk(rows_list):
        units = [(rows, cols) for rows in rows_list for cols in gcols]
        ld = [ld_ref[rows, cols] for rows, cols in units]
        cs = [_mm_exact_lhs(ltri, x) for x in ld]
        pre = []
        for g, (rows, cols) in enumerate(units):
            cl = cs[g][CHUNK - 1:CHUNK, :]
            e_neg = jnp.exp(-cs[g])
            e_dec = jnp.exp(cl - cs[g])
            kc, bc = ks_ref[rows, cols], bs_ref[rows, cols]
            rt = rs_ref[rows, cols] * jnp.exp(cs[g])
            kt = kks_ref[rows, cols] * jnp.exp(cs[g] - ld[g])
            pre.append(dict(cl=cl, rt=rt, kt=kt, vc=vs_ref[rows, cols], kh=kc * e_neg, bh=bc * e_neg,
                            kdec=kc * e_dec, bdec=bc * e_dec))
        akb = [_mm_nt(jnp.concatenate([u["kt"], u["rt"]], axis=0),
                      jnp.concatenate([bdiag(u["kh"]), bdiag(u["bh"])], axis=0)) for u in pre]
        t_inv = _tri_inv_packed([jnp.where(strict, a[:CHUNK, PACK_W:], 0.0) for a in akb], eye, bd)
        xo = [_mm(jnp.concatenate([jnp.where(strict, a[:CHUNK, :PACK_W], 0.0),
                                   jnp.where(incl, a[CHUNK:, :PACK_W], 0.0)], axis=0), bdiag(u["vc"]))
              for a, u in zip(akb, pre)]
        tt = [_mm(ti, jnp.where(bd2, _stack_rows(jnp.concatenate([x[:CHUNK], u["kt"]], axis=1),
                                                 HEADS_PER_GROUP), 0.0))
              for ti, x, u in zip(t_inv, xo, pre)]
        rr = [_mm(jnp.where(incl, a[CHUNK:, PACK_W:], 0.0),
                  jnp.where(bd2, _stack_rows(jnp.concatenate([t[:, PACK_W:], t[:, :PACK_W]], axis=1),
                                             HEADS_PER_GROUP), 0.0))
              for a, t in zip(akb, tt)]
        return [dict(u, tk=t[:, PACK_W:], u0=t[:, :PACK_W], rq=u["rt"] - r[:, :PACK_W],
                     o1=x[CHUNK:] - r[:, PACK_W:]) for u, x, t, r in zip(pre, xo, tt, rr)]

    def recurrence(rows, group_units):
        uo = [_mm_nt(jnp.concatenate([u["tk"], u["rq"]], axis=0), st_ref[g])
              for g, u in enumerate(group_units)]
        upd = []
        for g, (cols, u) in enumerate(zip(gcols, group_units)):
            os_ref[rows, cols] = u["o1"] + uo[g][CHUNK:]
            upd.append(_mm_tn(jnp.concatenate([u["vc"], u["u0"] + uo[g][:CHUNK]], axis=0),
                              jnp.concatenate([u["kdec"], -u["bdec"]], axis=0)))
        for g, u in enumerate(group_units):
            st_ref[g] = st_ref[g] * jnp.exp(u["cl"]) + jnp.where(bd, upd[g], 0.0)

    def chunks_body(c, carry):
        rows_list = [pl.ds(pl.multiple_of((c * RWKV_CHUNKS_PER_ITER + i) * CHUNK, CHUNK), CHUNK)
                     for i in range(RWKV_CHUNKS_PER_ITER)]
        units = intra_chunk(rows_list)
        for i, rows in enumerate(rows_list):
            recurrence(rows, units[i * n_groups:(i + 1) * n_groups])
        return carry

    lax.fori_loop(0, ts // (CHUNK * RWKV_CHUNKS_PER_ITER), chunks_body, 0)

    o = os_ref[...]
    inv_n = 1.0 / RWKV_N
    mean = seg_sum(o) * inv_n
    cen = o - mean
    var = seg_sum(cen * cen) * inv_n
    o = cen * lax.rsqrt(var + RWKV_GN_EPS) * lnw_ref[...] + lnb_ref[...]
    bonus = seg_sum(rs_ref[...] * ks_ref[...] * rk_ref[...]) * vs_ref[...]
    o_ref[...] = ((o + bonus) * gg_ref[...]).astype(BF16)


def _rwkv_parts(p, p_aux, mu, w0, w2, a0, a2, g2, k_k, k_a, r_k, ln_w, ln_b, ts, ns):
    width = D_MODEL
    row = lambda v: v.reshape(1, -1)
    w2p = jnp.concatenate([w2, jnp.zeros_like(w2)], axis=0).astype(BF16)
    a2p = jnp.concatenate([jnp.zeros_like(a2), a2], axis=0).astype(BF16)
    seg = (np.arange(PACK_W)[:, None] // RWKV_N == np.arange(PACK_W)[None, :] // RWKV_N)
    seg = jnp.asarray(seg, BF16)
    tok = lambda col: pl.BlockSpec((ts, width), lambda b, s, col=col: (b * ns + s, col))
    const = lambda shape: pl.BlockSpec(shape, lambda b, s: (0,) * len(shape))
    fbuf = lambda: pltpu.VMEM((ts, width), F32)
    in_specs = [tok(4), tok(5), tok(6),
                pl.BlockSpec((ts, LORA_COLS), lambda b, s: (b * ns + s, 0)),
                const((1, 3 * width + LORA_COLS)), const((1, width)), const((128, width)),
                const((1, width)), const((128, width)), const((128, width)),
                const((1, width)), const((1, width)), const((1, width)), const((1, width)),
                const((1, width)), const((PACK_W, PACK_W))]
    operands = [p, p, p, p_aux, row(mu), row(w0), w2p, row(a0), a2p, g2.astype(BF16), row(k_k), row(k_a),
                row(r_k), row(ln_w), row(ln_b), seg]
    scratch = [pltpu.VMEM((4, 8, width), F32),
               fbuf(), fbuf(), fbuf(), fbuf(), fbuf(), fbuf(), fbuf(), fbuf(),
               pltpu.VMEM((width // PACK_W, PACK_W, PACK_W), F32)]
    return in_specs, operands, scratch


def _mixer_call(body, name, parts, bsz, seq, ts):
    in_specs, operands, scratch = parts
    ns = seq // ts
    return pl.pallas_call(
        functools.partial(body, ts=ts),
        grid=(bsz, ns),
        in_specs=in_specs,
        out_specs=pl.BlockSpec((ts, D_MODEL), lambda b, s: (b * ns + s, 0)),
        out_shape=jax.ShapeDtypeStruct((bsz * seq, D_MODEL), BF16),
        scratch_shapes=scratch,
        compiler_params=pltpu.CompilerParams(dimension_semantics=("parallel", "arbitrary"),
                                             vmem_limit_bytes=VMEM_LIMIT),
        name=name,
    )(*operands)


def _mixers(p, p_aux, gdn_params, rwkv_params, bsz, seq):
    ts = min(MIXER_TILE, seq)
    ns = seq // ts
    ya = _mixer_call(_gdn_kernel, "gdn", _gdn_parts(p, p_aux, *gdn_params, ts, ns), bsz, seq, ts)
    yb = _mixer_call(_rwkv_kernel, "rwkv", _rwkv_parts(p, p_aux, *rwkv_params, ts, ns), bsz, seq, ts)
    return ya, yb


def _merge_kernel(x_ref, ya_ref, yb_ref, ga_ref, gb_ref, pa_ref, pb_ref, wo_ref, nw_ref, rw_ref, rb_ref,
                  x1_ref, hn_ref, lg_ref):
    merged = (jax.nn.sigmoid(ga_ref[...].astype(F32)) * jnp.dot(ya_ref[...], pa_ref[...], preferred_element_type=F32)
              + jax.nn.sigmoid(gb_ref[...].astype(F32)) * jnp.dot(yb_ref[...], pb_ref[...], preferred_element_type=F32))
    x1 = x_ref[...] + _mm(merged, wo_ref[...])
    x1_ref[...] = x1
    hn = x1 * lax.rsqrt(jnp.mean(x1 * x1, axis=-1, keepdims=True) + EPS) * nw_ref[...]
    hn_ref[...] = _pack_bf16_pairs(hn)
    lg_ref[...] = lax.dot_general(rw_ref[...], hn, (((1,), (1,)), ((), ())),
                                  preferred_element_type=F32, precision=HIGHEST) + rb_ref[...]


def _merge(x2, ya, yb, p, proj_a, proj_b, w_out, norm_w, router_w, router_b):
    t = x2.shape[0]
    tm = min(512, t)
    tok = lambda col: pl.BlockSpec((tm, D_MODEL), lambda i, col=col: (i, col))
    const = lambda shape: pl.BlockSpec(shape, lambda i: (0,) * len(shape))
    return pl.pallas_call(
        _merge_kernel,
        grid=(t // tm,),
        in_specs=[tok(0), tok(0), tok(0), tok(7), tok(8),
                  const((D_MODEL, D_MODEL)), const((D_MODEL, D_MODEL)), const((D_MODEL, D_MODEL)),
                  const((1, D_MODEL)), const((N_EXPERTS, D_MODEL)), const((N_EXPERTS, 1))],
        out_specs=[tok(0), pl.BlockSpec((tm, D_MODEL // 2), lambda i: (i, 0)),
                   pl.BlockSpec((N_EXPERTS, tm), lambda i: (0, i))],
        out_shape=[jax.ShapeDtypeStruct((t, D_MODEL), F32), jax.ShapeDtypeStruct((t, D_MODEL // 2), U32),
                   jax.ShapeDtypeStruct((N_EXPERTS, t), F32)],
        compiler_params=pltpu.CompilerParams(dimension_semantics=("parallel",),
                                             vmem_limit_bytes=VMEM_LIMIT),
        name="merge",
    )(x2, ya, yb, p, p, proj_a.astype(BF16), proj_b.astype(BF16), w_out.astype(BF16),
      norm_w.reshape(1, D_MODEL), router_w.T, router_b.reshape(N_EXPERTS, 1))


def _route_kernel(lg_ref, eidx_ref, gate_ref, rank_ref, base_ref, cnt_ref, carry_ref, *, tt):
    @pl.when(pl.program_id(0) == 0)
    def _():
        carry_ref[...] = jnp.zeros_like(carry_ref)

    l = lg_ref[...]
    ie = _iota((N_EXPERTS, tt), 0)
    vals, hots, idxs = [], [], []
    for _ in range(TOP_K):
        m = jnp.max(l, axis=0, keepdims=True)
        idx = jnp.min(jnp.where(l == m, ie, N_EXPERTS), axis=0, keepdims=True)
        hot = ie == idx
        vals.append(m)
        hots.append(hot)
        idxs.append(idx)
        l = jnp.where(hot, -jnp.inf, l)
    exps = [jnp.exp(v - vals[0]) for v in vals]
    den = exps[0] + exps[1] + exps[2] + exps[3]
    gate_ref[...] = jnp.concatenate([e / den for e in exps], axis=0)
    eidx_ref[...] = jnp.concatenate(idxs, axis=0)

    sel = jnp.zeros((N_EXPERTS, tt), F32)
    for hot in hots:
        sel = sel + jnp.where(hot, 1.0, 0.0)
    before = jnp.where(_iota((tt, tt), 0) < _iota((tt, tt), 1), 1.0, 0.0).astype(BF16)
    carry = carry_ref[...]
    prefix = jnp.dot(sel.astype(BF16), before, preferred_element_type=F32) + carry[:, 0:1]
    rank_ref[...] = jnp.concatenate(
        [jnp.sum(jnp.where(hot, prefix, 0.0), axis=0, keepdims=True) for hot in hots], axis=0).astype(I32)
    cnt = jnp.broadcast_to(jnp.sum(sel, axis=1, keepdims=True), (N_EXPERTS, 128))
    base_ref[0] = carry
    cnt_ref[0] = cnt
    carry_ref[...] = carry + cnt


def _route(logits_t, tt):
    t = logits_t.shape[1]
    nt = t // tt
    row4 = pl.BlockSpec((TOP_K, tt), lambda i: (0, i))
    per_tile = pl.BlockSpec((1, N_EXPERTS, 128), lambda i: (i, 0, 0))
    return pl.pallas_call(
        functools.partial(_route_kernel, tt=tt),
        grid=(nt,),
        in_specs=[pl.BlockSpec((N_EXPERTS, tt), lambda i: (0, i))],
        out_specs=[row4, row4, row4, per_tile, per_tile],
        out_shape=[jax.ShapeDtypeStruct((TOP_K, t), I32), jax.ShapeDtypeStruct((TOP_K, t), F32),
                   jax.ShapeDtypeStruct((TOP_K, t), I32),
                   jax.ShapeDtypeStruct((nt, N_EXPERTS, 128), F32),
                   jax.ShapeDtypeStruct((nt, N_EXPERTS, 128), F32)],
        scratch_shapes=[pltpu.VMEM((N_EXPERTS, 128), F32)],
        compiler_params=pltpu.CompilerParams(dimension_semantics=("arbitrary",)),
        name="route",
    )(logits_t)


def _count_le(sorted_vals, queries):
    return jnp.sum((sorted_vals[None, :] <= queries[:, None]).astype(I32), axis=1)


def _routing_plan(cnt, eidx, rank, tt):
    t = cnt.shape[0] * tt
    n_mb = (t * TOP_K) // EXPERT_BLOCK + N_EXPERTS
    counts = jnp.sum(cnt, axis=0)
    padded = ((counts + EXPERT_BLOCK - 1) // EXPERT_BLOCK) * EXPERT_BLOCK
    end_pad = jnp.cumsum(padded)
    start_pad = end_pad - padded
    hot = eidx[:, :, None] == jnp.arange(N_EXPERTS, dtype=I32)[None, None, :]
    dest = jnp.sum(jnp.where(hot, start_pad[None, None, :], 0), axis=-1) + rank

    mb_start = jnp.arange(n_mb, dtype=I32) * EXPERT_BLOCK
    mb_expert = jnp.minimum(_count_le(end_pad, mb_start), N_EXPERTS - 1).astype(I32)
    mb_active = (mb_start < end_pad[-1]).astype(I32)

    dest = dest.astype(I32)
    return dest, _sc_slot_tokens(dest, n_mb * EXPERT_BLOCK), mb_expert, mb_active, n_mb


def _sc_slot_tokens(dest, n_slots):
    info = plsc.get_sparse_core_info()
    nc, ns, nl = info.num_cores, info.num_subcores, info.num_lanes
    top_k, t = dest.shape
    per_worker = n_slots // (nc * ns)
    assert per_worker * nc * ns == n_slots and per_worker % nl == 0 and t % SC_INDEX_CHUNK == 0
    fill_mask = (1 << (t.bit_length() - 1)) - 1
    mesh = plsc.VectorSubcoreMesh(core_axis_name="c", subcore_axis_name="s")

    @functools.partial(
        pl.kernel, mesh=mesh,
        out_type=jax.ShapeDtypeStruct((n_slots,), I32),
        scratch_types=[pltpu.VMEM((per_worker,), I32), pltpu.VMEM((SC_INDEX_CHUNK,), I32)],
        compiler_params=pltpu.CompilerParams(needs_layout_passes=False),
    )
    def invert(dest_hbm, out_hbm, local_v, chunk_v):
        lo = (lax.axis_index("s") * nc + lax.axis_index("c")) * per_worker
        lane = lax.iota(I32, nl)

        @pl.loop(0, per_worker // nl)
        def _(j):
            local_v[pl.ds(j * nl, nl)] = (lo + j * nl + lane) & fill_mask

        for k in range(top_k):
            @pl.loop(0, t // SC_INDEX_CHUNK)
            def _(c):
                pltpu.sync_copy(dest_hbm.at[k, pl.ds(c * SC_INDEX_CHUNK, SC_INDEX_CHUNK)], chunk_v)

                @pl.loop(0, SC_INDEX_CHUNK // nl)
                def _(j):
                    d = chunk_v[pl.ds(j * nl, nl)] - lo
                    tok = c * SC_INDEX_CHUNK + j * nl + lane
                    plsc.store_scatter(local_v, [d], tok, mask=(d >= 0) & (d < per_worker))

        pltpu.sync_copy(local_v, out_hbm.at[pl.ds(lo, per_worker)])

    return invert(dest)


def _sc_gather(table, idx):
    info = plsc.get_sparse_core_info()
    nc, ns = info.num_cores, info.num_subcores
    n_rows, width = idx.shape[0], table.shape[1]
    per_worker = n_rows // (nc * ns)
    steps = per_worker // SC_GATHER_ROWS
    assert per_worker * nc * ns == n_rows and steps * SC_GATHER_ROWS == per_worker and steps % 2 == 0
    mesh = plsc.VectorSubcoreMesh(core_axis_name="c", subcore_axis_name="s")

    @functools.partial(
        pl.kernel, mesh=mesh,
        out_type=jax.ShapeDtypeStruct((n_rows, width), table.dtype),
        scratch_types=[pltpu.VMEM((steps, SC_GATHER_ROWS), I32),
                       pltpu.VMEM((2, SC_GATHER_ROWS, width), table.dtype),
                       pltpu.SemaphoreType.DMA((2,))],
    )
    def gather(table_hbm, idx_hbm, out_hbm, idx_v, rows_v, sems):
        first = (lax.axis_index("s") * nc + lax.axis_index("c")) * steps
        pltpu.sync_copy(idx_hbm.at[pl.ds(first, steps)], idx_v)

        def gather_copy(step, buf):
            return pltpu.make_async_copy(table_hbm.at[idx_v.at[step]], rows_v.at[buf], sems.at[buf])

        gather_copy(0, 0).start()

        @pl.loop(0, steps, step=2)
        def _(i):
            for buf in range(2):
                cur = i + buf

                @pl.when(cur + 1 < steps)
                def _():
                    gather_copy(cur + 1, 1 - buf).start()

                gather_copy(cur, buf).wait()
                row0 = pl.multiple_of((first + cur) * SC_GATHER_ROWS, 8)
                pltpu.sync_copy(rows_v.at[buf], out_hbm.at[pl.ds(row0, SC_GATHER_ROWS)])

    return gather(table, idx.reshape(n_rows // SC_GATHER_ROWS, SC_GATHER_ROWS))


def _expert_kernel(e_ref, act_ref, x_ref, wgu_ref, wd_ref, bg_ref, bl_ref, bd_ref, o_ref,
                   wg_c, wl_c, wd_c):
    mb = pl.program_id(0)
    new_expert = jnp.logical_or(mb == 0, e_ref[mb] != e_ref[jnp.maximum(mb - 1, 0)])

    @pl.when(jnp.logical_and(new_expert, act_ref[mb] == 1))
    def _():
        lane = _iota((D_MODEL, 128), 1)
        half = lane < 64
        idx = jnp.where(half, 2 * lane, 2 * (lane - 64) + 1)
        for m in range(D_MODEL // 128):
            a = jnp.take_along_axis(wgu_ref[0, :, (2 * m) * 128:(2 * m + 1) * 128], idx, axis=1)
            b = jnp.take_along_axis(wgu_ref[0, :, (2 * m + 1) * 128:(2 * m + 2) * 128], idx, axis=1)
            cols = slice(m * 128, (m + 1) * 128)
            wg_c[:, cols] = jnp.where(half, a, pltpu.roll(b, 64, 1)).astype(BF16)
            wl_c[:, cols] = jnp.where(half, pltpu.roll(a, 64, 1), b).astype(BF16)
        wd_c[...] = wd_ref[0].astype(BF16)

    @pl.when(act_ref[mb] == 0)
    def _():
        o_ref[...] = jnp.zeros_like(o_ref)

    @pl.when(act_ref[mb] == 1)
    def _():
        x = _unpack_bf16_pairs(x_ref[...]).astype(BF16)
        glu = jnp.dot(x, wg_c[...], preferred_element_type=F32) + bg_ref[0]
        lin = jnp.dot(x, wl_c[...], preferred_element_type=F32) + bl_ref[0]
        glu = jnp.minimum(glu, SWIGLU_LIMIT)
        lin = jnp.clip(lin, -SWIGLU_LIMIT, SWIGLU_LIMIT)
        act = glu * jax.nn.sigmoid(SWIGLU_ALPHA * glu) * (lin + 1.0)
        o_ref[...] = _pack_bf16_pairs(_mm(act, wd_c[...]) + bd_ref[0])


def _experts(xb, mb_expert, mb_active, w_gu, w_down, bg, bl, bd, n_mb):
    d_ff = w_down.shape[1]
    assert d_ff == D_MODEL and w_gu.shape[1:] == (D_MODEL, 2 * d_ff)
    bspec = pl.BlockSpec((1, 1, D_MODEL), lambda m, e, a: (e[m], 0, 0))
    xspec = pl.BlockSpec((EXPERT_BLOCK, D_MODEL // 2), lambda m, e, a: (m, 0))
    wcache = pltpu.VMEM((D_MODEL, D_MODEL), BF16)
    return pl.pallas_call(
        _expert_kernel,
        grid_spec=pltpu.PrefetchScalarGridSpec(
            num_scalar_prefetch=2,
            grid=(n_mb,),
            in_specs=[xspec,
                      pl.BlockSpec((1, D_MODEL, 2 * d_ff), lambda m, e, a: (e[m], 0, 0)),
                      pl.BlockSpec((1, d_ff, D_MODEL), lambda m, e, a: (e[m], 0, 0)),
                      bspec, bspec, bspec],
            out_specs=xspec,
            scratch_shapes=[wcache, wcache, wcache]),
        out_shape=jax.ShapeDtypeStruct(xb.shape, U32),
        compiler_params=pltpu.CompilerParams(dimension_semantics=("arbitrary",),
                                             vmem_limit_bytes=VMEM_LIMIT),
        name="experts",
    )(mb_expert, mb_active, xb, w_gu, w_down, bg, bl, bd)


def _combine_kernel(y4_ref, gate_ref, x1_ref, nw_ref, o_ref):
    g = gate_ref[...]
    y = x1_ref[...]
    for k in range(TOP_K):
        y = y + g[:, k:k + 1] * _unpack_bf16_pairs(y4_ref[k])
    o_ref[...] = y * lax.rsqrt(jnp.mean(y * y, axis=-1, keepdims=True) + EPS) * nw_ref[...]


def _combine(y4, gate_t, x1, norm_w):
    t = x1.shape[0]
    tm = min(512, t)
    return pl.pallas_call(
        _combine_kernel,
        grid=(t // tm,),
        in_specs=[pl.BlockSpec((TOP_K, tm, D_MODEL // 2), lambda i: (0, i, 0)),
                  pl.BlockSpec((tm, TOP_K), lambda i: (i, 0)),
                  pl.BlockSpec((tm, D_MODEL), lambda i: (i, 0)),
                  pl.BlockSpec((1, D_MODEL), lambda i: (0, 0))],
        out_specs=pl.BlockSpec((tm, D_MODEL), lambda i: (i, 0)),
        out_shape=jax.ShapeDtypeStruct((t, D_MODEL), F32),
        compiler_params=pltpu.CompilerParams(dimension_semantics=("parallel",)),
        name="combine",
    )(y4, gate_t, x1, norm_w.reshape(1, D_MODEL))


def _moe(x1, hn, logits_t, w_gu, b_gu, w_down, b_down, norm_final):
    t = x1.shape[0]
    tt = min(512, t)
    eidx, gate, rank, base, cnt = _route(logits_t, tt)
    cnt = cnt[:, :, 0].astype(I32)
    dest, slot_tok, mb_expert, mb_active, n_mb = _routing_plan(cnt, eidx, rank, tt)
    xb = _sc_gather(hn, slot_tok)
    bg = b_gu[:, None, 0::2]
    bl = b_gu[:, None, 1::2]
    yb = _experts(xb, mb_expert, mb_active, w_gu, w_down, bg, bl, b_down[:, None, :], n_mb)
    y4 = _sc_gather(yb, dest.reshape(-1)).reshape(TOP_K, t, D_MODEL // 2)
    return _combine(y4, gate.T, x1, norm_final)


def kernel(x, norm_mix, w_in, gdn_conv, gdn_A_log, gdn_dt_bias, gdn_norm, rwkv_mu, rwkv_w0, rwkv_w2, rwkv_a0, rwkv_a2, rwkv_g2, rwkv_k_k, rwkv_k_a, rwkv_r_k, rwkv_ln_w, rwkv_ln_b, proj_a, proj_b, w_out, norm_ffn, router_w, router_b, w_gate_up, b_gate_up, w_down, b_down, norm_final):
    bsz, seq, d = x.shape
    depth = w_in.shape[0]
    x2 = x.reshape(bsz * seq, d)
    out = None
    for l in range(depth):
        w = w_in[l]
        w_main = jnp.concatenate([w[:, 0:4096], w[:, 4112:7184], w[:, 7440:9488]], axis=1).astype(BF16)
        w_aux = jnp.concatenate([w[:, 7184:7440], w[:, 4096:4112],
                                 jnp.zeros((d, AUX_COLS - LORA_COLS - 16), w.dtype)], axis=1).astype(BF16)
        p, p_aux = _in_proj(x2, norm_mix[l], w_main, w_aux)
        ya, yb = _mixers(
            p, p_aux, (gdn_conv[l], gdn_A_log[l], gdn_dt_bias[l], gdn_norm[l]),
            (rwkv_mu[l], rwkv_w0[l], rwkv_w2[l], rwkv_a0[l], rwkv_a2[l], rwkv_g2[l], rwkv_k_k[l], rwkv_k_a[l],
             rwkv_r_k[l], rwkv_ln_w[l], rwkv_ln_b[l]), bsz, seq)
        x1, hn, logits_t = _merge(x2, ya, yb, p, proj_a[l], proj_b[l], w_out[l], norm_ffn[l],
                                  router_w[l], router_b[l])
        assert l == depth - 1, "only the final layer's residual is fused with the output norm"
        out = _moe(x1, hn, logits_t, w_gate_up[l], b_gate_up[l], w_down[l], b_down[l], norm_final)
    return out.reshape(bsz, seq, d)
```

```python
import functools

import jax
import jax.numpy as jnp
import numpy as np
from jax import lax
from jax.experimental import pallas as pl
from jax.experimental.pallas import tpu as pltpu
from jax.experimental.pallas import tpu_sc as plsc

F32 = jnp.float32
BF16 = jnp.bfloat16
I32 = jnp.int32
U32 = jnp.uint32
HIGHEST = lax.Precision.HIGHEST

D_MODEL = 1024
EPS = 1e-6
CHUNK = 64
GDN_HEADS = 8
GDN_DK = 128
CONV_W = 4
RWKV_HEADS = 16
RWKV_N = 64
RWKV_GN_EPS = 64e-5
LORA_COLS = 256
N_EXPERTS = 32
TOP_K = 4
SWIGLU_ALPHA = 1.702
SWIGLU_LIMIT = 7.0

MAIN_COLS = 9216
AUX_COLS = 384
HEADS_PER_GROUP = 4
PACK_W = HEADS_PER_GROUP * CHUNK
INV_LEAF = 16
MIXER_TILE = 256
GDN_CHUNKS_PER_ITER = 4
RWKV_CHUNKS_PER_ITER = 4

SC_GATHER_ROWS = 64
SC_INDEX_CHUNK = 2048
MOE_PARTS = 2
EXPERT_BLOCK = 512
VMEM_LIMIT = 48 * 1024 * 1024


def _mm(a, b):
    return jnp.dot(a.astype(BF16), b.astype(BF16), preferred_element_type=F32)


def _mm_nt(a, b):
    return lax.dot_general(a.astype(BF16), b.astype(BF16), (((1,), (1,)), ((), ())),
                           preferred_element_type=F32)


def _mm_tn(a, b):
    return lax.dot_general(a.astype(BF16), b.astype(BF16), (((0,), (0,)), ((), ())),
                           preferred_element_type=F32)


def _split_bf16(x, terms):
    parts = []
    for _ in range(terms - 1):
        hi = x.astype(BF16)
        parts.append(hi)
        x = x - hi.astype(F32)
    parts.append(x.astype(BF16))
    return parts


def _mm_exact_lhs(a_bf16, b):
    out = None
    for part in _split_bf16(b, 3):
        d = jnp.dot(a_bf16, part, preferred_element_type=F32)
        out = d if out is None else out + d
    return out


def _pack_bf16_pairs(x):
    half = x.shape[1] // 2
    lo = lax.bitcast_convert_type(x[:, :half].astype(BF16).astype(F32), U32) >> 16
    hi = lax.bitcast_convert_type(x[:, half:].astype(BF16).astype(F32), U32) & jnp.uint32(0xFFFF0000)
    return lo | hi


def _unpack_bf16_pairs(w):
    lo = lax.bitcast_convert_type(w << 16, F32)
    hi = lax.bitcast_convert_type(w & jnp.uint32(0xFFFF0000), F32)
    return jnp.concatenate([lo, hi], axis=1)


def _iota(shape, dim):
    return lax.broadcasted_iota(I32, shape, dim)


def _softplus(x):
    return jnp.maximum(x, 0.0) + jnp.log(1.0 + jnp.exp(-jnp.abs(x)))


def _stack_rows(x, n):
    return jnp.concatenate([x] * n, axis=0)


def _tri_inv_packed(ms, eye, bdmask):
    def bd(x):
        return jnp.where(bdmask, _stack_rows(x, HEADS_PER_GROUP), jnp.zeros((), BF16))

    def mul(a, b):
        ah, al = _split_bf16(a, 2)
        bh, bl = _split_bf16(b, 2)
        bdh = bd(bh)
        return (jnp.dot(ah, bdh, preferred_element_type=F32) + jnp.dot(al, bdh, preferred_element_type=F32)
                + jnp.dot(ah, bd(bl), preferred_element_type=F32))

    lane = _iota((CHUNK, PACK_W), 1) % CHUNK
    leaf = (lane // INV_LEAF) == (_iota((CHUNK, PACK_W), 0) // INV_LEAF)
    ds = [jnp.where(leaf, m, 0.0) for m in ms]
    ls = [m - d for m, d in zip(ms, ds)]
    ts = [eye - d for d in ds]
    xs = [mul(d, d) for d in ds]
    for _ in range(2):
        rs = [mul(jnp.concatenate([t, x], axis=0), x) for t, x in zip(ts, xs)]
        ts = [t + r[:CHUNK] for t, r in zip(ts, rs)]
        xs = [r[CHUNK:] for r in rs]
    bs = [t + mul(t, x) for t, x in zip(ts, xs)]
    ns = [mul(b, l) for b, l in zip(bs, ls)]
    n2 = [mul(n, n) for n in ns]
    ps = [(eye - n) + mul(eye - n, q) for n, q in zip(ns, n2)]
    return [mul(p, b) for p, b in zip(ps, bs)]


def _shift_rows(x, prev8, k, row8):
    r = pltpu.roll(x, k, 0)
    pr = pltpu.roll(prev8, k, 0)
    head = jnp.where(row8 < k, pr, r[:8])
    return jnp.concatenate([head, r[8:]], axis=0)


def _in_proj_kernel(x_ref, nw_ref, w_ref, wa_ref, o_ref, oa_ref, h_ref):
    @pl.when(pl.program_id(1) == 0)
    def _():
        x = x_ref[...]
        y = x * lax.rsqrt(jnp.mean(x * x, axis=-1, keepdims=True) + EPS)
        h_ref[...] = (y * nw_ref[...]).astype(BF16)
        oa_ref[...] = jnp.dot(h_ref[...], wa_ref[...], preferred_element_type=F32)

    o_ref[...] = jnp.dot(h_ref[...], w_ref[...], preferred_element_type=F32).astype(BF16)


def _in_proj(x2, norm_w, w_main, w_aux):
    t = x2.shape[0]
    tm = min(1024, t)
    tn = 2304
    return pl.pallas_call(
        _in_proj_kernel,
        grid=(t // tm, MAIN_COLS // tn),
        in_specs=[pl.BlockSpec((tm, D_MODEL), lambda i, j: (i, 0)),
                  pl.BlockSpec((1, D_MODEL), lambda i, j: (0, 0)),
                  pl.BlockSpec((D_MODEL, tn), lambda i, j: (0, j)),
                  pl.BlockSpec((D_MODEL, AUX_COLS), lambda i, j: (0, 0))],
        out_specs=[pl.BlockSpec((tm, tn), lambda i, j: (i, j)),
                   pl.BlockSpec((tm, AUX_COLS), lambda i, j: (i, 0))],
        out_shape=[jax.ShapeDtypeStruct((t, MAIN_COLS), BF16), jax.ShapeDtypeStruct((t, AUX_COLS), F32)],
        scratch_shapes=[pltpu.VMEM((tm, D_MODEL), BF16)],
        compiler_params=pltpu.CompilerParams(dimension_semantics=("parallel", "arbitrary"),
                                             vmem_limit_bytes=VMEM_LIMIT),
        name="in_proj",
    )(x2, norm_w.reshape(1, D_MODEL), w_main, w_aux)


def _gdn_kernel(q_ref, k_ref, v_ref, z_ref, ba_ref, conv_ref, alog_ref, dtb_ref, nw_ref, o_ref,
                carry_ref, qn_ref, kn_ref, vv_ref, beta_ref, g_ref, s_ref, *, ts):
    @pl.when(pl.program_id(1) == 0)
    def _():
        carry_ref[...] = jnp.zeros_like(carry_ref)
        s_ref[...] = jnp.zeros_like(s_ref)

    row8 = _iota((8, D_MODEL), 0)

    def conv_silu(idx, x_ref):
        x = x_ref[...].astype(F32)
        prev8 = carry_ref[idx]
        w4 = conv_ref[:, idx * D_MODEL:(idx + 1) * D_MODEL]
        y = x * w4[CONV_W - 1:CONV_W]
        for k in range(1, CONV_W):
            y = y + _shift_rows(x, prev8, k, row8) * w4[CONV_W - 1 - k:CONV_W - k]
        carry_ref[idx] = x[ts - 8:ts]
        return y * jax.nn.sigmoid(y)

    def l2norm_heads(x, scale):
        parts = []
        for h in range(GDN_HEADS):
            xh = x[:, h * GDN_DK:(h + 1) * GDN_DK]
            parts.append(xh * (lax.rsqrt(jnp.sum(xh * xh, axis=-1, keepdims=True) + EPS) * scale))
        return jnp.concatenate(parts, axis=1)

    qn_ref[...] = l2norm_heads(conv_silu(0, q_ref), GDN_DK ** -0.5)
    kn_ref[...] = l2norm_heads(conv_silu(1, k_ref), 1.0)
    vv_ref[...] = conv_silu(2, v_ref)
    ba = ba_ref[...]
    beta_ref[...] = jax.nn.sigmoid(ba)
    g_ref[...] = -jnp.exp(alog_ref[...]) * _softplus(ba + dtb_ref[...])

    gw = HEADS_PER_GROUP * GDN_DK
    lane_p = _iota((CHUNK, PACK_W), 1)
    row_p = _iota((CHUNK, PACK_W), 0)
    eye = jnp.where((lane_p % CHUNK) == row_p, 1.0, 0.0).astype(F32)
    strict = (lane_p % CHUNK) < row_p
    incl = (lane_p % CHUNK) <= row_p
    bd_pp = (_iota((PACK_W, PACK_W), 0) // CHUNK) == (_iota((PACK_W, PACK_W), 1) // CHUNK)
    bd_pn = (_iota((PACK_W, gw), 0) // CHUNK) == (_iota((PACK_W, gw), 1) // GDN_DK)
    bd_pn2 = (_iota((PACK_W, 2 * gw), 0) // CHUNK) == ((_iota((PACK_W, 2 * gw), 1) % gw) // GDN_DK)
    ltri = jnp.where(_iota((CHUNK, CHUNK), 1) <= _iota((CHUNK, CHUNK), 0), 1.0, 0.0).astype(BF16)
    ones_cc = jnp.ones((CHUNK, CHUNK), BF16)
    lane128 = _iota((CHUNK, 128), 1)
    nw = nw_ref[...]

    n_groups = GDN_HEADS // HEADS_PER_GROUP
    hcols = [slice(j * GDN_DK, (j + 1) * GDN_DK) for j in range(HEADS_PER_GROUP)]

    def intra_chunk(rows_list):
        units = [(i, gi) for i in range(len(rows_list)) for gi in range(n_groups)]
        gcs = [_mm_exact_lhs(ltri, g_ref[rows, :]) for rows in rows_list]
        beta = [beta_ref[rows, :] for rows in rows_list]
        pre = []
        for i, gi in units:
            rows = rows_list[i]
            heads = [gi * HEADS_PER_GROUP + j for j in range(HEADS_PER_GROUP)]
            cols = slice(gi * gw, (gi + 1) * gw)
            beta_nat = jnp.concatenate(
                [jnp.broadcast_to(beta[i][:, h:h + 1], (CHUNK, GDN_DK)) for h in heads], axis=1)
            gcol = [jnp.broadcast_to(gcs[i][:, 8 + h:9 + h], (CHUNK, GDN_DK)) for h in heads]
            gcol_p = jnp.concatenate([jnp.where(lane128 < CHUNK, gcol[0], gcol[1]),
                                      jnp.where(lane128 < CHUNK, gcol[2], gcol[3])], axis=1)
            pre.append(dict(qn=qn_ref[rows, cols], kn=kn_ref[rows, cols], vv=vv_ref[rows, cols],
                            beta_nat=beta_nat, gc_nat=jnp.concatenate(gcol, axis=1), gcol_p=gcol_p))
        grow = [_mm_exact_lhs(ones_cc, u["gcol_p"] * eye) for u in pre]
        aas = [_mm_nt(jnp.concatenate([u["kn"] * u["beta_nat"], u["qn"]], axis=0),
                      jnp.where(bd_pn, _stack_rows(u["kn"], HEADS_PER_GROUP), 0.0)) for u in pre]
        decay = [jnp.exp(jnp.minimum(u["gcol_p"] - gr, 0.0)) for u, gr in zip(pre, grow)]
        t_inv = _tri_inv_packed([jnp.where(strict, aa[:CHUNK] * d, 0.0) for aa, d in zip(aas, decay)],
                                eye, bd_pp)
        out = []
        for u, aa, d, ti in zip(pre, aas, decay, t_inv):
            egc = jnp.exp(u["gc_nat"])
            data = jnp.concatenate([u["vv"] * u["beta_nat"], u["kn"] * u["beta_nat"] * egc], axis=1)
            uw = _mm(ti, jnp.where(bd_pn2, _stack_rows(data, HEADS_PER_GROUP), 0.0))
            gl = u["gc_nat"][CHUNK - 1:CHUNK, :]
            out.append(dict(uw=uw, attn=jnp.where(incl, aa[CHUNK:] * d, 0.0), qd=u["qn"] * egc,
                            kd=u["kn"] * jnp.exp(gl - u["gc_nat"]), egl=jnp.exp(gl)))
        return out

    def recurrence(rows, group_units):
        wq = {}
        for gi, u in enumerate(group_units):
            for j, hc in enumerate(hcols):
                h = gi * HEADS_PER_GROUP + j
                wq[h] = _mm(jnp.concatenate([u["uw"][:, gw + j * GDN_DK:gw + (j + 1) * GDN_DK],
                                             u["qd"][:, hc]], axis=0), s_ref[h])
        vnew = {}
        for gi, u in enumerate(group_units):
            for j, hc in enumerate(hcols):
                h = gi * HEADS_PER_GROUP + j
                vnew[h] = u["uw"][:, hc] - wq[h][:CHUNK]
                s_ref[h] = s_ref[h] * u["egl"][:, hc] + _mm_tn(u["kd"][:, hc], vnew[h])
        for gi, u in enumerate(group_units):
            heads = [gi * HEADS_PER_GROUP + j for j in range(HEADS_PER_GROUP)]
            vn = jnp.concatenate([vnew[h] for h in heads], axis=1)
            o = jnp.concatenate([wq[h][CHUNK:] for h in heads], axis=1) + _mm(
                u["attn"], jnp.where(bd_pn, _stack_rows(vn, HEADS_PER_GROUP), 0.0))
            for j, h in enumerate(heads):
                oh = o[:, hcols[j]]
                oh = oh * lax.rsqrt(jnp.mean(oh * oh, axis=-1, keepdims=True) + EPS) * nw
                zh = z_ref[rows, h * GDN_DK:(h + 1) * GDN_DK].astype(F32)
                o_ref[rows, h * GDN_DK:(h + 1) * GDN_DK] = (oh * (zh * jax.nn.sigmoid(zh))).astype(BF16)

    def chunk_pair_body(c, carry):
        rows_list = [pl.ds(pl.multiple_of((c * GDN_CHUNKS_PER_ITER + i) * CHUNK, CHUNK), CHUNK)
                     for i in range(GDN_CHUNKS_PER_ITER)]
        units = intra_chunk(rows_list)
        for i, rows in enumerate(rows_list):
            recurrence(rows, units[i * n_groups:(i + 1) * n_groups])
        return carry

    lax.fori_loop(0, ts // (CHUNK * GDN_CHUNKS_PER_ITER), chunk_pair_body, 0)


def _gdn_parts(p, p_aux, conv_w, a_log, dt_bias, norm_w, ts, ns):
    pad8 = lambda v: jnp.zeros((1, 128), F32).at[0, 8:16].set(v)
    tok = lambda col: pl.BlockSpec((ts, D_MODEL), lambda b, s, col=col: (b * ns + s, col))
    const = lambda shape: pl.BlockSpec(shape, lambda b, s: (0,) * len(shape))
    in_specs = [tok(0), tok(1), tok(2), tok(3),
                pl.BlockSpec((ts, 128), lambda b, s: (b * ns + s, LORA_COLS // 128)),
                const((CONV_W, 3 * D_MODEL)), const((1, 128)), const((1, 128)), const((1, GDN_DK))]
    operands = [p, p, p, p, p_aux, conv_w, pad8(a_log), pad8(dt_bias), norm_w.reshape(1, GDN_DK)]
    scratch = [pltpu.VMEM((3, 8, D_MODEL), F32),
               pltpu.VMEM((ts, D_MODEL), F32), pltpu.VMEM((ts, D_MODEL), F32), pltpu.VMEM((ts, D_MODEL), F32),
               pltpu.VMEM((ts, 128), F32), pltpu.VMEM((ts, 128), F32),
               pltpu.VMEM((GDN_HEADS, GDN_DK, GDN_DK), F32)]
    return in_specs, operands, scratch


def _rwkv_kernel(r_ref, k_ref, v_ref, lo_ref, mu_ref, w0_ref, w2_ref, a0_ref, a2_ref, g2_ref,
                 kk_ref, ka_ref, rk_ref, lnw_ref, lnb_ref, seg_ref, o_ref,
                 carry_ref, rs_ref, ks_ref, vs_ref, kks_ref, bs_ref, ld_ref, gg_ref, os_ref, st_ref,
                 *, ts):
    @pl.when(pl.program_id(1) == 0)
    def _():
        carry_ref[...] = jnp.zeros_like(carry_ref)
        st_ref[...] = jnp.zeros_like(st_ref)

    width = D_MODEL
    row8 = _iota((8, width), 0)
    row8l = _iota((8, LORA_COLS), 0)

    def seg_sum(x):
        xb = x.astype(BF16)
        return jnp.concatenate(
            [jnp.dot(xb[:, g * PACK_W:(g + 1) * PACK_W], seg_ref[...], preferred_element_type=F32)
             for g in range(width // PACK_W)], axis=1)

    def lerp(idx, x_ref, mu, r8):
        x = x_ref[...].astype(F32)
        xs = x + (_shift_rows(x, carry_ref[idx, :, :x.shape[1]], 1, r8) - x) * mu
        carry_ref[idx, :, :x.shape[1]] = x[ts - 8:ts]
        return xs

    r = lerp(0, r_ref, mu_ref[:, 0:width], row8)
    k = lerp(1, k_ref, mu_ref[:, width:2 * width], row8)
    v = lerp(2, v_ref, mu_ref[:, 2 * width:3 * width], row8)
    lo = lerp(3, lo_ref, mu_ref[:, 3 * width:3 * width + LORA_COLS], row8l)
    lo_a = lo[:, :128]
    w_log = -_softplus(-(w0_ref[...] + _mm(jnp.tanh(lo_a), w2_ref[...]))) - 0.5
    ld_ref[...] = -jnp.exp(w_log)
    aa = jax.nn.sigmoid(a0_ref[...] + _mm(lo_a, a2_ref[...]))
    gg_ref[...] = _mm(jax.nn.sigmoid(lo[:, 128:]), g2_ref[...])
    kx = k * kk_ref[...]
    kk = kx * lax.rsqrt(seg_sum(kx * kx) + EPS)
    k = k * (1.0 + (aa - 1.0) * ka_ref[...])
    rs_ref[...] = r
    ks_ref[...] = k
    vs_ref[...] = v
    kks_ref[...] = kk
    bs_ref[...] = kk * aa

    lane_p = _iota((CHUNK, PACK_W), 1)
    row_p = _iota((CHUNK, PACK_W), 0)
    eye = jnp.where((lane_p % CHUNK) == row_p, 1.0, 0.0).astype(F32)
    strict = (lane_p % CHUNK) < row_p
    incl = (lane_p % CHUNK) <= row_p
    bd = (_iota((PACK_W, PACK_W), 0) // CHUNK) == (_iota((PACK_W, PACK_W), 1) // CHUNK)
    bd2 = jnp.concatenate([bd, bd], axis=1)
    ltri = jnp.where(_iota((CHUNK, CHUNK), 1) <= _iota((CHUNK, CHUNK), 0), 1.0, 0.0).astype(BF16)

    def bdiag(x):
        return jnp.where(bd, _stack_rows(x, HEADS_PER_GROUP), 0.0)

    n_groups = width // PACK_W
    gcols = [slice(g * PACK_W, (g + 1) * PACK_W) for g in range(n_groups)]

    def intra_chunk(rows_list):
        units = [(rows, cols) for rows in rows_list for cols in gcols]
        ld = [ld_ref[rows, cols] for rows, cols in units]
        cs = [_mm_exact_lhs(ltri, x) for x in ld]
        pre = []
        for g, (rows, cols) in enumerate(units):
            cl = cs[g][CHUNK - 1:CHUNK, :]
            e_neg = jnp.exp(-cs[g])
            e_dec = jnp.exp(cl - cs[g])
            kc, bc = ks_ref[rows, cols], bs_ref[rows, cols]
            rt = rs_ref[rows, cols] * jnp.exp(cs[g])
            kt = kks_ref[rows, cols] * jnp.exp(cs[g] - ld[g])
            pre.append(dict(cl=cl, rt=rt, kt=kt, vc=vs_ref[rows, cols], kh=kc * e_neg, bh=bc * e_neg,
                            kdec=kc * e_dec, bdec=bc * e_dec))
        akb = [_mm_nt(jnp.concatenate([u["kt"], u["rt"]], axis=0),
                      jnp.concatenate([bdiag(u["kh"]), bdiag(u["bh"])], axis=0)) for u in pre]
        t_inv = _tri_inv_packed([jnp.where(strict, a[:CHUNK, PACK_W:], 0.0) for a in akb], eye, bd)
        xo = [_mm(jnp.concatenate([jnp.where(strict, a[:CHUNK, :PACK_W], 0.0),
                                   jnp.where(incl, a[CHUNK:, :PACK_W], 0.0)], axis=0), bdiag(u["vc"]))
              for a, u in zip(akb, pre)]
        tt = [_mm(ti, jnp.where(bd2, _stack_rows(jnp.concatenate([x[:CHUNK], u["kt"]], axis=1),
                                                 HEADS_PER_GROUP), 0.0))
              for ti, x, u in zip(t_inv, xo, pre)]
        rr = [_mm(jnp.where(incl, a[CHUNK:, PACK_W:], 0.0),
                  jnp.where(bd2, _stack_rows(jnp.concatenate([t[:, PACK_W:], t[:, :PACK_W]], axis=1),
                                             HEADS_PER_GROUP), 0.0))
              for a, t in zip(akb, tt)]
        return [dict(u, tk=t[:, PACK_W:], u0=t[:, :PACK_W], rq=u["rt"] - r[:, :PACK_W],
                     o1=x[CHUNK:] - r[:, PACK_W:]) for u, x, t, r in zip(pre, xo, tt, rr)]

    def recurrence(rows, group_units):
        uo = [_mm_nt(jnp.concatenate([u["tk"], u["rq"]], axis=0), st_ref[g])
              for g, u in enumerate(group_units)]
        upd = []
        for g, (cols, u) in enumerate(zip(gcols, group_units)):
            os_ref[rows, cols] = u["o1"] + uo[g][CHUNK:]
            upd.append(_mm_tn(jnp.concatenate([u["vc"], u["u0"] + uo[g][:CHUNK]], axis=0),
                              jnp.concatenate([u["kdec"], -u["bdec"]], axis=0)))
        for g, u in enumerate(group_units):
            st_ref[g] = st_ref[g] * jnp.exp(u["cl"]) + jnp.where(bd, upd[g], 0.0)

    def chunks_body(c, carry):
        rows_list = [pl.ds(pl.multiple_of((c * RWKV_CHUNKS_PER_ITER + i) * CHUNK, CHUNK), CHUNK)
                     for i in range(RWKV_CHUNKS_PER_ITER)]
        units = intra_chunk(rows_list)
        for i, rows in enumerate(rows_list):
            recurrence(rows, units[i * n_groups:(i + 1) * n_groups])
        return carry

    lax.fori_loop(0, ts // (CHUNK * RWKV_CHUNKS_PER_ITER), chunks_body, 0)

    o = os_ref[...]
    inv_n = 1.0 / RWKV_N
    mean = seg_sum(o) * inv_n
    cen = o - mean
    var = seg_sum(cen * cen) * inv_n
    o = cen * lax.rsqrt(var + RWKV_GN_EPS) * lnw_ref[...] + lnb_ref[...]
    bonus = seg_sum(rs_ref[...] * ks_ref[...] * rk_ref[...]) * vs_ref[...]
    o_ref[...] = ((o + bonus) * gg_ref[...]).astype(BF16)


def _rwkv_parts(p, p_aux, mu, w0, w2, a0, a2, g2, k_k, k_a, r_k, ln_w, ln_b, ts, ns):
    width = D_MODEL
    row = lambda v: v.reshape(1, -1)
    w2p = jnp.concatenate([w2, jnp.zeros_like(w2)], axis=0).astype(BF16)
    a2p = jnp.concatenate([jnp.zeros_like(a2), a2], axis=0).astype(BF16)
    seg = (np.arange(PACK_W)[:, None] // RWKV_N == np.arange(PACK_W)[None, :] // RWKV_N)
    seg = jnp.asarray(seg, BF16)
    tok = lambda col: pl.BlockSpec((ts, width), lambda b, s, col=col: (b * ns + s, col))
    const = lambda shape: pl.BlockSpec(shape, lambda b, s: (0,) * len(shape))
    fbuf = lambda: pltpu.VMEM((ts, width), F32)
    in_specs = [tok(4), tok(5), tok(6),
                pl.BlockSpec((ts, LORA_COLS), lambda b, s: (b * ns + s, 0)),
                const((1, 3 * width + LORA_COLS)), const((1, width)), const((128, width)),
                const((1, width)), const((128, width)), const((128, width)),
                const((1, width)), const((1, width)), const((1, width)), const((1, width)),
                const((1, width)), const((PACK_W, PACK_W))]
    operands = [p, p, p, p_aux, row(mu), row(w0), w2p, row(a0), a2p, g2.astype(BF16), row(k_k), row(k_a),
                row(r_k), row(ln_w), row(ln_b), seg]
    scratch = [pltpu.VMEM((4, 8, width), F32),
               fbuf(), fbuf(), fbuf(), fbuf(), fbuf(), fbuf(), fbuf(), fbuf(),
               pltpu.VMEM((width // PACK_W, PACK_W, PACK_W), F32)]
    return in_specs, operands, scratch


def _mixer_call(body, name, parts, bsz, seq, ts):
    in_specs, operands, scratch = parts
    ns = seq // ts
    return pl.pallas_call(
        functools.partial(body, ts=ts),
        grid=(bsz, ns),
        in_specs=in_specs,
        out_specs=pl.BlockSpec((ts, D_MODEL), lambda b, s: (b * ns + s, 0)),
        out_shape=jax.ShapeDtypeStruct((bsz * seq, D_MODEL), BF16),
        scratch_shapes=scratch,
        compiler_params=pltpu.CompilerParams(dimension_semantics=("parallel", "arbitrary"),
                                             vmem_limit_bytes=VMEM_LIMIT),
        name=name,
    )(*operands)


def _mixers(p, p_aux, gdn_params, rwkv_params, bsz, seq):
    ts = min(MIXER_TILE, seq)
    ns = seq // ts
    ya = _mixer_call(_gdn_kernel, "gdn", _gdn_parts(p, p_aux, *gdn_params, ts, ns), bsz, seq, ts)
    yb = _mixer_call(_rwkv_kernel, "rwkv", _rwkv_parts(p, p_aux, *rwkv_params, ts, ns), bsz, seq, ts)
    return ya, yb


def _merge_kernel(x_ref, ya_ref, yb_ref, ga_ref, gb_ref, pa_ref, pb_ref, wo_ref, nw_ref, rw_ref, rb_ref,
                  x1_ref, hn_ref, lg_ref):
    merged = (jax.nn.sigmoid(ga_ref[...].astype(F32)) * jnp.dot(ya_ref[...], pa_ref[...], preferred_element_type=F32)
              + jax.nn.sigmoid(gb_ref[...].astype(F32)) * jnp.dot(yb_ref[...], pb_ref[...], preferred_element_type=F32))
    x1 = x_ref[...] + _mm(merged, wo_ref[...])
    x1_ref[...] = x1
    hn = x1 * lax.rsqrt(jnp.mean(x1 * x1, axis=-1, keepdims=True) + EPS) * nw_ref[...]
    hn_ref[...] = _pack_bf16_pairs(hn)
    lg_ref[...] = lax.dot_general(rw_ref[...], hn, (((1,), (1,)), ((), ())),
                                  preferred_element_type=F32, precision=HIGHEST) + rb_ref[...]


def _merge(x2, ya, yb, p, proj_a, proj_b, w_out, norm_w, router_w, router_b):
    t = x2.shape[0]
    tm = min(512, t)
    tok = lambda col: pl.BlockSpec((tm, D_MODEL), lambda i, col=col: (i, col))
    const = lambda shape: pl.BlockSpec(shape, lambda i: (0,) * len(shape))
    return pl.pallas_call(
        _merge_kernel,
        grid=(t // tm,),
        in_specs=[tok(0), tok(0), tok(0), tok(7), tok(8),
                  const((D_MODEL, D_MODEL)), const((D_MODEL, D_MODEL)), const((D_MODEL, D_MODEL)),
                  const((1, D_MODEL)), const((N_EXPERTS, D_MODEL)), const((N_EXPERTS, 1))],
        out_specs=[tok(0), pl.BlockSpec((tm, D_MODEL // 2), lambda i: (i, 0)),
                   pl.BlockSpec((N_EXPERTS, tm), lambda i: (0, i))],
        out_shape=[jax.ShapeDtypeStruct((t, D_MODEL), F32), jax.ShapeDtypeStruct((t, D_MODEL // 2), U32),
                   jax.ShapeDtypeStruct((N_EXPERTS, t), F32)],
        compiler_params=pltpu.CompilerParams(dimension_semantics=("parallel",),
                                             vmem_limit_bytes=VMEM_LIMIT),
        name="merge",
    )(x2, ya, yb, p, p, proj_a.astype(BF16), proj_b.astype(BF16), w_out.astype(BF16),
      norm_w.reshape(1, D_MODEL), router_w.T, router_b.reshape(N_EXPERTS, 1))


def _route_kernel(lg_ref, eidx_ref, gate_ref, rank_ref, base_ref, cnt_ref, carry_ref, *, tt):
    @pl.when(pl.program_id(0) == 0)
    def _():
        carry_ref[...] = jnp.zeros_like(carry_ref)

    l = lg_ref[...]
    ie = _iota((N_EXPERTS, tt), 0)
    vals, hots, idxs = [], [], []
    for _ in range(TOP_K):
        m = jnp.max(l, axis=0, keepdims=True)
        idx = jnp.min(jnp.where(l == m, ie, N_EXPERTS), axis=0, keepdims=True)
        hot = ie == idx
        vals.append(m)
        hots.append(hot)
        idxs.append(idx)
        l = jnp.where(hot, -jnp.inf, l)
    exps = [jnp.exp(v - vals[0]) for v in vals]
    den = exps[0] + exps[1] + exps[2] + exps[3]
    gate_ref[...] = jnp.concatenate([e / den for e in exps], axis=0)
    eidx_ref[...] = jnp.concatenate(idxs, axis=0)

    sel = jnp.zeros((N_EXPERTS, tt), F32)
    for hot in hots:
        sel = sel + jnp.where(hot, 1.0, 0.0)
    before = jnp.where(_iota((tt, tt), 0) < _iota((tt, tt), 1), 1.0, 0.0).astype(BF16)
    carry = carry_ref[...]
    prefix = jnp.dot(sel.astype(BF16), before, preferred_element_type=F32) + carry[:, 0:1]
    rank_ref[...] = jnp.concatenate(
        [jnp.sum(jnp.where(hot, prefix, 0.0), axis=0, keepdims=True) for hot in hots], axis=0).astype(I32)
    cnt = jnp.broadcast_to(jnp.sum(sel, axis=1, keepdims=True), (N_EXPERTS, 128))
    base_ref[0] = carry
    cnt_ref[0] = cnt
    carry_ref[...] = carry + cnt


def _route(logits_t, tt):
    t = logits_t.shape[1]
    nt = t // tt
    row4 = pl.BlockSpec((TOP_K, tt), lambda i: (0, i))
    per_tile = pl.BlockSpec((1, N_EXPERTS, 128), lambda i: (i, 0, 0))
    return pl.pallas_call(
        functools.partial(_route_kernel, tt=tt),
        grid=(nt,),
        in_specs=[pl.BlockSpec((N_EXPERTS, tt), lambda i: (0, i))],
        out_specs=[row4, row4, row4, per_tile, per_tile],
        out_shape=[jax.ShapeDtypeStruct((TOP_K, t), I32), jax.ShapeDtypeStruct((TOP_K, t), F32),
                   jax.ShapeDtypeStruct((TOP_K, t), I32),
                   jax.ShapeDtypeStruct((nt, N_EXPERTS, 128), F32),
                   jax.ShapeDtypeStruct((nt, N_EXPERTS, 128), F32)],
        scratch_shapes=[pltpu.VMEM((N_EXPERTS, 128), F32)],
        compiler_params=pltpu.CompilerParams(dimension_semantics=("arbitrary",)),
        name="route",
    )(logits_t)


def _count_le(sorted_vals, queries):
    return jnp.sum((sorted_vals[None, :] <= queries[:, None]).astype(I32), axis=1)


def _routing_plan(cnt, eidx, rank, tt):
    t = cnt.shape[0] * tt
    n_mb = (t * TOP_K) // EXPERT_BLOCK + N_EXPERTS
    counts = jnp.sum(cnt, axis=0)
    padded = ((counts + EXPERT_BLOCK - 1) // EXPERT_BLOCK) * EXPERT_BLOCK
    end_pad = jnp.cumsum(padded)
    start_pad = end_pad - padded
    hot = eidx[:, :, None] == jnp.arange(N_EXPERTS, dtype=I32)[None, None, :]
    dest = jnp.sum(jnp.where(hot, start_pad[None, None, :], 0), axis=-1) + rank

    mb_start = jnp.arange(n_mb, dtype=I32) * EXPERT_BLOCK
    mb_expert = jnp.minimum(_count_le(end_pad, mb_start), N_EXPERTS - 1).astype(I32)
    mb_active = (mb_start < end_pad[-1]).astype(I32)

    dest = dest.astype(I32)
    return dest, _sc_slot_tokens(dest, n_mb * EXPERT_BLOCK), mb_expert, mb_active, n_mb


def _sc_slot_tokens(dest, n_slots):
    info = plsc.get_sparse_core_info()
    nc, ns, nl = info.num_cores, info.num_subcores, info.num_lanes
    top_k, t = dest.shape
    per_worker = n_slots // (nc * ns)
    assert per_worker * nc * ns == n_slots and per_worker % nl == 0 and t % SC_INDEX_CHUNK == 0
    fill_mask = (1 << (t.bit_length() - 1)) - 1
    mesh = plsc.VectorSubcoreMesh(core_axis_name="c", subcore_axis_name="s")

    @functools.partial(
        pl.kernel, mesh=mesh,
        out_type=jax.ShapeDtypeStruct((n_slots,), I32),
        scratch_types=[pltpu.VMEM((per_worker,), I32), pltpu.VMEM((SC_INDEX_CHUNK,), I32)],
        compiler_params=pltpu.CompilerParams(needs_layout_passes=False),
    )
    def invert(dest_hbm, out_hbm, local_v, chunk_v):
        lo = (lax.axis_index("s") * nc + lax.axis_index("c")) * per_worker
        lane = lax.iota(I32, nl)

        @pl.loop(0, per_worker // nl)
        def _(j):
            local_v[pl.ds(j * nl, nl)] = (lo + j * nl + lane) & fill_mask

        for k in range(top_k):
            @pl.loop(0, t // SC_INDEX_CHUNK)
            def _(c):
                pltpu.sync_copy(dest_hbm.at[k, pl.ds(c * SC_INDEX_CHUNK, SC_INDEX_CHUNK)], chunk_v)

                @pl.loop(0, SC_INDEX_CHUNK // nl)
                def _(j):
                    d = chunk_v[pl.ds(j * nl, nl)] - lo
                    tok = c * SC_INDEX_CHUNK + j * nl + lane
                    plsc.store_scatter(local_v, [d], tok, mask=(d >= 0) & (d < per_worker))

        pltpu.sync_copy(local_v, out_hbm.at[pl.ds(lo, per_worker)])

    return invert(dest)


def _sc_gather(table, idx):
    info = plsc.get_sparse_core_info()
    nc, ns = info.num_cores, info.num_subcores
    n_rows, width = idx.shape[0], table.shape[1]
    per_worker = n_rows // (nc * ns)
    steps = per_worker // SC_GATHER_ROWS
    assert per_worker * nc * ns == n_rows and steps * SC_GATHER_ROWS == per_worker and steps % 2 == 0
    mesh = plsc.VectorSubcoreMesh(core_axis_name="c", subcore_axis_name="s")

    @functools.partial(
        pl.kernel, mesh=mesh,
        out_type=jax.ShapeDtypeStruct((n_rows, width), table.dtype),
        scratch_types=[pltpu.VMEM((steps, SC_GATHER_ROWS), I32),
                       pltpu.VMEM((2, SC_GATHER_ROWS, width), table.dtype),
                       pltpu.SemaphoreType.DMA((2,))],
    )
    def gather(table_hbm, idx_hbm, out_hbm, idx_v, rows_v, sems):
        worker = lax.axis_index("s") * nc + lax.axis_index("c")
        first = worker * steps
        pltpu.sync_copy(idx_hbm.at[worker], idx_v)

        def gather_copy(step, buf):
            return pltpu.make_async_copy(table_hbm.at[idx_v.at[step]], rows_v.at[buf], sems.at[buf])

        gather_copy(0, 0).start()

        @pl.loop(0, steps, step=2)
        def _(i):
            for buf in range(2):
                cur = i + buf

                @pl.when(cur + 1 < steps)
                def _():
                    gather_copy(cur + 1, 1 - buf).start()

                gather_copy(cur, buf).wait()
                row0 = pl.multiple_of((first + cur) * SC_GATHER_ROWS, 8)
                pltpu.sync_copy(rows_v.at[buf], out_hbm.at[pl.ds(row0, SC_GATHER_ROWS)])

    return gather(table, idx.reshape(nc * ns, steps, SC_GATHER_ROWS))


def _expert_kernel(e_ref, act_ref, x_ref, wgu_ref, wd_ref, bg_ref, bl_ref, bd_ref, *rest, first_block):
    o_ref, wg_c, wl_c, wd_c = rest[-4:]
    mb = pl.program_id(0) + first_block
    new_expert = jnp.logical_or(pl.program_id(0) == 0, e_ref[mb] != e_ref[jnp.maximum(mb - 1, 0)])

    @pl.when(jnp.logical_and(new_expert, act_ref[mb] == 1))
    def _():
        lane = _iota((D_MODEL, 128), 1)
        half = lane < 64
        idx = jnp.where(half, 2 * lane, 2 * (lane - 64) + 1)
        for m in range(D_MODEL // 128):
            a = jnp.take_along_axis(wgu_ref[0, :, (2 * m) * 128:(2 * m + 1) * 128], idx, axis=1)
            b = jnp.take_along_axis(wgu_ref[0, :, (2 * m + 1) * 128:(2 * m + 2) * 128], idx, axis=1)
            cols = slice(m * 128, (m + 1) * 128)
            wg_c[:, cols] = jnp.where(half, a, pltpu.roll(b, 64, 1)).astype(BF16)
            wl_c[:, cols] = jnp.where(half, pltpu.roll(a, 64, 1), b).astype(BF16)
        wd_c[...] = wd_ref[0].astype(BF16)

    @pl.when(act_ref[mb] == 0)
    def _():
        o_ref[...] = jnp.zeros_like(o_ref)

    @pl.when(act_ref[mb] == 1)
    def _():
        x = _unpack_bf16_pairs(x_ref[...]).astype(BF16)
        glu = jnp.dot(x, wg_c[...], preferred_element_type=F32) + bg_ref[0]
        lin = jnp.dot(x, wl_c[...], preferred_element_type=F32) + bl_ref[0]
        glu = jnp.minimum(glu, SWIGLU_LIMIT)
        lin = jnp.clip(lin, -SWIGLU_LIMIT, SWIGLU_LIMIT)
        act = glu * jax.nn.sigmoid(SWIGLU_ALPHA * glu) * (lin + 1.0)
        o_ref[...] = _pack_bf16_pairs(_mm(act, wd_c[...]) + bd_ref[0])


def _experts(xb_part, first_block, n_mb, prev_out, mb_expert, mb_active, w_gu, w_down, bg, bl, bd):
    d_ff = w_down.shape[1]
    assert d_ff == D_MODEL and w_gu.shape[1:] == (D_MODEL, 2 * d_ff)
    off = first_block
    bspec = pl.BlockSpec((1, 1, D_MODEL), lambda m, e, a: (e[m + off], 0, 0))
    wcache = pltpu.VMEM((D_MODEL, D_MODEL), BF16)
    in_specs = [pl.BlockSpec((EXPERT_BLOCK, D_MODEL // 2), lambda m, e, a: (m, 0)),
                pl.BlockSpec((1, D_MODEL, 2 * d_ff), lambda m, e, a: (e[m + off], 0, 0)),
                pl.BlockSpec((1, d_ff, D_MODEL), lambda m, e, a: (e[m + off], 0, 0)),
                bspec, bspec, bspec]
    operands = [mb_expert, mb_active, xb_part, w_gu, w_down, bg, bl, bd]
    aliases = {}
    if prev_out is not None:
        in_specs.append(pl.BlockSpec(memory_space=pl.ANY))
        aliases = {len(operands): 0}
        operands.append(prev_out)
    return pl.pallas_call(
        functools.partial(_expert_kernel, first_block=off),
        grid_spec=pltpu.PrefetchScalarGridSpec(
            num_scalar_prefetch=2,
            grid=(xb_part.shape[0] // EXPERT_BLOCK,),
            in_specs=in_specs,
            out_specs=pl.BlockSpec((EXPERT_BLOCK, D_MODEL // 2), lambda m, e, a: (m + off, 0)),
            scratch_shapes=[wcache, wcache, wcache]),
        out_shape=jax.ShapeDtypeStruct((n_mb * EXPERT_BLOCK, D_MODEL // 2), U32),
        input_output_aliases=aliases,
        compiler_params=pltpu.CompilerParams(dimension_semantics=("arbitrary",),
                                             vmem_limit_bytes=VMEM_LIMIT),
        name="experts",
    )(*operands)


def _combine_kernel(y4_ref, gate_ref, x1_ref, nw_ref, *rest):
    o_ref = rest[-1]
    g = gate_ref[...]
    y = x1_ref[...]
    for k in range(TOP_K):
        y = y + g[:, k:k + 1] * _unpack_bf16_pairs(y4_ref[k])
    o_ref[...] = y * lax.rsqrt(jnp.mean(y * y, axis=-1, keepdims=True) + EPS) * nw_ref[...]


def _combine(y4_part, first_token, prev_out, gate_t, x1, norm_w):
    t = x1.shape[0]
    tm = min(512, t)
    off = first_token // tm
    in_specs = [pl.BlockSpec((TOP_K, tm, D_MODEL // 2), lambda i: (0, i, 0)),
                pl.BlockSpec((tm, TOP_K), lambda i: (i + off, 0)),
                pl.BlockSpec((tm, D_MODEL), lambda i: (i + off, 0)),
                pl.BlockSpec((1, D_MODEL), lambda i: (0, 0))]
    operands = [y4_part, gate_t, x1, norm_w.reshape(1, D_MODEL)]
    aliases = {}
    if prev_out is not None:
        in_specs.append(pl.BlockSpec(memory_space=pl.ANY))
        aliases = {len(operands): 0}
        operands.append(prev_out)
    return pl.pallas_call(
        _combine_kernel,
        grid=(y4_part.shape[1] // tm,),
        in_specs=in_specs,
        out_specs=pl.BlockSpec((tm, D_MODEL), lambda i: (i + off, 0)),
        out_shape=jax.ShapeDtypeStruct((t, D_MODEL), F32),
        input_output_aliases=aliases,
        compiler_params=pltpu.CompilerParams(dimension_semantics=("parallel",)),
        name="combine",
    )(*operands)


def _moe(x1, hn, logits_t, w_gu, b_gu, w_down, b_down, norm_final):
    t = x1.shape[0]
    tt = min(512, t)
    eidx, gate, rank, base, cnt = _route(logits_t, tt)
    cnt = cnt[:, :, 0].astype(I32)
    dest, slot_tok, mb_expert, mb_active, n_mb = _routing_plan(cnt, eidx, rank, tt)
    bg = b_gu[:, None, 0::2]
    bl = b_gu[:, None, 1::2]
    mb_per = n_mb // MOE_PARTS
    assert mb_per * MOE_PARTS == n_mb and t % (MOE_PARTS * tt) == 0
    yb = None
    for part in range(MOE_PARTS):
        slots = slice(part * mb_per * EXPERT_BLOCK, (part + 1) * mb_per * EXPERT_BLOCK)
        xb = _sc_gather(hn, slot_tok[slots])
        yb = _experts(xb, part * mb_per, n_mb, yb, mb_expert, mb_active, w_gu, w_down, bg, bl,
                      b_down[:, None, :])
    out = None
    t_per = t // MOE_PARTS
    for part in range(MOE_PARTS):
        toks = slice(part * t_per, (part + 1) * t_per)
        y4 = _sc_gather(yb, dest[:, toks].reshape(-1)).reshape(TOP_K, t_per, D_MODEL // 2)
        out = _combine(y4, part * t_per, out, gate.T, x1, norm_final)
    return out


def kernel(x, norm_mix, w_in, gdn_conv, gdn_A_log, gdn_dt_bias, gdn_norm, rwkv_mu, rwkv_w0, rwkv_w2, rwkv_a0, rwkv_a2, rwkv_g2, rwkv_k_k, rwkv_k_a, rwkv_r_k, rwkv_ln_w, rwkv_ln_b, proj_a, proj_b, w_out, norm_ffn, router_w, router_b, w_gate_up, b_gate_up, w_down, b_down, norm_final):
    bsz, seq, d = x.shape
    depth = w_in.shape[0]
    x2 = x.reshape(bsz * seq, d)
    out = None
    for l in range(depth):
        w = w_in[l]
        w_main = jnp.concatenate([w[:, 0:4096], w[:, 4112:7184], w[:, 7440:9488]], axis=1).astype(BF16)
        w_aux = jnp.concatenate([w[:, 7184:7440], w[:, 4096:4112],
                                 jnp.zeros((d, AUX_COLS - LORA_COLS - 16), w.dtype)], axis=1).astype(BF16)
        p, p_aux = _in_proj(x2, norm_mix[l], w_main, w_aux)
        ya, yb = _mixers(
            p, p_aux, (gdn_conv[l], gdn_A_log[l], gdn_dt_bias[l], gdn_norm[l]),
            (rwkv_mu[l], rwkv_w0[l], rwkv_w2[l], rwkv_a0[l], rwkv_a2[l], rwkv_g2[l], rwkv_k_k[l], rwkv_k_a[l],
             rwkv_r_k[l], rwkv_ln_w[l], rwkv_ln_b[l]), bsz, seq)
        x1, hn, logits_t = _merge(x2, ya, yb, p, proj_a[l], proj_b[l], w_out[l], norm_ffn[l],
                                  router_w[l], router_b[l])
        assert l == depth - 1, "only the final layer's residual is fused with the output norm"
        out = _moe(x1, hn, logits_t, w_gate_up[l], b_gate_up[l], w_down[l], b_down[l], norm_final)
    return out.reshape(bsz, seq, d)
```

```python
import functools

import jax
import jax.numpy as jnp
import numpy as np
from jax import lax
from jax.experimental import pallas as pl
from jax.experimental.pallas import tpu as pltpu
from jax.experimental.pallas import tpu_sc as plsc

F32 = jnp.float32
BF16 = jnp.bfloat16
I32 = jnp.int32
U32 = jnp.uint32
HIGHEST = lax.Precision.HIGHEST

D_MODEL = 1024
EPS = 1e-6
CHUNK = 64
GDN_HEADS = 8
GDN_DK = 128
CONV_W = 4
RWKV_HEADS = 16
RWKV_N = 64
RWKV_GN_EPS = 64e-5
LORA_COLS = 256
N_EXPERTS = 32
TOP_K = 4
SWIGLU_ALPHA = 1.702
SWIGLU_LIMIT = 7.0

MAIN_COLS = 9216
AUX_COLS = 384
HEADS_PER_GROUP = 4
PACK_W = HEADS_PER_GROUP * CHUNK
INV_LEAF = 16
MIXER_TILE = 256
GDN_CHUNKS_PER_ITER = 4
RWKV_CHUNKS_PER_ITER = 4

SC_GATHER_ROWS = 64
SC_INDEX_CHUNK = 2048
EXPERT_BLOCK = 512
VMEM_LIMIT = 48 * 1024 * 1024


def _mm(a, b):
    return jnp.dot(a.astype(BF16), b.astype(BF16), preferred_element_type=F32)


def _mm_nt(a, b):
    return lax.dot_general(a.astype(BF16), b.astype(BF16), (((1,), (1,)), ((), ())),
                           preferred_element_type=F32)


def _mm_tn(a, b):
    return lax.dot_general(a.astype(BF16), b.astype(BF16), (((0,), (0,)), ((), ())),
                           preferred_element_type=F32)


def _split_bf16(x, terms):
    parts = []
    for _ in range(terms - 1):
        hi = x.astype(BF16)
        parts.append(hi)
        x = x - hi.astype(F32)
    parts.append(x.astype(BF16))
    return parts


def _mm_exact_lhs(a_bf16, b):
    out = None
    for part in _split_bf16(b, 3):
        d = jnp.dot(a_bf16, part, preferred_element_type=F32)
        out = d if out is None else out + d
    return out


def _pack_bf16_pairs(x):
    half = x.shape[1] // 2
    lo = lax.bitcast_convert_type(x[:, :half].astype(BF16).astype(F32), U32) >> 16
    hi = lax.bitcast_convert_type(x[:, half:].astype(BF16).astype(F32), U32) & jnp.uint32(0xFFFF0000)
    return lo | hi


def _unpack_bf16_pairs(w):
    lo = lax.bitcast_convert_type(w << 16, F32)
    hi = lax.bitcast_convert_type(w & jnp.uint32(0xFFFF0000), F32)
    return jnp.concatenate([lo, hi], axis=1)


def _iota(shape, dim):
    return lax.broadcasted_iota(I32, shape, dim)


def _softplus(x):
    return jnp.maximum(x, 0.0) + jnp.log(1.0 + jnp.exp(-jnp.abs(x)))


def _stack_rows(x, n):
    return jnp.concatenate([x] * n, axis=0)


def _tri_inv_packed(ms, eye, bdmask):
    def bd(x):
        return jnp.where(bdmask, _stack_rows(x, HEADS_PER_GROUP), jnp.zeros((), BF16))

    def mul(a, b):
        ah, al = _split_bf16(a, 2)
        bh, bl = _split_bf16(b, 2)
        bdh = bd(bh)
        return (jnp.dot(ah, bdh, preferred_element_type=F32) + jnp.dot(al, bdh, preferred_element_type=F32)
                + jnp.dot(ah, bd(bl), preferred_element_type=F32))

    def mul_bf16(a, b):
        return jnp.dot(a.astype(BF16), bd(b.astype(BF16)), preferred_element_type=F32)

    lane = _iota((CHUNK, PACK_W), 1) % CHUNK
    leaf = (lane // INV_LEAF) == (_iota((CHUNK, PACK_W), 0) // INV_LEAF)
    ds = [jnp.where(leaf, m, 0.0) for m in ms]
    ls = [m - d for m, d in zip(ms, ds)]
    ts = [eye - d for d in ds]
    xs = [mul(d, d) for d in ds]
    for _ in range(2):
        rs = [mul(jnp.concatenate([t, x], axis=0), x) for t, x in zip(ts, xs)]
        ts = [t + r[:CHUNK] for t, r in zip(ts, rs)]
        xs = [r[CHUNK:] for r in rs]
    bs = [t + mul(t, x) for t, x in zip(ts, xs)]
    ns = [mul_bf16(b, l) for b, l in zip(bs, ls)]
    n2 = [mul_bf16(n, n) for n in ns]
    ps = [(eye - n) + mul_bf16(eye - n, q) for n, q in zip(ns, n2)]
    return [mul_bf16(p, b) for p, b in zip(ps, bs)]


def _shift_rows(x, prev8, k, row8):
    r = pltpu.roll(x, k, 0)
    pr = pltpu.roll(prev8, k, 0)
    head = jnp.where(row8 < k, pr, r[:8])
    return jnp.concatenate([head, r[8:]], axis=0)


def _in_proj_kernel(x_ref, nw_ref, w_ref, wa_ref, o_ref, oa_ref, h_ref):
    @pl.when(pl.program_id(1) == 0)
    def _():
        x = x_ref[...]
        y = x * lax.rsqrt(jnp.mean(x * x, axis=-1, keepdims=True) + EPS)
        h_ref[...] = (y * nw_ref[...]).astype(BF16)
        oa_ref[...] = jnp.dot(h_ref[...], wa_ref[...], preferred_element_type=F32)

    o_ref[...] = jnp.dot(h_ref[...], w_ref[...], preferred_element_type=F32).astype(BF16)


def _in_proj(x2, norm_w, w_main, w_aux):
    t = x2.shape[0]
    tm = min(1024, t)
    tn = 2304
    return pl.pallas_call(
        _in_proj_kernel,
        grid=(t // tm, MAIN_COLS // tn),
        in_specs=[pl.BlockSpec((tm, D_MODEL), lambda i, j: (i, 0)),
                  pl.BlockSpec((1, D_MODEL), lambda i, j: (0, 0)),
                  pl.BlockSpec((D_MODEL, tn), lambda i, j: (0, j)),
                  pl.BlockSpec((D_MODEL, AUX_COLS), lambda i, j: (0, 0))],
        out_specs=[pl.BlockSpec((tm, tn), lambda i, j: (i, j)),
                   pl.BlockSpec((tm, AUX_COLS), lambda i, j: (i, 0))],
        out_shape=[jax.ShapeDtypeStruct((t, MAIN_COLS), BF16), jax.ShapeDtypeStruct((t, AUX_COLS), F32)],
        scratch_shapes=[pltpu.VMEM((tm, D_MODEL), BF16)],
        compiler_params=pltpu.CompilerParams(dimension_semantics=("parallel", "arbitrary"),
                                             vmem_limit_bytes=VMEM_LIMIT),
        name="in_proj",
    )(x2, norm_w.reshape(1, D_MODEL), w_main, w_aux)


def _gdn_kernel(q_ref, k_ref, v_ref, z_ref, ba_ref, conv_ref, alog_ref, dtb_ref, nw_ref, o_ref,
                carry_ref, qn_ref, kn_ref, vv_ref, beta_ref, g_ref, s_ref, *, ts):
    @pl.when(pl.program_id(1) == 0)
    def _():
        carry_ref[...] = jnp.zeros_like(carry_ref)
        s_ref[...] = jnp.zeros_like(s_ref)

    row8 = _iota((8, D_MODEL), 0)

    def conv_silu(idx, x_ref):
        x = x_ref[...].astype(F32)
        prev8 = carry_ref[idx]
        w4 = conv_ref[:, idx * D_MODEL:(idx + 1) * D_MODEL]
        y = x * w4[CONV_W - 1:CONV_W]
        for k in range(1, CONV_W):
            y = y + _shift_rows(x, prev8, k, row8) * w4[CONV_W - 1 - k:CONV_W - k]
        carry_ref[idx] = x[ts - 8:ts]
        return y * jax.nn.sigmoid(y)

    def l2norm_heads(x, scale):
        parts = []
        for h in range(GDN_HEADS):
            xh = x[:, h * GDN_DK:(h + 1) * GDN_DK]
            parts.append(xh * (lax.rsqrt(jnp.sum(xh * xh, axis=-1, keepdims=True) + EPS) * scale))
        return jnp.concatenate(parts, axis=1)

    qn_ref[...] = l2norm_heads(conv_silu(0, q_ref), GDN_DK ** -0.5)
    kn_ref[...] = l2norm_heads(conv_silu(1, k_ref), 1.0)
    vv_ref[...] = conv_silu(2, v_ref)
    ba = ba_ref[...]
    beta_ref[...] = jax.nn.sigmoid(ba)
    g_ref[...] = -jnp.exp(alog_ref[...]) * _softplus(ba + dtb_ref[...])

    gw = HEADS_PER_GROUP * GDN_DK
    lane_p = _iota((CHUNK, PACK_W), 1)
    row_p = _iota((CHUNK, PACK_W), 0)
    eye = jnp.where((lane_p % CHUNK) == row_p, 1.0, 0.0).astype(F32)
    strict = (lane_p % CHUNK) < row_p
    incl = (lane_p % CHUNK) <= row_p
    bd_pp = (_iota((PACK_W, PACK_W), 0) // CHUNK) == (_iota((PACK_W, PACK_W), 1) // CHUNK)
    bd_pn = (_iota((PACK_W, gw), 0) // CHUNK) == (_iota((PACK_W, gw), 1) // GDN_DK)
    bd_pn2 = (_iota((PACK_W, 2 * gw), 0) // CHUNK) == ((_iota((PACK_W, 2 * gw), 1) % gw) // GDN_DK)
    ltri = jnp.where(_iota((CHUNK, CHUNK), 1) <= _iota((CHUNK, CHUNK), 0), 1.0, 0.0).astype(BF16)
    ones_cc = jnp.ones((CHUNK, CHUNK), BF16)
    lane128 = _iota((CHUNK, 128), 1)
    nw = nw_ref[...]

    n_groups = GDN_HEADS // HEADS_PER_GROUP
    hcols = [slice(j * GDN_DK, (j + 1) * GDN_DK) for j in range(HEADS_PER_GROUP)]

    def intra_chunk(rows_list):
        units = [(i, gi) for i in range(len(rows_list)) for gi in range(n_groups)]
        gcs = [_mm_exact_lhs(ltri, g_ref[rows, :]) for rows in rows_list]
        beta = [beta_ref[rows, :] for rows in rows_list]
        pre = []
        for i, gi in units:
            rows = rows_list[i]
            heads = [gi * HEADS_PER_GROUP + j for j in range(HEADS_PER_GROUP)]
            cols = slice(gi * gw, (gi + 1) * gw)
            beta_nat = jnp.concatenate(
                [jnp.broadcast_to(beta[i][:, h:h + 1], (CHUNK, GDN_DK)) for h in heads], axis=1)
            gcol = [jnp.broadcast_to(gcs[i][:, 8 + h:9 + h], (CHUNK, GDN_DK)) for h in heads]
            gcol_p = jnp.concatenate([jnp.where(lane128 < CHUNK, gcol[0], gcol[1]),
                                      jnp.where(lane128 < CHUNK, gcol[2], gcol[3])], axis=1)
            pre.append(dict(qn=qn_ref[rows, cols], kn=kn_ref[rows, cols], vv=vv_ref[rows, cols],
                            beta_nat=beta_nat, gc_nat=jnp.concatenate(gcol, axis=1), gcol_p=gcol_p))
        grow = [_mm_exact_lhs(ones_cc, u["gcol_p"] * eye) for u in pre]
        aas = [_mm_nt(jnp.concatenate([u["kn"] * u["beta_nat"], u["qn"]], axis=0),
                      jnp.where(bd_pn, _stack_rows(u["kn"], HEADS_PER_GROUP), 0.0)) for u in pre]
        decay = [jnp.exp(jnp.minimum(u["gcol_p"] - gr, 0.0)) for u, gr in zip(pre, grow)]
        t_inv = _tri_inv_packed([jnp.where(strict, aa[:CHUNK] * d, 0.0) for aa, d in zip(aas, decay)],
                                eye, bd_pp)
        out = []
        for u, aa, d, ti in zip(pre, aas, decay, t_inv):
            egc = jnp.exp(u["gc_nat"])
            data = jnp.concatenate([u["vv"] * u["beta_nat"], u["kn"] * u["beta_nat"] * egc], axis=1)
            uw = _mm(ti, jnp.where(bd_pn2, _stack_rows(data, HEADS_PER_GROUP), 0.0))
            gl = u["gc_nat"][CHUNK - 1:CHUNK, :]
            out.append(dict(uw=uw, attn=jnp.where(incl, aa[CHUNK:] * d, 0.0), qd=u["qn"] * egc,
                            kd=u["kn"] * jnp.exp(gl - u["gc_nat"]), egl=jnp.exp(gl)))
        return out

    def recurrence(rows, group_units):
        wq = {}
        for gi, u in enumerate(group_units):
            for j, hc in enumerate(hcols):
                h = gi * HEADS_PER_GROUP + j
                wq[h] = _mm(jnp.concatenate([u["uw"][:, gw + j * GDN_DK:gw + (j + 1) * GDN_DK],
                                             u["qd"][:, hc]], axis=0), s_ref[h])
        vnew = {}
        for gi, u in enumerate(group_units):
            for j, hc in enumerate(hcols):
                h = gi * HEADS_PER_GROUP + j
                vnew[h] = u["uw"][:, hc] - wq[h][:CHUNK]
                s_ref[h] = s_ref[h] * u["egl"][:, hc] + _mm_tn(u["kd"][:, hc], vnew[h])
        for gi, u in enumerate(group_units):
            heads = [gi * HEADS_PER_GROUP + j for j in range(HEADS_PER_GROUP)]
            vn = jnp.concatenate([vnew[h] for h in heads], axis=1)
            o = jnp.concatenate([wq[h][CHUNK:] for h in heads], axis=1) + _mm(
                u["attn"], jnp.where(bd_pn, _stack_rows(vn, HEADS_PER_GROUP), 0.0))
            for j, h in enumerate(heads):
                oh = o[:, hcols[j]]
                oh = oh * lax.rsqrt(jnp.mean(oh * oh, axis=-1, keepdims=True) + EPS) * nw
                zh = z_ref[rows, h * GDN_DK:(h + 1) * GDN_DK].astype(F32)
                o_ref[rows, h * GDN_DK:(h + 1) * GDN_DK] = (oh * (zh * jax.nn.sigmoid(zh))).astype(BF16)

    def chunk_pair_body(c, carry):
        rows_list = [pl.ds(pl.multiple_of((c * GDN_CHUNKS_PER_ITER + i) * CHUNK, CHUNK), CHUNK)
                     for i in range(GDN_CHUNKS_PER_ITER)]
        units = intra_chunk(rows_list)
        for i, rows in enumerate(rows_list):
            recurrence(rows, units[i * n_groups:(i + 1) * n_groups])
        return carry

    lax.fori_loop(0, ts // (CHUNK * GDN_CHUNKS_PER_ITER), chunk_pair_body, 0)


def _gdn_parts(p, p_aux, conv_w, a_log, dt_bias, norm_w, ts, ns):
    pad8 = lambda v: jnp.zeros((1, 128), F32).at[0, 8:16].set(v)
    tok = lambda col: pl.BlockSpec((ts, D_MODEL), lambda b, s, col=col: (b * ns + s, col))
    const = lambda shape: pl.BlockSpec(shape, lambda b, s: (0,) * len(shape))
    in_specs = [tok(0), tok(1), tok(2), tok(3),
                pl.BlockSpec((ts, 128), lambda b, s: (b * ns + s, LORA_COLS // 128)),
                const((CONV_W, 3 * D_MODEL)), const((1, 128)), const((1, 128)), const((1, GDN_DK))]
    operands = [p, p, p, p, p_aux, conv_w, pad8(a_log), pad8(dt_bias), norm_w.reshape(1, GDN_DK)]
    scratch = [pltpu.VMEM((3, 8, D_MODEL), F32),
               pltpu.VMEM((ts, D_MODEL), F32), pltpu.VMEM((ts, D_MODEL), F32), pltpu.VMEM((ts, D_MODEL), F32),
               pltpu.VMEM((ts, 128), F32), pltpu.VMEM((ts, 128), F32),
               pltpu.VMEM((GDN_HEADS, GDN_DK, GDN_DK), F32)]
    return in_specs, operands, scratch


def _rwkv_kernel(r_ref, k_ref, v_ref, lo_ref, mu_ref, w0_ref, w2_ref, a0_ref, a2_ref, g2_ref,
                 kk_ref, ka_ref, rk_ref, lnw_ref, lnb_ref, seg_ref, o_ref,
                 carry_ref, rs_ref, ks_ref, vs_ref, kks_ref, bs_ref, ld_ref, gg_ref, os_ref, st_ref,
                 *, ts):
    @pl.when(pl.program_id(1) == 0)
    def _():
        carry_ref[...] = jnp.zeros_like(carry_ref)
        st_ref[...] = jnp.zeros_like(st_ref)

    width = D_MODEL
    row8 = _iota((8, width), 0)
    row8l = _iota((8, LORA_COLS), 0)

    def seg_sum(x):
        xb = x.astype(BF16)
        return jnp.concatenate(
            [jnp.dot(xb[:, g * PACK_W:(g + 1) * PACK_W], seg_ref[...], preferred_element_type=F32)
             for g in range(width // PACK_W)], axis=1)

    def lerp(idx, x_ref, mu, r8):
        x = x_ref[...].astype(F32)
        xs = x + (_shift_rows(x, carry_ref[idx, :, :x.shape[1]], 1, r8) - x) * mu
        carry_ref[idx, :, :x.shape[1]] = x[ts - 8:ts]
        return xs

    r = lerp(0, r_ref, mu_ref[:, 0:width], row8)
    k = lerp(1, k_ref, mu_ref[:, width:2 * width], row8)
    v = lerp(2, v_ref, mu_ref[:, 2 * width:3 * width], row8)
    lo = lerp(3, lo_ref, mu_ref[:, 3 * width:3 * width + LORA_COLS], row8l)
    lo_a = lo[:, :128]
    w_log = -_softplus(-(w0_ref[...] + _mm(jnp.tanh(lo_a), w2_ref[...]))) - 0.5
    ld_ref[...] = -jnp.exp(w_log)
    aa = jax.nn.sigmoid(a0_ref[...] + _mm(lo_a, a2_ref[...]))
    gg_ref[...] = _mm(jax.nn.sigmoid(lo[:, 128:]), g2_ref[...])
    kx = k * kk_ref[...]
    kk = kx * lax.rsqrt(seg_sum(kx * kx) + EPS)
    k = k * (1.0 + (aa - 1.0) * ka_ref[...])
    rs_ref[...] = r
    ks_ref[...] = k
    vs_ref[...] = v
    kks_ref[...] = kk
    bs_ref[...] = kk * aa

    lane_p = _iota((CHUNK, PACK_W), 1)
    row_p = _iota((CHUNK, PACK_W), 0)
    eye = jnp.where((lane_p % CHUNK) == row_p, 1.0, 0.0).astype(F32)
    strict = (lane_p % CHUNK) < row_p
    incl = (lane_p % CHUNK) <= row_p
    bd = (_iota((PACK_W, PACK_W), 0) // CHUNK) == (_iota((PACK_W, PACK_W), 1) // CHUNK)
    bd2 = jnp.concatenate([bd, bd], axis=1)
    ltri = jnp.where(_iota((CHUNK, CHUNK), 1) <= _iota((CHUNK, CHUNK), 0), 1.0, 0.0).astype(BF16)

    def bdiag(x):
        return jnp.where(bd, _stack_rows(x, HEADS_PER_GROUP), 0.0)

    n_groups = width // PACK_W
    gcols = [slice(g * PACK_W, (g + 1) * PACK_W) for g in range(n_groups)]

    def intra_chunk(rows_list):
        units = [(rows, cols) for rows in rows_list for cols in gcols]
        ld = [ld_ref[rows, cols] for rows, cols in units]
        cs = [_mm_exact_lhs(ltri, x) for x in ld]
        pre = []
        for g, (rows, cols) in enumerate(units):
            cl = cs[g][CHUNK - 1:CHUNK, :]
            e_neg = jnp.exp(-cs[g])
            e_dec = jnp.exp(cl - cs[g])
            kc, bc = ks_ref[rows, cols], bs_ref[rows, cols]
            rt = rs_ref[rows, cols] * jnp.exp(cs[g])
            kt = kks_ref[rows, cols] * jnp.exp(cs[g] - ld[g])
            pre.append(dict(cl=cl, rt=rt, kt=kt, vc=vs_ref[rows, cols], kh=kc * e_neg, bh=bc * e_neg,
                            kdec=kc * e_dec, bdec=bc * e_dec))
        akb = [_mm_nt(jnp.concatenate([u["kt"], u["rt"]], axis=0),
                      jnp.concatenate([bdiag(u["kh"]), bdiag(u["bh"])], axis=0)) for u in pre]
        t_inv = _tri_inv_packed([jnp.where(strict, a[:CHUNK, PACK_W:], 0.0) for a in akb], eye, bd)
        xo = [_mm(jnp.concatenate([jnp.where(strict, a[:CHUNK, :PACK_W], 0.0),
                                   jnp.where(incl, a[CHUNK:, :PACK_W], 0.0)], axis=0), bdiag(u["vc"]))
              for a, u in zip(akb, pre)]
        tt = [_mm(ti, jnp.where(bd2, _stack_rows(jnp.concatenate([x[:CHUNK], u["kt"]], axis=1),
                                                 HEADS_PER_GROUP), 0.0))
              for ti, x, u in zip(t_inv, xo, pre)]
        rr = [_mm(jnp.where(incl, a[CHUNK:, PACK_W:], 0.0),
                  jnp.where(bd2, _stack_rows(jnp.concatenate([t[:, PACK_W:], t[:, :PACK_W]], axis=1),
                                             HEADS_PER_GROUP), 0.0))
              for a, t in zip(akb, tt)]
        return [dict(u, tk=t[:, PACK_W:], u0=t[:, :PACK_W], rq=u["rt"] - r[:, :PACK_W],
                     o1=x[CHUNK:] - r[:, PACK_W:]) for u, x, t, r in zip(pre, xo, tt, rr)]

    def recurrence(rows, group_units):
        uo = [_mm_nt(jnp.concatenate([u["tk"], u["rq"]], axis=0), st_ref[g])
              for g, u in enumerate(group_units)]
        upd = []
        for g, (cols, u) in enumerate(zip(gcols, group_units)):
            os_ref[rows, cols] = u["o1"] + uo[g][CHUNK:]
            upd.append(_mm_tn(jnp.concatenate([u["vc"], u["u0"] + uo[g][:CHUNK]], axis=0),
                              jnp.concatenate([u["kdec"], -u["bdec"]], axis=0)))
        for g, u in enumerate(group_units):
            st_ref[g] = st_ref[g] * jnp.exp(u["cl"]) + jnp.where(bd, upd[g], 0.0)

    def chunks_body(c, carry):
        rows_list = [pl.ds(pl.multiple_of((c * RWKV_CHUNKS_PER_ITER + i) * CHUNK, CHUNK), CHUNK)
                     for i in range(RWKV_CHUNKS_PER_ITER)]
        units = intra_chunk(rows_list)
        for i, rows in enumerate(rows_list):
            recurrence(rows, units[i * n_groups:(i + 1) * n_groups])
        return carry

    lax.fori_loop(0, ts // (CHUNK * RWKV_CHUNKS_PER_ITER), chunks_body, 0)

    o = os_ref[...]
    inv_n = 1.0 / RWKV_N
    mean = seg_sum(o) * inv_n
    cen = o - mean
    var = seg_sum(cen * cen) * inv_n
    o = cen * lax.rsqrt(var + RWKV_GN_EPS) * lnw_ref[...] + lnb_ref[...]
    bonus = seg_sum(rs_ref[...] * ks_ref[...] * rk_ref[...]) * vs_ref[...]
    o_ref[...] = ((o + bonus) * gg_ref[...]).astype(BF16)


def _rwkv_parts(p, p_aux, mu, w0, w2, a0, a2, g2, k_k, k_a, r_k, ln_w, ln_b, ts, ns):
    width = D_MODEL
    row = lambda v: v.reshape(1, -1)
    w2p = jnp.concatenate([w2, jnp.zeros_like(w2)], axis=0).astype(BF16)
    a2p = jnp.concatenate([jnp.zeros_like(a2), a2], axis=0).astype(BF16)
    seg = (np.arange(PACK_W)[:, None] // RWKV_N == np.arange(PACK_W)[None, :] // RWKV_N)
    seg = jnp.asarray(seg, BF16)
    tok = lambda col: pl.BlockSpec((ts, width), lambda b, s, col=col: (b * ns + s, col))
    const = lambda shape: pl.BlockSpec(shape, lambda b, s: (0,) * len(shape))
    fbuf = lambda: pltpu.VMEM((ts, width), F32)
    in_specs = [tok(4), tok(5), tok(6),
                pl.BlockSpec((ts, LORA_COLS), lambda b, s: (b * ns + s, 0)),
                const((1, 3 * width + LORA_COLS)), const((1, width)), const((128, width)),
                const((1, width)), const((128, width)), const((128, width)),
                const((1, width)), const((1, width)), const((1, width)), const((1, width)),
                const((1, width)), const((PACK_W, PACK_W))]
    operands = [p, p, p, p_aux, row(mu), row(w0), w2p, row(a0), a2p, g2.astype(BF16), row(k_k), row(k_a),
                row(r_k), row(ln_w), row(ln_b), seg]
    scratch = [pltpu.VMEM((4, 8, width), F32),
               fbuf(), fbuf(), fbuf(), fbuf(), fbuf(), fbuf(), fbuf(), fbuf(),
               pltpu.VMEM((width // PACK_W, PACK_W, PACK_W), F32)]
    return in_specs, operands, scratch


def _mixer_call(body, name, parts, bsz, seq, ts):
    in_specs, operands, scratch = parts
    ns = seq // ts
    return pl.pallas_call(
        functools.partial(body, ts=ts),
        grid=(bsz, ns),
        in_specs=in_specs,
        out_specs=pl.BlockSpec((ts, D_MODEL), lambda b, s: (b * ns + s, 0)),
        out_shape=jax.ShapeDtypeStruct((bsz * seq, D_MODEL), BF16),
        scratch_shapes=scratch,
        compiler_params=pltpu.CompilerParams(dimension_semantics=("parallel", "arbitrary"),
                                             vmem_limit_bytes=VMEM_LIMIT),
        name=name,
    )(*operands)


def _mixers(p, p_aux, gdn_params, rwkv_params, bsz, seq):
    ts = min(MIXER_TILE, seq)
    ns = seq // ts
    ya = _mixer_call(_gdn_kernel, "gdn", _gdn_parts(p, p_aux, *gdn_params, ts, ns), bsz, seq, ts)
    yb = _mixer_call(_rwkv_kernel, "rwkv", _rwkv_parts(p, p_aux, *rwkv_params, ts, ns), bsz, seq, ts)
    return ya, yb


def _merge_kernel(x_ref, ya_ref, yb_ref, ga_ref, gb_ref, pa_ref, pb_ref, wo_ref, nw_ref, rw_ref, rb_ref,
                  x1_ref, hn_ref, lg_ref):
    merged = (jax.nn.sigmoid(ga_ref[...].astype(F32)) * jnp.dot(ya_ref[...], pa_ref[...], preferred_element_type=F32)
              + jax.nn.sigmoid(gb_ref[...].astype(F32)) * jnp.dot(yb_ref[...], pb_ref[...], preferred_element_type=F32))
    x1 = x_ref[...] + _mm(merged, wo_ref[...])
    x1_ref[...] = x1
    hn = x1 * lax.rsqrt(jnp.mean(x1 * x1, axis=-1, keepdims=True) + EPS) * nw_ref[...]
    hn_ref[...] = _pack_bf16_pairs(hn)
    lg_ref[...] = lax.dot_general(rw_ref[...], hn, (((1,), (1,)), ((), ())),
                                  preferred_element_type=F32, precision=HIGHEST) + rb_ref[...]


def _merge(x2, ya, yb, p, proj_a, proj_b, w_out, norm_w, router_w, router_b):
    t = x2.shape[0]
    tm = min(512, t)
    tok = lambda col: pl.BlockSpec((tm, D_MODEL), lambda i, col=col: (i, col))
    const = lambda shape: pl.BlockSpec(shape, lambda i: (0,) * len(shape))
    return pl.pallas_call(
        _merge_kernel,
        grid=(t // tm,),
        in_specs=[tok(0), tok(0), tok(0), tok(7), tok(8),
                  const((D_MODEL, D_MODEL)), const((D_MODEL, D_MODEL)), const((D_MODEL, D_MODEL)),
                  const((1, D_MODEL)), const((N_EXPERTS, D_MODEL)), const((N_EXPERTS, 1))],
        out_specs=[tok(0), pl.BlockSpec((tm, D_MODEL // 2), lambda i: (i, 0)),
                   pl.BlockSpec((N_EXPERTS, tm), lambda i: (0, i))],
        out_shape=[jax.ShapeDtypeStruct((t, D_MODEL), F32), jax.ShapeDtypeStruct((t, D_MODEL // 2), U32),
                   jax.ShapeDtypeStruct((N_EXPERTS, t), F32)],
        compiler_params=pltpu.CompilerParams(dimension_semantics=("parallel",),
                                             vmem_limit_bytes=VMEM_LIMIT),
        name="merge",
    )(x2, ya, yb, p, p, proj_a.astype(BF16), proj_b.astype(BF16), w_out.astype(BF16),
      norm_w.reshape(1, D_MODEL), router_w.T, router_b.reshape(N_EXPERTS, 1))


def _route_kernel(lg_ref, eidx_ref, gate_ref, rank_ref, base_ref, cnt_ref, carry_ref, *, tt):
    @pl.when(pl.program_id(0) == 0)
    def _():
        carry_ref[...] = jnp.zeros_like(carry_ref)

    l = lg_ref[...]
    ie = _iota((N_EXPERTS, tt), 0)
    vals, hots, idxs = [], [], []
    for _ in range(TOP_K):
        m = jnp.max(l, axis=0, keepdims=True)
        idx = jnp.min(jnp.where(l == m, ie, N_EXPERTS), axis=0, keepdims=True)
        hot = ie == idx
        vals.append(m)
        hots.append(hot)
        idxs.append(idx)
        l = jnp.where(hot, -jnp.inf, l)
    exps = [jnp.exp(v - vals[0]) for v in vals]
    den = exps[0] + exps[1] + exps[2] + exps[3]
    gate_ref[...] = jnp.concatenate([e / den for e in exps], axis=0)
    eidx_ref[...] = jnp.concatenate(idxs, axis=0)

    sel = jnp.zeros((N_EXPERTS, tt), F32)
    for hot in hots:
        sel = sel + jnp.where(hot, 1.0, 0.0)
    before = jnp.where(_iota((tt, tt), 0) < _iota((tt, tt), 1), 1.0, 0.0).astype(BF16)
    carry = carry_ref[...]
    prefix = jnp.dot(sel.astype(BF16), before, preferred_element_type=F32) + carry[:, 0:1]
    rank_ref[...] = jnp.concatenate(
        [jnp.sum(jnp.where(hot, prefix, 0.0), axis=0, keepdims=True) for hot in hots], axis=0).astype(I32)
    cnt = jnp.broadcast_to(jnp.sum(sel, axis=1, keepdims=True), (N_EXPERTS, 128))
    base_ref[0] = carry
    cnt_ref[0] = cnt
    carry_ref[...] = carry + cnt


def _route(logits_t, tt):
    t = logits_t.shape[1]
    nt = t // tt
    row4 = pl.BlockSpec((TOP_K, tt), lambda i: (0, i))
    per_tile = pl.BlockSpec((1, N_EXPERTS, 128), lambda i: (i, 0, 0))
    return pl.pallas_call(
        functools.partial(_route_kernel, tt=tt),
        grid=(nt,),
        in_specs=[pl.BlockSpec((N_EXPERTS, tt), lambda i: (0, i))],
        out_specs=[row4, row4, row4, per_tile, per_tile],
        out_shape=[jax.ShapeDtypeStruct((TOP_K, t), I32), jax.ShapeDtypeStruct((TOP_K, t), F32),
                   jax.ShapeDtypeStruct((TOP_K, t), I32),
                   jax.ShapeDtypeStruct((nt, N_EXPERTS, 128), F32),
                   jax.ShapeDtypeStruct((nt, N_EXPERTS, 128), F32)],
        scratch_shapes=[pltpu.VMEM((N_EXPERTS, 128), F32)],
        compiler_params=pltpu.CompilerParams(dimension_semantics=("arbitrary",)),
        name="route",
    )(logits_t)


def _count_le(sorted_vals, queries):
    return jnp.sum((sorted_vals[None, :] <= queries[:, None]).astype(I32), axis=1)


def _routing_plan(cnt, eidx, rank, tt):
    t = cnt.shape[0] * tt
    n_mb = (t * TOP_K) // EXPERT_BLOCK + N_EXPERTS
    counts = jnp.sum(cnt, axis=0)
    padded = ((counts + EXPERT_BLOCK - 1) // EXPERT_BLOCK) * EXPERT_BLOCK
    end_pad = jnp.cumsum(padded)
    start_pad = end_pad - padded
    hot = eidx[:, :, None] == jnp.arange(N_EXPERTS, dtype=I32)[None, None, :]
    dest = jnp.sum(jnp.where(hot, start_pad[None, None, :], 0), axis=-1) + rank

    mb_start = jnp.arange(n_mb, dtype=I32) * EXPERT_BLOCK
    mb_expert = jnp.minimum(_count_le(end_pad, mb_start), N_EXPERTS - 1).astype(I32)
    mb_active = (mb_start < end_pad[-1]).astype(I32)

    dest = dest.astype(I32)
    return dest, _sc_slot_tokens(dest, n_mb * EXPERT_BLOCK), mb_expert, mb_active, n_mb


def _sc_slot_tokens(dest, n_slots):
    info = plsc.get_sparse_core_info()
    nc, ns, nl = info.num_cores, info.num_subcores, info.num_lanes
    top_k, t = dest.shape
    per_worker = n_slots // (nc * ns)
    assert per_worker * nc * ns == n_slots and per_worker % nl == 0 and t % SC_INDEX_CHUNK == 0
    fill_mask = (1 << (t.bit_length() - 1)) - 1
    mesh = plsc.VectorSubcoreMesh(core_axis_name="c", subcore_axis_name="s")

    @functools.partial(
        pl.kernel, mesh=mesh,
        out_type=jax.ShapeDtypeStruct((n_slots,), I32),
        scratch_types=[pltpu.VMEM((per_worker,), I32), pltpu.VMEM((SC_INDEX_CHUNK,), I32)],
        compiler_params=pltpu.CompilerParams(needs_layout_passes=False),
    )
    def invert(dest_hbm, out_hbm, local_v, chunk_v):
        lo = (lax.axis_index("s") * nc + lax.axis_index("c")) * per_worker
        lane = lax.iota(I32, nl)

        @pl.loop(0, per_worker // nl)
        def _(j):
            local_v[pl.ds(j * nl, nl)] = (lo + j * nl + lane) & fill_mask

        for k in range(top_k):
            @pl.loop(0, t // SC_INDEX_CHUNK)
            def _(c):
                pltpu.sync_copy(dest_hbm.at[k, pl.ds(c * SC_INDEX_CHUNK, SC_INDEX_CHUNK)], chunk_v)

                @pl.loop(0, SC_INDEX_CHUNK // nl)
                def _(j):
                    d = chunk_v[pl.ds(j * nl, nl)] - lo
                    tok = c * SC_INDEX_CHUNK + j * nl + lane
                    plsc.store_scatter(local_v, [d], tok, mask=(d >= 0) & (d < per_worker))

        pltpu.sync_copy(local_v, out_hbm.at[pl.ds(lo, per_worker)])

    return invert(dest)


def _sc_gather(table, idx):
    info = plsc.get_sparse_core_info()
    nc, ns = info.num_cores, info.num_subcores
    n_rows, width = idx.shape[0], table.shape[1]
    per_worker = n_rows // (nc * ns)
    steps = per_worker // SC_GATHER_ROWS
    assert per_worker * nc * ns == n_rows and steps * SC_GATHER_ROWS == per_worker and steps % 2 == 0
    mesh = plsc.VectorSubcoreMesh(core_axis_name="c", subcore_axis_name="s")

    @functools.partial(
        pl.kernel, mesh=mesh,
        out_type=jax.ShapeDtypeStruct((n_rows, width), table.dtype),
        scratch_types=[pltpu.VMEM((steps, SC_GATHER_ROWS), I32),
                       pltpu.VMEM((2, SC_GATHER_ROWS, width), table.dtype),
                       pltpu.SemaphoreType.DMA((2,))],
    )
    def gather(table_hbm, idx_hbm, out_hbm, idx_v, rows_v, sems):
        worker = lax.axis_index("s") * nc + lax.axis_index("c")
        first = worker * steps
        pltpu.sync_copy(idx_hbm.at[worker], idx_v)

        def gather_copy(step, buf):
            return pltpu.make_async_copy(table_hbm.at[idx_v.at[step]], rows_v.at[buf], sems.at[buf])

        gather_copy(0, 0).start()

        @pl.loop(0, steps, step=2)
        def _(i):
            for buf in range(2):
                cur = i + buf

                @pl.when(cur + 1 < steps)
                def _():
                    gather_copy(cur + 1, 1 - buf).start()

                gather_copy(cur, buf).wait()
                row0 = pl.multiple_of((first + cur) * SC_GATHER_ROWS, 8)
                pltpu.sync_copy(rows_v.at[buf], out_hbm.at[pl.ds(row0, SC_GATHER_ROWS)])

    return gather(table, idx.reshape(nc * ns, steps, SC_GATHER_ROWS))


def _expert_kernel(e_ref, act_ref, x_ref, wgu_ref, wd_ref, bg_ref, bl_ref, bd_ref, o_ref,
                   wg_c, wl_c, wd_c):
    mb = pl.program_id(0)
    new_expert = jnp.logical_or(mb == 0, e_ref[mb] != e_ref[jnp.maximum(mb - 1, 0)])

    @pl.when(jnp.logical_and(new_expert, act_ref[mb] == 1))
    def _():
        lane = _iota((D_MODEL, 128), 1)
        half = lane < 64
        idx = jnp.where(half, 2 * lane, 2 * (lane - 64) + 1)
        for m in range(D_MODEL // 128):
            a = jnp.take_along_axis(wgu_ref[0, :, (2 * m) * 128:(2 * m + 1) * 128], idx, axis=1)
            b = jnp.take_along_axis(wgu_ref[0, :, (2 * m + 1) * 128:(2 * m + 2) * 128], idx, axis=1)
            cols = slice(m * 128, (m + 1) * 128)
            wg_c[:, cols] = jnp.where(half, a, pltpu.roll(b, 64, 1)).astype(BF16)
            wl_c[:, cols] = jnp.where(half, pltpu.roll(a, 64, 1), b).astype(BF16)
        wd_c[...] = wd_ref[0].astype(BF16)

    @pl.when(act_ref[mb] == 0)
    def _():
        o_ref[...] = jnp.zeros_like(o_ref)

    @pl.when(act_ref[mb] == 1)
    def _():
        x = _unpack_bf16_pairs(x_ref[...]).astype(BF16)
        glu = jnp.dot(x, wg_c[...], preferred_element_type=F32) + bg_ref[0]
        lin = jnp.dot(x, wl_c[...], preferred_element_type=F32) + bl_ref[0]
        glu = jnp.minimum(glu, SWIGLU_LIMIT)
        lin = jnp.clip(lin, -SWIGLU_LIMIT, SWIGLU_LIMIT)
        act = glu * jax.nn.sigmoid(SWIGLU_ALPHA * glu) * (lin + 1.0)
        o_ref[...] = _pack_bf16_pairs(_mm(act, wd_c[...]) + bd_ref[0])


def _experts(xb, mb_expert, mb_active, w_gu, w_down, bg, bl, bd, n_mb):
    d_ff = w_down.shape[1]
    assert d_ff == D_MODEL and w_gu.shape[1:] == (D_MODEL, 2 * d_ff)
    bspec = pl.BlockSpec((1, 1, D_MODEL), lambda m, e, a: (e[m], 0, 0))
    xspec = pl.BlockSpec((EXPERT_BLOCK, D_MODEL // 2), lambda m, e, a: (m, 0))
    wcache = pltpu.VMEM((D_MODEL, D_MODEL), BF16)
    return pl.pallas_call(
        _expert_kernel,
        grid_spec=pltpu.PrefetchScalarGridSpec(
            num_scalar_prefetch=2,
            grid=(n_mb,),
            in_specs=[xspec,
                      pl.BlockSpec((1, D_MODEL, 2 * d_ff), lambda m, e, a: (e[m], 0, 0)),
                      pl.BlockSpec((1, d_ff, D_MODEL), lambda m, e, a: (e[m], 0, 0)),
                      bspec, bspec, bspec],
            out_specs=xspec,
            scratch_shapes=[wcache, wcache, wcache]),
        out_shape=jax.ShapeDtypeStruct(xb.shape, U32),
        compiler_params=pltpu.CompilerParams(dimension_semantics=("arbitrary",),
                                             vmem_limit_bytes=VMEM_LIMIT),
        name="experts",
    )(mb_expert, mb_active, xb, w_gu, w_down, bg, bl, bd)


def _combine_kernel(y4_ref, gate_ref, x1_ref, nw_ref, o_ref):
    g = gate_ref[...]
    y = x1_ref[...]
    for k in range(TOP_K):
        y = y + g[:, k:k + 1] * _unpack_bf16_pairs(y4_ref[k])
    o_ref[...] = y * lax.rsqrt(jnp.mean(y * y, axis=-1, keepdims=True) + EPS) * nw_ref[...]


def _combine(y4, gate_t, x1, norm_w):
    t = x1.shape[0]
    tm = min(512, t)
    return pl.pallas_call(
        _combine_kernel,
        grid=(t // tm,),
        in_specs=[pl.BlockSpec((TOP_K, tm, D_MODEL // 2), lambda i: (0, i, 0)),
                  pl.BlockSpec((tm, TOP_K), lambda i: (i, 0)),
                  pl.BlockSpec((tm, D_MODEL), lambda i: (i, 0)),
                  pl.BlockSpec((1, D_MODEL), lambda i: (0, 0))],
        out_specs=pl.BlockSpec((tm, D_MODEL), lambda i: (i, 0)),
        out_shape=jax.ShapeDtypeStruct((t, D_MODEL), F32),
        compiler_params=pltpu.CompilerParams(dimension_semantics=("parallel",)),
        name="combine",
    )(y4, gate_t, x1, norm_w.reshape(1, D_MODEL))


def _moe(x1, hn, logits_t, w_gu, b_gu, w_down, b_down, norm_final):
    t = x1.shape[0]
    tt = min(512, t)
    eidx, gate, rank, base, cnt = _route(logits_t, tt)
    cnt = cnt[:, :, 0].astype(I32)
    dest, slot_tok, mb_expert, mb_active, n_mb = _routing_plan(cnt, eidx, rank, tt)
    bg = b_gu[:, None, 0::2]
    bl = b_gu[:, None, 1::2]
    xb = _sc_gather(hn, slot_tok)
    yb = _experts(xb, mb_expert, mb_active, w_gu, w_down, bg, bl, b_down[:, None, :], n_mb)
    y4 = _sc_gather(yb, dest.reshape(-1)).reshape(TOP_K, t, D_MODEL // 2)
    return _combine(y4, gate.T, x1, norm_final)


def kernel(x, norm_mix, w_in, gdn_conv, gdn_A_log, gdn_dt_bias, gdn_norm, rwkv_mu, rwkv_w0, rwkv_w2, rwkv_a0, rwkv_a2, rwkv_g2, rwkv_k_k, rwkv_k_a, rwkv_r_k, rwkv_ln_w, rwkv_ln_b, proj_a, proj_b, w_out, norm_ffn, router_w, router_b, w_gate_up, b_gate_up, w_down, b_down, norm_final):
    bsz, seq, d = x.shape
    depth = w_in.shape[0]
    x2 = x.reshape(bsz * seq, d)
    out = None
    for l in range(depth):
        w = w_in[l]
        w_main = jnp.concatenate([w[:, 0:4096], w[:, 4112:7184], w[:, 7440:9488]], axis=1).astype(BF16)
        w_aux = jnp.concatenate([w[:, 7184:7440], w[:, 4096:4112],
                                 jnp.zeros((d, AUX_COLS - LORA_COLS - 16), w.dtype)], axis=1).astype(BF16)
        p, p_aux = _in_proj(x2, norm_mix[l], w_main, w_aux)
        ya, yb = _mixers(
            p, p_aux, (gdn_conv[l], gdn_A_log[l], gdn_dt_bias[l], gdn_norm[l]),
            (rwkv_mu[l], rwkv_w0[l], rwkv_w2[l], rwkv_a0[l], rwkv_a2[l], rwkv_g2[l], rwkv_k_k[l], rwkv_k_a[l],
             rwkv_r_k[l], rwkv_ln_w[l], rwkv_ln_b[l]), bsz, seq)
        x1, hn, logits_t = _merge(x2, ya, yb, p, proj_a[l], proj_b[l], w_out[l], norm_ffn[l],
                                  router_w[l], router_b[l])
        assert l == depth - 1, "only the final layer's residual is fused with the output norm"
        out = _moe(x1, hn, logits_t, w_gate_up[l], b_gate_up[l], w_down[l], b_down[l], norm_final)
    return out.reshape(bsz, seq, d)
```

```python
import functools

import jax
import jax.numpy as jnp
import numpy as np
from jax import lax
from jax.experimental import pallas as pl
from jax.experimental.pallas import tpu as pltpu
from jax.experimental.pallas import tpu_sc as plsc

F32 = jnp.float32
BF16 = jnp.bfloat16
I32 = jnp.int32
U32 = jnp.uint32

D_MODEL = 1024
EPS = 1e-6
CHUNK = 64
GDN_HEADS = 8
GDN_DK = 128
CONV_W = 4
RWKV_HEADS = 16
RWKV_N = 64
RWKV_GN_EPS = 64e-5
LORA_COLS = 256
N_EXPERTS = 32
TOP_K = 4
SWIGLU_ALPHA = 1.702
SWIGLU_LIMIT = 7.0

MAIN_COLS = 9216
AUX_COLS = 384
HEADS_PER_GROUP = 4
PACK_W = HEADS_PER_GROUP * CHUNK
INV_LEAF = 16
MIXER_TILE = 256
GDN_CHUNKS_PER_ITER = 4
RWKV_CHUNKS_PER_ITER = 4

SC_GATHER_ROWS = 64
SC_INDEX_CHUNK = 2048
EXPERT_BLOCK = 512
VMEM_LIMIT = 48 * 1024 * 1024


def _mm(a, b):
    return jnp.dot(a.astype(BF16), b.astype(BF16), preferred_element_type=F32)


def _mm_nt(a, b):
    return lax.dot_general(a.astype(BF16), b.astype(BF16), (((1,), (1,)), ((), ())),
                           preferred_element_type=F32)


def _mm_tn(a, b):
    return lax.dot_general(a.astype(BF16), b.astype(BF16), (((0,), (0,)), ((), ())),
                           preferred_element_type=F32)


def _split_bf16(x, terms):
    parts = []
    for _ in range(terms - 1):
        hi = x.astype(BF16)
        parts.append(hi)
        x = x - hi.astype(F32)
    parts.append(x.astype(BF16))
    return parts


def _mm_exact_lhs(a_bf16, b):
    out = None
    for part in _split_bf16(b, 3):
        d = jnp.dot(a_bf16, part, preferred_element_type=F32)
        out = d if out is None else out + d
    return out


def _pack_bf16_pairs(x):
    half = x.shape[1] // 2
    lo = lax.bitcast_convert_type(x[:, :half].astype(BF16).astype(F32), U32) >> 16
    hi = lax.bitcast_convert_type(x[:, half:].astype(BF16).astype(F32), U32) & jnp.uint32(0xFFFF0000)
    return lo | hi


def _unpack_bf16_pairs(w):
    lo = lax.bitcast_convert_type(w << 16, F32)
    hi = lax.bitcast_convert_type(w & jnp.uint32(0xFFFF0000), F32)
    return jnp.concatenate([lo, hi], axis=1)


def _iota(shape, dim):
    return lax.broadcasted_iota(I32, shape, dim)


def _softplus(x):
    return jnp.maximum(x, 0.0) + jnp.log(1.0 + jnp.exp(-jnp.abs(x)))


def _stack_rows(x, n):
    return jnp.concatenate([x] * n, axis=0)


def _tri_inv_packed(ms, eye, bdmask):
    def bd(x):
        return jnp.where(bdmask, _stack_rows(x, HEADS_PER_GROUP), jnp.zeros((), BF16))

    def mul(a, b):
        ah, al = _split_bf16(a, 2)
        bh, bl = _split_bf16(b, 2)
        bdh = bd(bh)
        return (jnp.dot(ah, bdh, preferred_element_type=F32) + jnp.dot(al, bdh, preferred_element_type=F32)
                + jnp.dot(ah, bd(bl), preferred_element_type=F32))

    def mul_bf16(a, b):
        return jnp.dot(a.astype(BF16), bd(b.astype(BF16)), preferred_element_type=F32)

    lane = _iota((CHUNK, PACK_W), 1) % CHUNK
    leaf = (lane // INV_LEAF) == (_iota((CHUNK, PACK_W), 0) // INV_LEAF)
    ds = [jnp.where(leaf, m, 0.0) for m in ms]
    ls = [m - d for m, d in zip(ms, ds)]
    ts = [eye - d for d in ds]
    xs = [mul(d, d) for d in ds]
    for _ in range(2):
        rs = [mul(jnp.concatenate([t, x], axis=0), x) for t, x in zip(ts, xs)]
        ts = [t + r[:CHUNK] for t, r in zip(ts, rs)]
        xs = [r[CHUNK:] for r in rs]
    bs = [t + mul(t, x) for t, x in zip(ts, xs)]
    ns = [mul_bf16(b, l) for b, l in zip(bs, ls)]
    n2 = [mul_bf16(n, n) for n in ns]
    ps = [(eye - n) + mul_bf16(eye - n, q) for n, q in zip(ns, n2)]
    return [mul_bf16(p, b) for p, b in zip(ps, bs)]


def _shift_rows(x, prev8, k, row8):
    r = pltpu.roll(x, k, 0)
    pr = pltpu.roll(prev8, k, 0)
    head = jnp.where(row8 < k, pr, r[:8])
    return jnp.concatenate([head, r[8:]], axis=0)


def _in_proj_kernel(x_ref, nw_ref, w_ref, wa_ref, o_ref, oa_ref, h_ref):
    @pl.when(pl.program_id(1) == 0)
    def _():
        x = x_ref[...]
        y = x * lax.rsqrt(jnp.mean(x * x, axis=-1, keepdims=True) + EPS)
        h_ref[...] = (y * nw_ref[...]).astype(BF16)
        oa_ref[...] = jnp.dot(h_ref[...], wa_ref[...], preferred_element_type=F32)

    o_ref[...] = jnp.dot(h_ref[...], w_ref[...], preferred_element_type=F32).astype(BF16)


def _in_proj(x2, norm_w, w_main, w_aux):
    t = x2.shape[0]
    tm = min(1024, t)
    tn = 2304
    return pl.pallas_call(
        _in_proj_kernel,
        grid=(t // tm, MAIN_COLS // tn),
        in_specs=[pl.BlockSpec((tm, D_MODEL), lambda i, j: (i, 0)),
                  pl.BlockSpec((1, D_MODEL), lambda i, j: (0, 0)),
                  pl.BlockSpec((D_MODEL, tn), lambda i, j: (0, j)),
                  pl.BlockSpec((D_MODEL, AUX_COLS), lambda i, j: (0, 0))],
        out_specs=[pl.BlockSpec((tm, tn), lambda i, j: (i, j)),
                   pl.BlockSpec((tm, AUX_COLS), lambda i, j: (i, 0))],
        out_shape=[jax.ShapeDtypeStruct((t, MAIN_COLS), BF16), jax.ShapeDtypeStruct((t, AUX_COLS), F32)],
        scratch_shapes=[pltpu.VMEM((tm, D_MODEL), BF16)],
        compiler_params=pltpu.CompilerParams(dimension_semantics=("parallel", "arbitrary"),
                                             vmem_limit_bytes=VMEM_LIMIT),
        name="in_proj",
    )(x2, norm_w.reshape(1, D_MODEL), w_main, w_aux)


def _gdn_kernel(q_ref, k_ref, v_ref, z_ref, ba_ref, conv_ref, alog_ref, dtb_ref, nw_ref, o_ref,
                carry_ref, qn_ref, kn_ref, vv_ref, beta_ref, g_ref, s_ref, *, ts):
    @pl.when(pl.program_id(1) == 0)
    def _():
        carry_ref[...] = jnp.zeros_like(carry_ref)
        s_ref[...] = jnp.zeros_like(s_ref)

    row8 = _iota((8, D_MODEL), 0)

    def conv_silu(idx, x_ref):
        x = x_ref[...].astype(F32)
        prev8 = carry_ref[idx]
        w4 = conv_ref[:, idx * D_MODEL:(idx + 1) * D_MODEL]
        y = x * w4[CONV_W - 1:CONV_W]
        for k in range(1, CONV_W):
            y = y + _shift_rows(x, prev8, k, row8) * w4[CONV_W - 1 - k:CONV_W - k]
        carry_ref[idx] = x[ts - 8:ts]
        return y * jax.nn.sigmoid(y)

    def l2norm_heads(x, scale):
        parts = []
        for h in range(GDN_HEADS):
            xh = x[:, h * GDN_DK:(h + 1) * GDN_DK]
            parts.append(xh * (lax.rsqrt(jnp.sum(xh * xh, axis=-1, keepdims=True) + EPS) * scale))
        return jnp.concatenate(parts, axis=1)

    qn_ref[...] = l2norm_heads(conv_silu(0, q_ref), GDN_DK ** -0.5)
    kn_ref[...] = l2norm_heads(conv_silu(1, k_ref), 1.0)
    vv_ref[...] = conv_silu(2, v_ref)
    ba = ba_ref[...]
    beta_ref[...] = jax.nn.sigmoid(ba)
    g_ref[...] = -jnp.exp(alog_ref[...]) * _softplus(ba + dtb_ref[...])

    gw = HEADS_PER_GROUP * GDN_DK
    lane_p = _iota((CHUNK, PACK_W), 1)
    row_p = _iota((CHUNK, PACK_W), 0)
    eye = jnp.where((lane_p % CHUNK) == row_p, 1.0, 0.0).astype(F32)
    strict = (lane_p % CHUNK) < row_p
    incl = (lane_p % CHUNK) <= row_p
    bd_pp = (_iota((PACK_W, PACK_W), 0) // CHUNK) == (_iota((PACK_W, PACK_W), 1) // CHUNK)
    bd_pn = (_iota((PACK_W, gw), 0) // CHUNK) == (_iota((PACK_W, gw), 1) // GDN_DK)
    bd_pn2 = (_iota((PACK_W, 2 * gw), 0) // CHUNK) == ((_iota((PACK_W, 2 * gw), 1) % gw) // GDN_DK)
    ltri = jnp.where(_iota((CHUNK, CHUNK), 1) <= _iota((CHUNK, CHUNK), 0), 1.0, 0.0).astype(BF16)
    ones_cc = jnp.ones((CHUNK, CHUNK), BF16)
    lane128 = _iota((CHUNK, 128), 1)
    nw = nw_ref[...]

    n_groups = GDN_HEADS // HEADS_PER_GROUP
    hcols = [slice(j * GDN_DK, (j + 1) * GDN_DK) for j in range(HEADS_PER_GROUP)]

    def intra_chunk(rows_list):
        units = [(i, gi) for i in range(len(rows_list)) for gi in range(n_groups)]
        gcs = [_mm_exact_lhs(ltri, g_ref[rows, :]) for rows in rows_list]
        beta = [beta_ref[rows, :] for rows in rows_list]
        pre = []
        for i, gi in units:
            rows = rows_list[i]
            heads = [gi * HEADS_PER_GROUP + j for j in range(HEADS_PER_GROUP)]
            cols = slice(gi * gw, (gi + 1) * gw)
            beta_nat = jnp.concatenate(
                [jnp.broadcast_to(beta[i][:, h:h + 1], (CHUNK, GDN_DK)) for h in heads], axis=1)
            gcol = [jnp.broadcast_to(gcs[i][:, 8 + h:9 + h], (CHUNK, GDN_DK)) for h in heads]
            gcol_p = jnp.concatenate([jnp.where(lane128 < CHUNK, gcol[0], gcol[1]),
                                      jnp.where(lane128 < CHUNK, gcol[2], gcol[3])], axis=1)
            pre.append(dict(qn=qn_ref[rows, cols], kn=kn_ref[rows, cols], vv=vv_ref[rows, cols],
                            beta_nat=beta_nat, gc_nat=jnp.concatenate(gcol, axis=1), gcol_p=gcol_p))
        grow = [_mm_exact_lhs(ones_cc, u["gcol_p"] * eye) for u in pre]
        aas = [_mm_nt(jnp.concatenate([u["kn"] * u["beta_nat"], u["qn"]], axis=0),
                      jnp.where(bd_pn, _stack_rows(u["kn"], HEADS_PER_GROUP), 0.0)) for u in pre]
        decay = [jnp.exp(jnp.minimum(u["gcol_p"] - gr, 0.0)) for u, gr in zip(pre, grow)]
        t_inv = _tri_inv_packed([jnp.where(strict, aa[:CHUNK] * d, 0.0) for aa, d in zip(aas, decay)],
                                eye, bd_pp)
        out = []
        for u, aa, d, ti in zip(pre, aas, decay, t_inv):
            egc = jnp.exp(u["gc_nat"])
            data = jnp.concatenate([u["vv"] * u["beta_nat"], u["kn"] * u["beta_nat"] * egc], axis=1)
            uw = _mm(ti, jnp.where(bd_pn2, _stack_rows(data, HEADS_PER_GROUP), 0.0))
            gl = u["gc_nat"][CHUNK - 1:CHUNK, :]
            out.append(dict(uw=uw, attn=jnp.where(incl, aa[CHUNK:] * d, 0.0), qd=u["qn"] * egc,
                            kd=u["kn"] * jnp.exp(gl - u["gc_nat"]), egl=jnp.exp(gl)))
        return out

    def recurrence(rows, group_units):
        wq = {}
        for gi, u in enumerate(group_units):
            for j, hc in enumerate(hcols):
                h = gi * HEADS_PER_GROUP + j
                wq[h] = _mm(jnp.concatenate([u["uw"][:, gw + j * GDN_DK:gw + (j + 1) * GDN_DK],
                                             u["qd"][:, hc]], axis=0), s_ref[h])
        vnew = {}
        for gi, u in enumerate(group_units):
            for j, hc in enumerate(hcols):
                h = gi * HEADS_PER_GROUP + j
                vnew[h] = u["uw"][:, hc] - wq[h][:CHUNK]
                s_ref[h] = s_ref[h] * u["egl"][:, hc] + _mm_tn(u["kd"][:, hc], vnew[h])
        for gi, u in enumerate(group_units):
            heads = [gi * HEADS_PER_GROUP + j for j in range(HEADS_PER_GROUP)]
            vn = jnp.concatenate([vnew[h] for h in heads], axis=1)
            o = jnp.concatenate([wq[h][CHUNK:] for h in heads], axis=1) + _mm(
                u["attn"], jnp.where(bd_pn, _stack_rows(vn, HEADS_PER_GROUP), 0.0))
            for j, h in enumerate(heads):
                oh = o[:, hcols[j]]
                oh = oh * lax.rsqrt(jnp.mean(oh * oh, axis=-1, keepdims=True) + EPS) * nw
                zh = z_ref[rows, h * GDN_DK:(h + 1) * GDN_DK].astype(F32)
                o_ref[rows, h * GDN_DK:(h + 1) * GDN_DK] = (oh * (zh * jax.nn.sigmoid(zh))).astype(BF16)

    def chunk_pair_body(c, carry):
        rows_list = [pl.ds(pl.multiple_of((c * GDN_CHUNKS_PER_ITER + i) * CHUNK, CHUNK), CHUNK)
                     for i in range(GDN_CHUNKS_PER_ITER)]
        units = intra_chunk(rows_list)
        for i, rows in enumerate(rows_list):
            recurrence(rows, units[i * n_groups:(i + 1) * n_groups])
        return carry

    lax.fori_loop(0, ts // (CHUNK * GDN_CHUNKS_PER_ITER), chunk_pair_body, 0)


def _gdn_parts(p, p_aux, conv_w, a_log, dt_bias, norm_w, ts, ns):
    pad8 = lambda v: jnp.zeros((1, 128), F32).at[0, 8:16].set(v)
    tok = lambda col: pl.BlockSpec((ts, D_MODEL), lambda b, s, col=col: (b * ns + s, col))
    const = lambda shape: pl.BlockSpec(shape, lambda b, s: (0,) * len(shape))
    in_specs = [tok(0), tok(1), tok(2), tok(3),
                pl.BlockSpec((ts, 128), lambda b, s: (b * ns + s, LORA_COLS // 128)),
                const((CONV_W, 3 * D_MODEL)), const((1, 128)), const((1, 128)), const((1, GDN_DK))]
    operands = [p, p, p, p, p_aux, conv_w, pad8(a_log), pad8(dt_bias), norm_w.reshape(1, GDN_DK)]
    scratch = [pltpu.VMEM((3, 8, D_MODEL), F32),
               pltpu.VMEM((ts, D_MODEL), F32), pltpu.VMEM((ts, D_MODEL), F32), pltpu.VMEM((ts, D_MODEL), F32),
               pltpu.VMEM((ts, 128), F32), pltpu.VMEM((ts, 128), F32),
               pltpu.VMEM((GDN_HEADS, GDN_DK, GDN_DK), F32)]
    return in_specs, operands, scratch


def _rwkv_kernel(r_ref, k_ref, v_ref, lo_ref, mu_ref, w0_ref, w2_ref, a0_ref, a2_ref, g2_ref,
                 kk_ref, ka_ref, rk_ref, lnw_ref, lnb_ref, seg_ref, o_ref,
                 carry_ref, rs_ref, ks_ref, vs_ref, kks_ref, bs_ref, ld_ref, gg_ref, os_ref, st_ref,
                 *, ts):
    @pl.when(pl.program_id(1) == 0)
    def _():
        carry_ref[...] = jnp.zeros_like(carry_ref)
        st_ref[...] = jnp.zeros_like(st_ref)

    width = D_MODEL
    row8 = _iota((8, width), 0)
    row8l = _iota((8, LORA_COLS), 0)

    def seg_sum(x):
        xb = x.astype(BF16)
        return jnp.concatenate(
            [jnp.dot(xb[:, g * PACK_W:(g + 1) * PACK_W], seg_ref[...], preferred_element_type=F32)
             for g in range(width // PACK_W)], axis=1)

    def lerp(idx, x_ref, mu, r8):
        x = x_ref[...].astype(F32)
        xs = x + (_shift_rows(x, carry_ref[idx, :, :x.shape[1]], 1, r8) - x) * mu
        carry_ref[idx, :, :x.shape[1]] = x[ts - 8:ts]
        return xs

    r = lerp(0, r_ref, mu_ref[:, 0:width], row8)
    k = lerp(1, k_ref, mu_ref[:, width:2 * width], row8)
    v = lerp(2, v_ref, mu_ref[:, 2 * width:3 * width], row8)
    lo = lerp(3, lo_ref, mu_ref[:, 3 * width:3 * width + LORA_COLS], row8l)
    lo_a = lo[:, :128]
    w_log = -_softplus(-(w0_ref[...] + _mm(jnp.tanh(lo_a), w2_ref[...]))) - 0.5
    ld_ref[...] = -jnp.exp(w_log)
    aa = jax.nn.sigmoid(a0_ref[...] + _mm(lo_a, a2_ref[...]))
    gg_ref[...] = _mm(jax.nn.sigmoid(lo[:, 128:]), g2_ref[...])
    kx = k * kk_ref[...]
    kk = kx * lax.rsqrt(seg_sum(kx * kx) + EPS)
    k = k * (1.0 + (aa - 1.0) * ka_ref[...])
    rs_ref[...] = r
    ks_ref[...] = k
    vs_ref[...] = v
    kks_ref[...] = kk
    bs_ref[...] = kk * aa

    lane_p = _iota((CHUNK, PACK_W), 1)
    row_p = _iota((CHUNK, PACK_W), 0)
    eye = jnp.where((lane_p % CHUNK) == row_p, 1.0, 0.0).astype(F32)
    strict = (lane_p % CHUNK) < row_p
    incl = (lane_p % CHUNK) <= row_p
    bd = (_iota((PACK_W, PACK_W), 0) // CHUNK) == (_iota((PACK_W, PACK_W), 1) // CHUNK)
    bd2 = jnp.concatenate([bd, bd], axis=1)
    ltri = jnp.where(_iota((CHUNK, CHUNK), 1) <= _iota((CHUNK, CHUNK), 0), 1.0, 0.0).astype(BF16)

    def bdiag(x):
        return jnp.where(bd, _stack_rows(x, HEADS_PER_GROUP), 0.0)

    n_groups = width // PACK_W
    gcols = [slice(g * PACK_W, (g + 1) * PACK_W) for g in range(n_groups)]

    def intra_chunk(rows_list):
        units = [(rows, cols) for rows in rows_list for cols in gcols]
        ld = [ld_ref[rows, cols] for rows, cols in units]
        cs = [_mm_exact_lhs(ltri, x) for x in ld]
        pre = []
        for g, (rows, cols) in enumerate(units):
            cl = cs[g][CHUNK - 1:CHUNK, :]
            e_neg = jnp.exp(-cs[g])
            e_dec = jnp.exp(cl - cs[g])
            kc, bc = ks_ref[rows, cols], bs_ref[rows, cols]
            rt = rs_ref[rows, cols] * jnp.exp(cs[g])
            kt = kks_ref[rows, cols] * jnp.exp(cs[g] - ld[g])
            pre.append(dict(cl=cl, rt=rt, kt=kt, vc=vs_ref[rows, cols], kh=kc * e_neg, bh=bc * e_neg,
                            kdec=kc * e_dec, bdec=bc * e_dec))
        akb = [_mm_nt(jnp.concatenate([u["kt"], u["rt"]], axis=0),
                      jnp.concatenate([bdiag(u["kh"]), bdiag(u["bh"])], axis=0)) for u in pre]
        t_inv = _tri_inv_packed([jnp.where(strict, a[:CHUNK, PACK_W:], 0.0) for a in akb], eye, bd)
        xo = [_mm(jnp.concatenate([jnp.where(strict, a[:CHUNK, :PACK_W], 0.0),
                                   jnp.where(incl, a[CHUNK:, :PACK_W], 0.0)], axis=0), bdiag(u["vc"]))
              for a, u in zip(akb, pre)]
        tt = [_mm(ti, jnp.where(bd2, _stack_rows(jnp.concatenate([x[:CHUNK], u["kt"]], axis=1),
                                                 HEADS_PER_GROUP), 0.0))
              for ti, x, u in zip(t_inv, xo, pre)]
        rr = [_mm(jnp.where(incl, a[CHUNK:, PACK_W:], 0.0),
                  jnp.where(bd2, _stack_rows(jnp.concatenate([t[:, PACK_W:], t[:, :PACK_W]], axis=1),
                                             HEADS_PER_GROUP), 0.0))
              for a, t in zip(akb, tt)]
        return [dict(u, tk=t[:, PACK_W:], u0=t[:, :PACK_W], rq=u["rt"] - r[:, :PACK_W],
                     o1=x[CHUNK:] - r[:, PACK_W:]) for u, x, t, r in zip(pre, xo, tt, rr)]

    def recurrence(rows, group_units):
        uo = [_mm_nt(jnp.concatenate([u["tk"], u["rq"]], axis=0), st_ref[g])
              for g, u in enumerate(group_units)]
        upd = []
        for g, (cols, u) in enumerate(zip(gcols, group_units)):
            os_ref[rows, cols] = u["o1"] + uo[g][CHUNK:]
            upd.append(_mm_tn(jnp.concatenate([u["vc"], u["u0"] + uo[g][:CHUNK]], axis=0),
                              jnp.concatenate([u["kdec"], -u["bdec"]], axis=0)))
        for g, u in enumerate(group_units):
            st_ref[g] = st_ref[g] * jnp.exp(u["cl"]) + jnp.where(bd, upd[g], 0.0)

    def chunks_body(c, carry):
        rows_list = [pl.ds(pl.multiple_of((c * RWKV_CHUNKS_PER_ITER + i) * CHUNK, CHUNK), CHUNK)
                     for i in range(RWKV_CHUNKS_PER_ITER)]
        units = intra_chunk(rows_list)
        for i, rows in enumerate(rows_list):
            recurrence(rows, units[i * n_groups:(i + 1) * n_groups])
        return carry

    lax.fori_loop(0, ts // (CHUNK * RWKV_CHUNKS_PER_ITER), chunks_body, 0)

    o = os_ref[...]
    inv_n = 1.0 / RWKV_N
    mean = seg_sum(o) * inv_n
    cen = o - mean
    var = seg_sum(cen * cen) * inv_n
    o = cen * lax.rsqrt(var + RWKV_GN_EPS) * lnw_ref[...] + lnb_ref[...]
    bonus = seg_sum(rs_ref[...] * ks_ref[...] * rk_ref[...]) * vs_ref[...]
    o_ref[...] = ((o + bonus) * gg_ref[...]).astype(BF16)


def _rwkv_parts(p, p_aux, mu, w0, w2, a0, a2, g2, k_k, k_a, r_k, ln_w, ln_b, ts, ns):
    width = D_MODEL
    row = lambda v: v.reshape(1, -1)
    w2p = jnp.concatenate([w2, jnp.zeros_like(w2)], axis=0).astype(BF16)
    a2p = jnp.concatenate([jnp.zeros_like(a2), a2], axis=0).astype(BF16)
    seg = (np.arange(PACK_W)[:, None] // RWKV_N == np.arange(PACK_W)[None, :] // RWKV_N)
    seg = jnp.asarray(seg, BF16)
    tok = lambda col: pl.BlockSpec((ts, width), lambda b, s, col=col: (b * ns + s, col))
    const = lambda shape: pl.BlockSpec(shape, lambda b, s: (0,) * len(shape))
    fbuf = lambda: pltpu.VMEM((ts, width), F32)
    in_specs = [tok(4), tok(5), tok(6),
                pl.BlockSpec((ts, LORA_COLS), lambda b, s: (b * ns + s, 0)),
                const((1, 3 * width + LORA_COLS)), const((1, width)), const((128, width)),
                const((1, width)), const((128, width)), const((128, width)),
                const((1, width)), const((1, width)), const((1, width)), const((1, width)),
                const((1, width)), const((PACK_W, PACK_W))]
    operands = [p, p, p, p_aux, row(mu), row(w0), w2p, row(a0), a2p, g2.astype(BF16), row(k_k), row(k_a),
                row(r_k), row(ln_w), row(ln_b), seg]
    scratch = [pltpu.VMEM((4, 8, width), F32),
               fbuf(), fbuf(), fbuf(), fbuf(), fbuf(), fbuf(), fbuf(), fbuf(),
               pltpu.VMEM((width // PACK_W, PACK_W, PACK_W), F32)]
    return in_specs, operands, scratch


def _mixer_call(body, name, parts, bsz, seq, ts):
    in_specs, operands, scratch = parts
    ns = seq // ts
    return pl.pallas_call(
        functools.partial(body, ts=ts),
        grid=(bsz, ns),
        in_specs=in_specs,
        out_specs=pl.BlockSpec((ts, D_MODEL), lambda b, s: (b * ns + s, 0)),
        out_shape=jax.ShapeDtypeStruct((bsz * seq, D_MODEL), BF16),
        scratch_shapes=scratch,
        compiler_params=pltpu.CompilerParams(dimension_semantics=("parallel", "arbitrary"),
                                             vmem_limit_bytes=VMEM_LIMIT),
        name=name,
    )(*operands)


def _mixers(p, p_aux, gdn_params, rwkv_params, bsz, seq):
    ts = min(MIXER_TILE, seq)
    ns = seq // ts
    ya = _mixer_call(_gdn_kernel, "gdn", _gdn_parts(p, p_aux, *gdn_params, ts, ns), bsz, seq, ts)
    yb = _mixer_call(_rwkv_kernel, "rwkv", _rwkv_parts(p, p_aux, *rwkv_params, ts, ns), bsz, seq, ts)
    return ya, yb


def _merge_kernel(x_ref, ya_ref, yb_ref, ga_ref, gb_ref, pa_ref, pb_ref, wo_ref, nw_ref, rw_ref, rb_ref,
                  x1_ref, hn_ref, lg_ref):
    merged = (jax.nn.sigmoid(ga_ref[...].astype(F32)) * jnp.dot(ya_ref[...], pa_ref[...], preferred_element_type=F32)
              + jax.nn.sigmoid(gb_ref[...].astype(F32)) * jnp.dot(yb_ref[...], pb_ref[...], preferred_element_type=F32))
    x1 = x_ref[...] + _mm(merged, wo_ref[...])
    x1_ref[...] = x1
    hn = x1 * lax.rsqrt(jnp.mean(x1 * x1, axis=-1, keepdims=True) + EPS) * nw_ref[...]
    hn_ref[...] = _pack_bf16_pairs(hn)
    hn_hi, hn_lo = _split_bf16(hn, 2)
    rw_hi, rw_lo = _split_bf16(rw_ref[...], 2)
    lg_ref[...] = _mm_nt(rw_hi, hn_hi) + _mm_nt(rw_lo, hn_hi) + _mm_nt(rw_hi, hn_lo) + rb_ref[...]


def _merge(x2, ya, yb, p, proj_a, proj_b, w_out, norm_w, router_w, router_b):
    t = x2.shape[0]
    tm = min(512, t)
    tok = lambda col: pl.BlockSpec((tm, D_MODEL), lambda i, col=col: (i, col))
    const = lambda shape: pl.BlockSpec(shape, lambda i: (0,) * len(shape))
    return pl.pallas_call(
        _merge_kernel,
        grid=(t // tm,),
        in_specs=[tok(0), tok(0), tok(0), tok(7), tok(8),
                  const((D_MODEL, D_MODEL)), const((D_MODEL, D_MODEL)), const((D_MODEL, D_MODEL)),
                  const((1, D_MODEL)), const((N_EXPERTS, D_MODEL)), const((N_EXPERTS, 1))],
        out_specs=[tok(0), pl.BlockSpec((tm, D_MODEL // 2), lambda i: (i, 0)),
                   pl.BlockSpec((N_EXPERTS, tm), lambda i: (0, i))],
        out_shape=[jax.ShapeDtypeStruct((t, D_MODEL), F32), jax.ShapeDtypeStruct((t, D_MODEL // 2), U32),
                   jax.ShapeDtypeStruct((N_EXPERTS, t), F32)],
        compiler_params=pltpu.CompilerParams(dimension_semantics=("parallel",),
                                             vmem_limit_bytes=VMEM_LIMIT),
        name="merge",
    )(x2, ya, yb, p, p, proj_a.astype(BF16), proj_b.astype(BF16), w_out.astype(BF16),
      norm_w.reshape(1, D_MODEL), router_w.T, router_b.reshape(N_EXPERTS, 1))


def _route_kernel(lg_ref, eidx_ref, gate_ref, rank_ref, base_ref, cnt_ref, carry_ref, *, tt):
    @pl.when(pl.program_id(0) == 0)
    def _():
        carry_ref[...] = jnp.zeros_like(carry_ref)

    l = lg_ref[...]
    ie = _iota((N_EXPERTS, tt), 0)
    vals, hots, idxs = [], [], []
    for _ in range(TOP_K):
        m = jnp.max(l, axis=0, keepdims=True)
        idx = jnp.min(jnp.where(l == m, ie, N_EXPERTS), axis=0, keepdims=True)
        hot = ie == idx
        vals.append(m)
        hots.append(hot)
        idxs.append(idx)
        l = jnp.where(hot, -jnp.inf, l)
    exps = [jnp.exp(v - vals[0]) for v in vals]
    den = exps[0] + exps[1] + exps[2] + exps[3]
    gate_ref[...] = jnp.concatenate([e / den for e in exps], axis=0)
    eidx_ref[...] = jnp.concatenate(idxs, axis=0)

    sel = jnp.zeros((N_EXPERTS, tt), F32)
    for hot in hots:
        sel = sel + jnp.where(hot, 1.0, 0.0)
    before = jnp.where(_iota((tt, tt), 0) < _iota((tt, tt), 1), 1.0, 0.0).astype(BF16)
    carry = carry_ref[...]
    prefix = jnp.dot(sel.astype(BF16), before, preferred_element_type=F32) + carry[:, 0:1]
    rank_ref[...] = jnp.concatenate(
        [jnp.sum(jnp.where(hot, prefix, 0.0), axis=0, keepdims=True) for hot in hots], axis=0).astype(I32)
    cnt = jnp.broadcast_to(jnp.sum(sel, axis=1, keepdims=True), (N_EXPERTS, 128))
    base_ref[0] = carry
    cnt_ref[0] = cnt
    carry_ref[...] = carry + cnt


def _route(logits_t, tt):
    t = logits_t.shape[1]
    nt = t // tt
    row4 = pl.BlockSpec((TOP_K, tt), lambda i: (0, i))
    per_tile = pl.BlockSpec((1, N_EXPERTS, 128), lambda i: (i, 0, 0))
    return pl.pallas_call(
        functools.partial(_route_kernel, tt=tt),
        grid=(nt,),
        in_specs=[pl.BlockSpec((N_EXPERTS, tt), lambda i: (0, i))],
        out_specs=[row4, row4, row4, per_tile, per_tile],
        out_shape=[jax.ShapeDtypeStruct((TOP_K, t), I32), jax.ShapeDtypeStruct((TOP_K, t), F32),
                   jax.ShapeDtypeStruct((TOP_K, t), I32),
                   jax.ShapeDtypeStruct((nt, N_EXPERTS, 128), F32),
                   jax.ShapeDtypeStruct((nt, N_EXPERTS, 128), F32)],
        scratch_shapes=[pltpu.VMEM((N_EXPERTS, 128), F32)],
        compiler_params=pltpu.CompilerParams(dimension_semantics=("arbitrary",)),
        name="route",
    )(logits_t)


def _count_le(sorted_vals, queries):
    return jnp.sum((sorted_vals[None, :] <= queries[:, None]).astype(I32), axis=1)


def _routing_plan(cnt, eidx, rank, tt):
    t = cnt.shape[0] * tt
    n_mb = (t * TOP_K) // EXPERT_BLOCK + N_EXPERTS
    counts = jnp.sum(cnt, axis=0)
    padded = ((counts + EXPERT_BLOCK - 1) // EXPERT_BLOCK) * EXPERT_BLOCK
    end_pad = jnp.cumsum(padded)
    start_pad = end_pad - padded
    hot = eidx[:, :, None] == jnp.arange(N_EXPERTS, dtype=I32)[None, None, :]
    dest = jnp.sum(jnp.where(hot, start_pad[None, None, :], 0), axis=-1) + rank

    mb_start = jnp.arange(n_mb, dtype=I32) * EXPERT_BLOCK
    mb_expert = jnp.minimum(_count_le(end_pad, mb_start), N_EXPERTS - 1).astype(I32)
    mb_active = (mb_start < end_pad[-1]).astype(I32)

    dest = dest.astype(I32)
    return dest, _sc_slot_tokens(dest, n_mb * EXPERT_BLOCK), mb_expert, mb_active, n_mb


def _sc_slot_tokens(dest, n_slots):
    info = plsc.get_sparse_core_info()
    nc, ns, nl = info.num_cores, info.num_subcores, info.num_lanes
    top_k, t = dest.shape
    per_worker = n_slots // (nc * ns)
    assert per_worker * nc * ns == n_slots and per_worker % nl == 0 and t % SC_INDEX_CHUNK == 0
    fill_mask = (1 << (t.bit_length() - 1)) - 1
    mesh = plsc.VectorSubcoreMesh(core_axis_name="c", subcore_axis_name="s")

    @functools.partial(
        pl.kernel, mesh=mesh,
        out_type=jax.ShapeDtypeStruct((n_slots,), I32),
        scratch_types=[pltpu.VMEM((per_worker,), I32), pltpu.VMEM((SC_INDEX_CHUNK,), I32)],
        compiler_params=pltpu.CompilerParams(needs_layout_passes=False),
    )
    def invert(dest_hbm, out_hbm, local_v, chunk_v):
        lo = (lax.axis_index("s") * nc + lax.axis_index("c")) * per_worker
        lane = lax.iota(I32, nl)

        @pl.loop(0, per_worker // nl)
        def _(j):
            local_v[pl.ds(j * nl, nl)] = (lo + j * nl + lane) & fill_mask

        for k in range(top_k):
            @pl.loop(0, t // SC_INDEX_CHUNK)
            def _(c):
                pltpu.sync_copy(dest_hbm.at[k, pl.ds(c * SC_INDEX_CHUNK, SC_INDEX_CHUNK)], chunk_v)

                @pl.loop(0, SC_INDEX_CHUNK // nl)
                def _(j):
                    d = chunk_v[pl.ds(j * nl, nl)] - lo
                    tok = c * SC_INDEX_CHUNK + j * nl + lane
                    plsc.store_scatter(local_v, [d], tok, mask=(d >= 0) & (d < per_worker))

        pltpu.sync_copy(local_v, out_hbm.at[pl.ds(lo, per_worker)])

    return invert(dest)


def _sc_gather(table, idx):
    info = plsc.get_sparse_core_info()
    nc, ns = info.num_cores, info.num_subcores
    n_rows, width = idx.shape[0], table.shape[1]
    per_worker = n_rows // (nc * ns)
    steps = per_worker // SC_GATHER_ROWS
    assert per_worker * nc * ns == n_rows and steps * SC_GATHER_ROWS == per_worker and steps % 2 == 0
    mesh = plsc.VectorSubcoreMesh(core_axis_name="c", subcore_axis_name="s")

    @functools.partial(
        pl.kernel, mesh=mesh,
        out_type=jax.ShapeDtypeStruct((n_rows, width), table.dtype),
        scratch_types=[pltpu.VMEM((steps, SC_GATHER_ROWS), I32),
                       pltpu.VMEM((2, SC_GATHER_ROWS, width), table.dtype),
                       pltpu.SemaphoreType.DMA((2,))],
    )
    def gather(table_hbm, idx_hbm, out_hbm, idx_v, rows_v, sems):
        worker = lax.axis_index("s") * nc + lax.axis_index("c")
        first = worker * steps
        pltpu.sync_copy(idx_hbm.at[worker], idx_v)

        def gather_copy(step, buf):
            return pltpu.make_async_copy(table_hbm.at[idx_v.at[step]], rows_v.at[buf], sems.at[buf])

        gather_copy(0, 0).start()

        @pl.loop(0, steps, step=2)
        def _(i):
            for buf in range(2):
                cur = i + buf

                @pl.when(cur + 1 < steps)
                def _():
                    gather_copy(cur + 1, 1 - buf).start()

                gather_copy(cur, buf).wait()
                row0 = pl.multiple_of((first + cur) * SC_GATHER_ROWS, 8)
                pltpu.sync_copy(rows_v.at[buf], out_hbm.at[pl.ds(row0, SC_GATHER_ROWS)])

    return gather(table, idx.reshape(nc * ns, steps, SC_GATHER_ROWS))


def _expert_kernel(e_ref, act_ref, x_ref, wgu_ref, wd_ref, bg_ref, bl_ref, bd_ref, o_ref,
                   wg_c, wl_c, wd_c):
    mb = pl.program_id(0)
    new_expert = jnp.logical_or(mb == 0, e_ref[mb] != e_ref[jnp.maximum(mb - 1, 0)])

    @pl.when(jnp.logical_and(new_expert, act_ref[mb] == 1))
    def _():
        src = _iota((256, 256), 0)
        dst = _iota((256, 256), 1)
        sel = jnp.where(src == jnp.where(dst < 128, 2 * dst, 2 * (dst - 128) + 1), 1.0, 0.0).astype(BF16)
        for m in range(D_MODEL // 128):
            pair = jnp.dot(wgu_ref[0, :, m * 256:(m + 1) * 256].astype(BF16), sel, preferred_element_type=F32)
            cols = slice(m * 128, (m + 1) * 128)
            wg_c[:, cols] = pair[:, :128].astype(BF16)
            wl_c[:, cols] = pair[:, 128:].astype(BF16)
        wd_c[...] = wd_ref[0].astype(BF16)

    @pl.when(act_ref[mb] == 0)
    def _():
        o_ref[...] = jnp.zeros_like(o_ref)

    @pl.when(act_ref[mb] == 1)
    def _():
        x = _unpack_bf16_pairs(x_ref[...]).astype(BF16)
        glu = jnp.dot(x, wg_c[...], preferred_element_type=F32) + bg_ref[0]
        lin = jnp.dot(x, wl_c[...], preferred_element_type=F32) + bl_ref[0]
        glu = jnp.minimum(glu, SWIGLU_LIMIT)
        lin = jnp.clip(lin, -SWIGLU_LIMIT, SWIGLU_LIMIT)
        act = glu * jax.nn.sigmoid(SWIGLU_ALPHA * glu) * (lin + 1.0)
        o_ref[...] = _pack_bf16_pairs(_mm(act, wd_c[...]) + bd_ref[0])


def _experts(xb, mb_expert, mb_active, w_gu, w_down, bg, bl, bd, n_mb):
    d_ff = w_down.shape[1]
    assert d_ff == D_MODEL and w_gu.shape[1:] == (D_MODEL, 2 * d_ff)
    bspec = pl.BlockSpec((1, 1, D_MODEL), lambda m, e, a: (e[m], 0, 0))
    xspec = pl.BlockSpec((EXPERT_BLOCK, D_MODEL // 2), lambda m, e, a: (m, 0))
    wcache = pltpu.VMEM((D_MODEL, D_MODEL), BF16)
    return pl.pallas_call(
        _expert_kernel,
        grid_spec=pltpu.PrefetchScalarGridSpec(
            num_scalar_prefetch=2,
            grid=(n_mb,),
            in_specs=[xspec,
                      pl.BlockSpec((1, D_MODEL, 2 * d_ff), lambda m, e, a: (e[m], 0, 0)),
                      pl.BlockSpec((1, d_ff, D_MODEL), lambda m, e, a: (e[m], 0, 0)),
                      bspec, bspec, bspec],
            out_specs=xspec,
            scratch_shapes=[wcache, wcache, wcache]),
        out_shape=jax.ShapeDtypeStruct(xb.shape, U32),
        compiler_params=pltpu.CompilerParams(dimension_semantics=("arbitrary",),
                                             vmem_limit_bytes=VMEM_LIMIT),
        name="experts",
    )(mb_expert, mb_active, xb, w_gu, w_down, bg, bl, bd)


def _combine_kernel(y4_ref, gate_ref, x1_ref, nw_ref, o_ref):
    g = gate_ref[...]
    y = x1_ref[...]
    for k in range(TOP_K):
        y = y + g[:, k:k + 1] * _unpack_bf16_pairs(y4_ref[k])
    o_ref[...] = y * lax.rsqrt(jnp.mean(y * y, axis=-1, keepdims=True) + EPS) * nw_ref[...]


def _combine(y4, gate_t, x1, norm_w):
    t = x1.shape[0]
    tm = min(512, t)
    return pl.pallas_call(
        _combine_kernel,
        grid=(t // tm,),
        in_specs=[pl.BlockSpec((TOP_K, tm, D_MODEL // 2), lambda i: (0, i, 0)),
                  pl.BlockSpec((tm, TOP_K), lambda i: (i, 0)),
                  pl.BlockSpec((tm, D_MODEL), lambda i: (i, 0)),
                  pl.BlockSpec((1, D_MODEL), lambda i: (0, 0))],
        out_specs=pl.BlockSpec((tm, D_MODEL), lambda i: (i, 0)),
        out_shape=jax.ShapeDtypeStruct((t, D_MODEL), F32),
        compiler_params=pltpu.CompilerParams(dimension_semantics=("parallel",)),
        name="combine",
    )(y4, gate_t, x1, norm_w.reshape(1, D_MODEL))


def _moe(x1, hn, logits_t, w_gu, b_gu, w_down, b_down, norm_final):
    t = x1.shape[0]
    tt = min(512, t)
    eidx, gate, rank, base, cnt = _route(logits_t, tt)
    cnt = cnt[:, :, 0].astype(I32)
    dest, slot_tok, mb_expert, mb_active, n_mb = _routing_plan(cnt, eidx, rank, tt)
    bg = b_gu[:, None, 0::2]
    bl = b_gu[:, None, 1::2]
    xb = _sc_gather(hn, slot_tok)
    yb = _experts(xb, mb_expert, mb_active, w_gu, w_down, bg, bl, b_down[:, None, :], n_mb)
    y4 = _sc_gather(yb, dest.reshape(-1)).reshape(TOP_K, t, D_MODEL // 2)
    return _combine(y4, gate.T, x1, norm_final)


def kernel(x, norm_mix, w_in, gdn_conv, gdn_A_log, gdn_dt_bias, gdn_norm, rwkv_mu, rwkv_w0, rwkv_w2, rwkv_a0, rwkv_a2, rwkv_g2, rwkv_k_k, rwkv_k_a, rwkv_r_k, rwkv_ln_w, rwkv_ln_b, proj_a, proj_b, w_out, norm_ffn, router_w, router_b, w_gate_up, b_gate_up, w_down, b_down, norm_final):
    bsz, seq, d = x.shape
    depth = w_in.shape[0]
    x2 = x.reshape(bsz * seq, d)
    out = None
    for l in range(depth):
        w = w_in[l]
        w_main = jnp.concatenate([w[:, 0:4096], w[:, 4112:7184], w[:, 7440:9488]], axis=1).astype(BF16)
        w_aux = jnp.concatenate([w[:, 7184:7440], w[:, 4096:4112],
                                 jnp.zeros((d, AUX_COLS - LORA_COLS - 16), w.dtype)], axis=1).astype(BF16)
        p, p_aux = _in_proj(x2, norm_mix[l], w_main, w_aux)
        ya, yb = _mixers(
            p, p_aux, (gdn_conv[l], gdn_A_log[l], gdn_dt_bias[l], gdn_norm[l]),
            (rwkv_mu[l], rwkv_w0[l], rwkv_w2[l], rwkv_a0[l], rwkv_a2[l], rwkv_g2[l], rwkv_k_k[l], rwkv_k_a[l],
             rwkv_r_k[l], rwkv_ln_w[l], rwkv_ln_b[l]), bsz, seq)
        x1, hn, logits_t = _merge(x2, ya, yb, p, proj_a[l], proj_b[l], w_out[l], norm_ffn[l],
                                  router_w[l], router_b[l])
        assert l == depth - 1, "only the final layer's residual is fused with the output norm"
        out = _moe(x1, hn, logits_t, w_gate_up[l], b_gate_up[l], w_down[l], b_down[l], norm_final)
    return out.reshape(bsz, seq, d)
```
